```python
import jax, jax.numpy as jnp
from jax import lax
import numpy as np

D_MODEL = 1024
BATCH = 32
SEQ = 256
DEPTH = 2
DEC_BATCH = 2
DEC_SEQ = 4096
PAST_LEN = 512

GRID_W = 64
H_HGRN = 4
DK_HGRN = 64
DV_HGRN = 64
HGRN_CHUNK = 32
H_MLA = 8
Q_RANK = 384
KV_RANK = 256
D_NOPE = 64
D_ROPE = 32
D_V = 64
D_QK = D_NOPE + D_ROPE
ROPE_THETA = 10000.0
Q_BLOCK = 128
G_FNET = 4
C_FNET = 64
W_HGRN = H_HGRN * DV_HGRN
W_MLA = H_MLA * D_V
W_FNET = G_FNET * C_FNET
MIX_WIDTH = W_HGRN + W_MLA + W_FNET
IN_SIZES = (H_HGRN * DK_HGRN, H_HGRN * DK_HGRN, H_HGRN * DK_HGRN, W_HGRN, W_HGRN, Q_RANK, KV_RANK, D_ROPE, W_FNET)
IN_WIDTH = sum(IN_SIZES)
D_FF = 2816
N_MOD = 9
EPS = 1e-6

kernel_name = 'hybrid_hgrn2_mla_fnet_diffusion_step'


def rmsnorm(x, g):
    xf = x.astype(jnp.float32)
    y = xf * lax.rsqrt(jnp.mean(xf * xf, axis=-1, keepdims=True) + EPS)
    return (y * g.astype(jnp.float32)).astype(x.dtype)


def swiglu(h, w_gu, w_down):
    gate, up = jnp.split(h @ w_gu, 2, axis=-1)
    return (jax.nn.silu(gate) * up) @ w_down


def axial_rope_tables(L):
    rows = L // GRID_W
    row = jnp.repeat(jnp.arange(rows, dtype=jnp.float32), GRID_W)
    col = jnp.tile(jnp.arange(GRID_W, dtype=jnp.float32), rows)
    pos = jnp.stack([row, col], axis=-1)
    n_freq = D_ROPE // 4
    freq = ROPE_THETA ** (-jnp.arange(n_freq, dtype=jnp.float32) / n_freq)
    ang = pos[:, :, None] * freq
    return jnp.cos(ang), jnp.sin(ang)


def apply_axial_rope(x, cos, sin):
    shp = x.shape
    xr = x.astype(jnp.float32).reshape(shp[:-1] + (2, 2, D_ROPE // 4))
    x1, x2 = xr[..., 0, :], xr[..., 1, :]
    out = jnp.stack([x1 * cos - x2 * sin, x2 * cos + x1 * sin], axis=-2)
    return out.reshape(shp).astype(x.dtype)


def block_attention(q, k, v):
    B, Lq, H, E = q.shape
    nb = Lq // Q_BLOCK
    scale = E ** -0.5
    qb = q.reshape(B, nb, Q_BLOCK, H, E).transpose(1, 0, 2, 3, 4)

    def attend(qi):
        s = jnp.einsum('bqhe,bkhe->bhqk', qi, k).astype(jnp.float32) * scale
        pr = jax.nn.softmax(s, axis=-1).astype(v.dtype)
        return jnp.einsum('bhqk,bkhd->bqhd', pr, v)

    o = lax.map(attend, qb)
    return o.transpose(1, 0, 2, 3, 4).reshape(B, Lq, H, v.shape[-1])


def hgrn_gates(f_raw, lb):
    x = f_raw.astype(jnp.float32)
    lbf = lb.astype(jnp.float32)
    g = jnp.logaddexp(jnp.log(lbf), jnp.log1p(-lbf) + jax.nn.log_sigmoid(x))
    return g, -jnp.expm1(g)


def hgrn_scan(q, k, v, g, s0):
    B, L, H, DK = q.shape
    DV = v.shape[-1]
    n = L // HGRN_CHUNK

    def chunks(a):
        return a.astype(jnp.float32).reshape(B, n, HGRN_CHUNK, H, a.shape[-1]).transpose(1, 0, 3, 2, 4)

    mask = jnp.tril(jnp.ones((HGRN_CHUNK, HGRN_CHUNK), dtype=bool))[:, :, None]

    def step(S, inp):
        qc, kc, vc, gc = inp
        b = jnp.cumsum(gc, axis=2)
        o_inter = jnp.einsum('bhtk,bhkv->bhtv', qc * jnp.exp(b), S)
        diff = b[:, :, :, None, :] - b[:, :, None, :, :]
        decay = jnp.exp(jnp.where(mask, diff, -jnp.inf))
        scores = jnp.einsum('bhtsk,bhsk->bhts', decay * qc[:, :, :, None, :], kc)
        o_intra = jnp.einsum('bhts,bhsv->bhtv', scores, vc)
        b_last = b[:, :, -1:, :]
        S_new = jnp.exp(b_last[:, :, 0, :])[..., None] * S + jnp.einsum('bhsk,bhsv->bhkv', kc * jnp.exp(b_last - b), vc)
        return S_new, o_inter + o_intra

    S_fin, o = lax.scan(step, s0.astype(jnp.float32), (chunks(q), chunks(k), chunks(v), chunks(g)))
    return o.transpose(1, 0, 3, 2, 4).reshape(B, L, H, DV), S_fin


def mla_query(c_q, l, p, rope):
    B, L, _ = c_q.shape
    q = (rmsnorm(c_q, p['mla_q_norm_g'][l]) @ p['mla_w_q_up'][l]).reshape(B, L, H_MLA, D_QK)
    q = rmsnorm(q, p['mla_qk_norm_g'][l, 0])
    if rope is not None:
        cos, sin = rope
        q = jnp.concatenate([q[..., :D_NOPE], apply_axial_rope(q[..., D_NOPE:], cos[:, None], sin[:, None])], axis=-1)
    return q


def mla_keys(ckv_n, k_r, l, p, rope):
    B, L, _ = ckv_n.shape
    kv = (ckv_n @ p['mla_w_kv_up'][l]).reshape(B, L, H_MLA, D_NOPE + D_V)
    k_nope, v = kv[..., :D_NOPE], kv[..., D_NOPE:]
    k = jnp.concatenate([k_nope, jnp.broadcast_to(k_r[:, :, None, :], (B, L, H_MLA, D_ROPE)).astype(k_nope.dtype)], axis=-1)
    k = rmsnorm(k, p['mla_qk_norm_g'][l, 1])
    if rope is not None:
        cos, sin = rope
        k = jnp.concatenate([k[..., :D_NOPE], apply_axial_rope(k[..., D_NOPE:], cos[:, None], sin[:, None])], axis=-1)
    return k, v


def mixer(h, l, p, lb, cache, rope):
    B, L, _ = h.shape
    u = h @ p['w_in'][l]
    offsets = [int(o) for o in np.cumsum(IN_SIZES)[:-1]]
    q_h, f_fw, f_bw, v_h, gate_h, c_q, c_kv, k_r, u_f = jnp.split(u, offsets, axis=-1)

    def heads(a):
        return a.reshape(B, L, H_HGRN, -1)
    qh = jax.nn.silu(heads(q_h))
    vh = heads(v_h)
    g_f, k_f = hgrn_gates(heads(f_fw), lb[0].reshape(H_HGRN, DK_HGRN))
    g_b, k_b = hgrn_gates(heads(f_bw), lb[1].reshape(H_HGRN, DK_HGRN))
    if cache is None:
        s_f0 = jnp.zeros((B, H_HGRN, DK_HGRN, DV_HGRN), jnp.float32)
        s_b0 = s_f0
    else:
        s_f0, s_b0 = cache[2], cache[3]
    o_f, s_f = hgrn_scan(qh, k_f, vh, g_f, s_f0)
    o_b, s_b = hgrn_scan(jnp.flip(qh, 1), jnp.flip(k_b, 1), jnp.flip(vh, 1), jnp.flip(g_b, 1), s_b0)
    o_h = (o_f + jnp.flip(o_b, 1)).astype(h.dtype)
    o_hgrn = (rmsnorm(o_h, p['hgrn_norm_g'][l]) * jax.nn.silu(heads(gate_h))).reshape(B, L, W_HGRN)

    ckv_n = rmsnorm(c_kv, p['mla_kv_norm_g'][l])
    q = mla_query(c_q, l, p, rope)
    k, v = mla_keys(ckv_n, k_r, l, p, rope)
    if cache is not None:
        k_c, v_c = mla_keys(cache[0], cache[1], l, p, None)
        k = jnp.concatenate([k_c.astype(k.dtype), k], axis=1)
        v = jnp.concatenate([v_c.astype(v.dtype), v], axis=1)
    o_mla = rmsnorm(block_attention(q, k, v).reshape(B, L, W_MLA), p['mla_out_norm_g'][l])

    uf = u_f.astype(jnp.float32).reshape(B, L, G_FNET, C_FNET)
    spec = jnp.fft.fft2(uf, axes=(1, 3), norm='ortho').real.astype(h.dtype)
    o_fn = jnp.einsum('blgc,gcd->blgd', spec, p['fnet_w'][l]).reshape(B, L, W_FNET)
    o_fnet = rmsnorm(o_fn, p['fnet_norm_g'][l])

    out = jnp.concatenate([o_hgrn, o_mla, o_fnet], axis=-1) @ p['w_out'][l]
    return out, (ckv_n, k_r, s_f, s_b)


def layer_block(x, mod, l, p, lb, cache, rope):
    m = jnp.split(mod, N_MOD, axis=-1)
    h = rmsnorm(x, p['norm_g'][l, 0]) * (1 + m[1]) + m[0]
    x = x + 0.5 * m[2] * swiglu(h, p['ffn_w_gu'][l, 0], p['ffn_w_down'][l, 0])
    h = rmsnorm(x, p['norm_g'][l, 1]) * (1 + m[4]) + m[3]
    o, ctx = mixer(h, l, p, lb, cache, rope)
    x = x + m[5] * o
    h = rmsnorm(x, p['norm_g'][l, 2]) * (1 + m[7]) + m[6]
    x = x + 0.5 * m[8] * swiglu(h, p['ffn_w_gu'][l, 1], p['ffn_w_down'][l, 1])
    return x, ctx


def setup_inputs(seed: int = 0) -> dict:
    key = jax.random.key(seed)
    ks = jax.random.split(key, 24)
    f32 = jnp.float32

    def nrm(k, shape, scale):
        return jax.random.normal(k, shape, f32) * scale

    def gain(k, shape):
        return 1.0 + 0.05 * jax.random.normal(k, shape, f32)

    return {
        'x_prompt': nrm(ks[0], (BATCH, SEQ, D_MODEL), 1.0),
        'x_sample': nrm(ks[1], (DEC_BATCH, DEC_SEQ, D_MODEL), 1.0),
        'cache_ckv': nrm(ks[2], (DEC_BATCH, DEPTH, PAST_LEN, KV_RANK), 1.0),
        'cache_krope': nrm(ks[3], (DEC_BATCH, DEPTH, PAST_LEN, D_ROPE), 1.0),
        'state_hgrn': nrm(ks[4], (DEC_BATCH, DEPTH, 2, H_HGRN, DK_HGRN, DV_HGRN), 0.5),
        'c': nrm(ks[5], (DEC_BATCH, D_MODEL), 1.0),
        'c_ctx': nrm(ks[6], (D_MODEL,), 1.0),
        'ada_w': nrm(ks[7], (DEPTH, D_MODEL, N_MOD * D_MODEL), 0.5 * D_MODEL ** -0.5),
        'ada_b': nrm(ks[8], (DEPTH, N_MOD * D_MODEL), 0.01),
        'norm_g': gain(ks[9], (DEPTH, 3, D_MODEL)),
        'ffn_w_gu': nrm(ks[10], (DEPTH, 2, D_MODEL, 2 * D_FF), D_MODEL ** -0.5),
        'ffn_w_down': nrm(ks[11], (DEPTH, 2, D_FF, D_MODEL), D_FF ** -0.5),
        'w_in': nrm(ks[12], (DEPTH, D_MODEL, IN_WIDTH), D_MODEL ** -0.5),
        'hgrn_lb': nrm(ks[13], (DEPTH, 2, H_HGRN * DK_HGRN), 0.5),
        'hgrn_norm_g': gain(ks[14], (DEPTH, DV_HGRN)),
        'mla_q_norm_g': gain(ks[15], (DEPTH, Q_RANK)),
        'mla_w_q_up': nrm(ks[16], (DEPTH, Q_RANK, H_MLA * D_QK), Q_RANK ** -0.5),
        'mla_kv_norm_g': gain(ks[17], (DEPTH, KV_RANK)),
        'mla_w_kv_up': nrm(ks[18], (DEPTH, KV_RANK, H_MLA * (D_NOPE + D_V)), KV_RANK ** -0.5),
        'mla_qk_norm_g': gain(ks[19], (DEPTH, 2, D_QK)),
        'mla_out_norm_g': gain(ks[20], (DEPTH, W_MLA)),
        'fnet_w': nrm(ks[21], (DEPTH, G_FNET, C_FNET, C_FNET), C_FNET ** -0.5),
        'fnet_norm_g': gain(ks[22], (DEPTH, W_FNET)),
        'w_out': nrm(ks[23], (DEPTH, MIX_WIDTH, D_MODEL), MIX_WIDTH ** -0.5),
    }


def reference(x_prompt, x_sample, cache_ckv, cache_krope, state_hgrn, c, c_ctx, ada_w, ada_b, norm_g,
              ffn_w_gu, ffn_w_down, w_in, hgrn_lb, hgrn_norm_g, mla_q_norm_g, mla_w_q_up, mla_kv_norm_g,
              mla_w_kv_up, mla_qk_norm_g, mla_out_norm_g, fnet_w, fnet_norm_g, w_out):
    p = dict(norm_g=norm_g, ffn_w_gu=ffn_w_gu, ffn_w_down=ffn_w_down, w_in=w_in, hgrn_norm_g=hgrn_norm_g,
             mla_q_norm_g=mla_q_norm_g, mla_w_q_up=mla_w_q_up, mla_kv_norm_g=mla_kv_norm_g,
             mla_w_kv_up=mla_w_kv_up, mla_qk_norm_g=mla_qk_norm_g, mla_out_norm_g=mla_out_norm_g,
             fnet_w=fnet_w, fnet_norm_g=fnet_norm_g, w_out=w_out)
    lbs = jnp.cumsum(jax.nn.softmax(hgrn_lb.astype(jnp.float32), axis=0), axis=0)
    lbs = lbs - lbs[:1]
    rope = axial_rope_tables(x_sample.shape[1])

    x_ctx = x_prompt
    x_lat = x_sample
    ckv_list, kr_list, st_list = [], [], []
    for l in range(DEPTH):
        mod_ctx = (jax.nn.silu(c_ctx) @ ada_w[l] + ada_b[l])[None, None, :]
        x_ctx, (ckv, kr, s_f, s_b) = layer_block(x_ctx, mod_ctx, l, p, lbs[l], None, None)
        ckv_list.append(ckv)
        kr_list.append(kr)
        st_list.append(jnp.stack([s_f, s_b], axis=1))

        mod_lat = (jax.nn.silu(c) @ ada_w[l] + ada_b[l])[:, None, :]
        cache_l = (cache_ckv[:, l], cache_krope[:, l], state_hgrn[:, l, 0], state_hgrn[:, l, 1])
        x_lat, _ = layer_block(x_lat, mod_lat, l, p, lbs[l], cache_l, rope)

    new_cache_ckv = jnp.stack(ckv_list, axis=1)
    new_cache_krope = jnp.stack(kr_list, axis=1)
    new_state_hgrn = jnp.stack(st_list, axis=1)
    return (x_ctx, x_lat, new_cache_ckv, new_cache_krope, new_state_hgrn)
```

```python
import functools

import numpy as np
import jax
import jax.numpy as jnp
from jax import lax
from jax.experimental import pallas as pl
from jax.experimental.pallas import tpu as pltpu

F32 = jnp.float32
BF16 = jnp.bfloat16
HIGHEST = lax.Precision.HIGHEST

D = 1024
N_CTX_SEQ, L_CTX = 32, 256
N_LAT_SEQ, L_LAT = 2, 4096
T_CTX = N_CTX_SEQ * L_CTX
T_LAT = N_LAT_SEQ * L_LAT
T_ALL = T_CTX + T_LAT
DEPTH = 2
PAST = 512
GRID_W = 64
N_GROUPS = 1 + N_LAT_SEQ
N_MOD = 9
EPS = 1e-6

H_HGRN, DK_HGRN, DV_HGRN = 4, 64, 64
W_HGRN = H_HGRN * DV_HGRN
CHUNK = 32
H_MLA, Q_RANK, KV_RANK = 8, 384, 256
D_NOPE, D_ROPE, D_V = 64, 32, 64
D_QK = D_NOPE + D_ROPE
HEAD_PAD = 128
W_MLA = H_MLA * D_V
G_FNET, C_FNET = 4, 64
W_FNET = G_FNET * C_FNET
D_FF = 2816
FF_CHUNK = 256
ROPE_THETA = 10000.0

IN_HG = 5 * W_HGRN
IN_ARR = IN_HG + Q_RANK + KV_RANK + HEAD_PAD + W_FNET

TM = 512
N_TILES = T_ALL // TM
N_CTX_TILES = T_CTX // TM
LAT_TILES_PER_SEQ = L_LAT // TM
T_BLK = 256
TQ = 256

VMEM_LIMIT = 56 * 1024 * 1024


def _cparams(sem):
    return pltpu.CompilerParams(dimension_semantics=sem, vmem_limit_bytes=VMEM_LIMIT)


def _tile_group(i):
    return (i >= N_CTX_TILES).astype(jnp.int32) + (i >= N_CTX_TILES + LAT_TILES_PER_SEQ).astype(jnp.int32)


def _silu(x):
    return x * (1.0 / (1.0 + jnp.exp(-x)))


def _rms_rows(x, g):
    return x * lax.rsqrt(jnp.mean(x * x, axis=-1, keepdims=True) + EPS) * g


def _dot(a, b):
    return jnp.dot(a, b, preferred_element_type=F32)


def _dot_nt(a, b):
    return lax.dot_general(a, b, (((1,), (1,)), ((), ())), preferred_element_type=F32)


def _dot_tn(a, b):
    return lax.dot_general(a, b, (((0,), (0,)), ((), ())), preferred_element_type=F32)


def _dot_exact(a, b):
    return jnp.dot(a, b, preferred_element_type=F32, precision=HIGHEST)


def _mods_kernel(c_ref, w_ref, b_ref, o_ref):
    a = _silu(c_ref[...]).astype(BF16)
    o_ref[...] = _dot(a, w_ref[...].astype(BF16)) + b_ref[...]


def _mods(cond8, ada_w, ada_b):
    tn = 1024
    return pl.pallas_call(
        _mods_kernel,
        grid=(DEPTH, N_MOD * D // tn),
        in_specs=[
            pl.BlockSpec((8, D), lambda l, j: (0, 0)),
            pl.BlockSpec((None, D, tn), lambda l, j: (l, 0, j)),
            pl.BlockSpec((None, 1, tn), lambda l, j: (l, 0, j)),
        ],
        out_specs=pl.BlockSpec((None, 8, tn), lambda l, j: (l, 0, j)),
        out_shape=jax.ShapeDtypeStruct((DEPTH, 8, N_MOD * D), F32),
        compiler_params=_cparams(("arbitrary", "arbitrary")),
        name="ada_mods",
    )(cond8, ada_w, ada_b.reshape(DEPTH, 1, N_MOD * D))


def _mod_spec(layer):
    return pl.BlockSpec((None, None, N_MOD, D), lambda i: (layer, _tile_group(i), 0, 0))


def _resident(shape, index_map):
    return pl.BlockSpec(shape, index_map, pipeline_mode=pl.Buffered(1))


def _ffn_kernel(x_ref, mod_ref, g_ref, wg_ref, wu_ref, wd_ref, o_ref, *, mi):
    x = x_ref[...]
    shift = mod_ref[mi:mi + 1, :]
    scale = mod_ref[mi + 1:mi + 2, :]
    gate = mod_ref[mi + 2:mi + 3, :]
    hb = (_rms_rows(x, g_ref[...]) * (1.0 + scale) + shift).astype(BF16)
    acc = jnp.zeros(x.shape, F32)
    for j in range(D_FF // FF_CHUNK):
        cs = slice(j * FF_CHUNK, (j + 1) * FF_CHUNK)
        a = _silu(_dot(hb, wg_ref[:, cs])) * _dot(hb, wu_ref[:, cs])
        acc = acc + _dot(a.astype(BF16), wd_ref[cs, :])
    o_ref[...] = x + 0.5 * gate * acc


def _ffn(x, mods, norm_g, w_gu, w_down, layer, which):
    mi = 0 if which == 0 else 6
    gi = 0 if which == 0 else 2
    row = pl.BlockSpec((TM, D), lambda i: (i, 0))
    return pl.pallas_call(
        functools.partial(_ffn_kernel, mi=mi),
        grid=(N_TILES,),
        in_specs=[
            row,
            _mod_spec(layer),
            pl.BlockSpec((None, None, 1, D), lambda i: (layer, gi, 0, 0)),
            _resident((None, None, D, D_FF), lambda i: (layer, which, 0, 0)),
            _resident((None, None, D, D_FF), lambda i: (layer, which, 0, 1)),
            _resident((None, None, D_FF, D), lambda i: (layer, which, 0, 0)),
        ],
        out_specs=row,
        out_shape=jax.ShapeDtypeStruct((T_ALL, D), F32),
        compiler_params=_cparams(("arbitrary",)),
        name=f"ffn_l{layer}_{which}",
    )(x, mods, norm_g.reshape(DEPTH, 3, 1, D), w_gu, w_gu, w_down)


def _inproj_kernel(x_ref, mod_ref, g_ref, w_ref, gq_ref, gkv_ref, hg_ref, cq_ref, ckv_ref, kr_ref, uf_ref):
    x = x_ref[...]
    shift = mod_ref[3:4, :]
    scale = mod_ref[4:5, :]
    hb = (_rms_rows(x, g_ref[...]) * (1.0 + scale) + shift).astype(BF16)
    u = _dot(hb, w_ref[...])
    o = 0
    hg_ref[...] = u[:, o:o + IN_HG]
    o += IN_HG
    cq_ref[...] = _rms_rows(u[:, o:o + Q_RANK], gq_ref[...])
    o += Q_RANK
    ckv_ref[...] = _rms_rows(u[:, o:o + KV_RANK], gkv_ref[...])
    o += KV_RANK
    kr_ref[...] = u[:, o:o + HEAD_PAD]
    o += HEAD_PAD
    uf_ref[...] = u[:, o:o + W_FNET]


def _inproj(x, mods, norm_g, w_in_arr, gq, gkv, layer):
    def row(w):
        return pl.BlockSpec((TM, w), lambda i: (i, 0))
    widths = (IN_HG, Q_RANK, KV_RANK, HEAD_PAD, W_FNET)
    return pl.pallas_call(
        _inproj_kernel,
        grid=(N_TILES,),
        in_specs=[
            row(D),
            _mod_spec(layer),
            pl.BlockSpec((None, None, 1, D), lambda i: (layer, 1, 0, 0)),
            _resident((None, D, IN_ARR), lambda i: (layer, 0, 0)),
            pl.BlockSpec((None, 1, Q_RANK), lambda i: (layer, 0, 0)),
            pl.BlockSpec((None, 1, KV_RANK), lambda i: (layer, 0, 0)),
        ],
        out_specs=[row(w) for w in widths],
        out_shape=[jax.ShapeDtypeStruct((T_ALL, w), F32) for w in widths],
        compiler_params=_cparams(("arbitrary",)),
        name=f"inproj_l{layer}",
    )(x, mods, norm_g.reshape(DEPTH, 3, 1, D), w_in_arr, gq.reshape(DEPTH, 1, Q_RANK),
      gkv.reshape(DEPTH, 1, KV_RANK))


N_SEQ = N_CTX_SEQ + N_LAT_SEQ
BLK_PER_LAT = L_LAT // T_BLK
N_HGRN_STEPS = N_CTX_SEQ + N_LAT_SEQ * BLK_PER_LAT


def _hgrn_step_maps(reverse):
    def seq_of(i):
        return jnp.where(i < N_CTX_SEQ, i, N_CTX_SEQ + (i - N_CTX_SEQ) // BLK_PER_LAT)

    def blk_of(i):
        j = (i - N_CTX_SEQ) % BLK_PER_LAT
        if reverse:
            j = BLK_PER_LAT - 1 - j
        lat = N_CTX_SEQ + ((i - N_CTX_SEQ) // BLK_PER_LAT) * BLK_PER_LAT + j
        return jnp.where(i < N_CTX_SEQ, i, lat)
    return seq_of, blk_of


def _hgrn_kernel(*refs, reverse, final):
    if final:
        (q_ref, f_ref, v_ref, lb_ref, s0_ref, tri_ref, tri4_ref, hm_ref, bd_ref,
         of_ref, gate_ref, gn_ref, ones_ref, o_ref, sout_ref, st_scr) = refs
    else:
        (q_ref, f_ref, v_ref, lb_ref, s0_ref, tri_ref, tri4_ref, hm_ref, bd_ref,
         o_ref, sout_ref, st_scr) = refs
    i = pl.program_id(0)
    first = jnp.logical_or(i < N_CTX_SEQ, (i - N_CTX_SEQ) % BLK_PER_LAT == 0)

    @pl.when(first)
    def _():
        st_scr[...] = s0_ref[...]

    lb = lb_ref[...]
    loglb = jnp.log(lb)
    log1mlb = jnp.log(1.0 - lb)
    tri = tri_ref[...]
    n_chunks = T_BLK // CHUNK
    end_row = 0 if reverse else CHUNK - 1
    mid_row = CHUNK // 2 if reverse else CHUNK // 2 - 1
    for c in range(n_chunks):
        cc = n_chunks - 1 - c if reverse else c
        rows = slice(cc * CHUNK, (cc + 1) * CHUNK)
        q = _silu(q_ref[rows, :])
        x = f_ref[rows, :]
        v = v_ref[rows, :]
        y = log1mlb + (jnp.minimum(x, 0.0) - jnp.log(1.0 + jnp.exp(-jnp.abs(x))))
        g = jnp.maximum(loglb, y) + jnp.log(1.0 + jnp.exp(-jnp.abs(loglb - y)))
        kk = 1.0 - jnp.exp(g)
        b = _dot_exact(tri, g)
        b_end = b[end_row:end_row + 1, :]
        b_mid = b[mid_row:mid_row + 1, :]
        q_in = (q * jnp.exp(b)).astype(BF16)
        q_t = (q * jnp.exp(b - b_mid)).astype(BF16)
        k_t = kk * jnp.exp(b_mid - b)
        k_e = (kk * jnp.exp(b_end - b)).astype(BF16)
        hm = hm_ref[...]
        k_bd = (jnp.concatenate([k_t] * H_HGRN, axis=0) * hm).astype(BF16)
        v_bd = (jnp.concatenate([v] * H_HGRN, axis=0) * hm).astype(BF16)
        sc = _dot_nt(q_t, k_bd) * tri4_ref[...]
        st = st_scr[...]
        o = _dot(sc.astype(BF16), v_bd) + _dot_nt(q_in, st.astype(BF16))
        o_ref[rows, :] = o
        st_scr[...] = st * jnp.exp(b_end) + _dot_tn(v.astype(BF16), k_e) * bd_ref[...]
    sout_ref[...] = st_scr[...]
    if final:
        o = o_ref[...] + of_ref[...]
        ms = _dot_exact(o * o, ones_ref[...])
        o_ref[...] = o * lax.rsqrt(ms + EPS) * gn_ref[...] * _silu(gate_ref[...])


def _hgrn_consts(reverse):
    t = np.arange(CHUNK)
    tri = (t[:, None] <= t[None, :]) if reverse else (t[:, None] >= t[None, :])
    tri = tri.astype(np.float32)
    tri4 = np.tile(tri, (1, H_HGRN))
    r = np.arange(H_HGRN * CHUNK)
    lane = np.arange(W_HGRN)
    hm = (r[:, None] // CHUNK == lane[None, :] // DK_HGRN).astype(np.float32)
    bd = (lane[:, None] // DV_HGRN == lane[None, :] // DK_HGRN).astype(np.float32)
    return jnp.asarray(tri), jnp.asarray(tri4), jnp.asarray(hm), jnp.asarray(bd)


def _hgrn(hg, lb_row, s0t, reverse, final=None):
    seq_of, blk_of = _hgrn_step_maps(reverse)
    tri, tri4, hm, bd = _hgrn_consts(reverse)

    def col(cb):
        return pl.BlockSpec((T_BLK, W_HGRN), lambda i: (blk_of(i), cb))

    def const(shape):
        return pl.BlockSpec(shape, lambda i: (0,) * len(shape))
    f_col = 2 if reverse else 1
    in_specs = [col(0), col(f_col), col(3), const((1, W_HGRN)),
                pl.BlockSpec((None, W_HGRN, W_HGRN), lambda i: (seq_of(i), 0, 0)),
                const(tri.shape), const(tri4.shape), const(hm.shape), const(bd.shape)]
    args = [hg, hg, hg, lb_row, s0t, tri, tri4, hm, bd]
    if final is not None:
        o_fwd, gn_row = final
        ones_bd = bd / DV_HGRN
        in_specs += [pl.BlockSpec((T_BLK, W_HGRN), lambda i: (blk_of(i), 0)), col(4),
                     const((1, W_HGRN)), const(bd.shape)]
        args += [o_fwd, hg, gn_row, ones_bd]
    return pl.pallas_call(
        functools.partial(_hgrn_kernel, reverse=reverse, final=final is not None),
        grid=(N_HGRN_STEPS,),
        in_specs=in_specs,
        out_specs=[pl.BlockSpec((T_BLK, W_HGRN), lambda i: (blk_of(i), 0)),
                   pl.BlockSpec((None, W_HGRN, W_HGRN), lambda i: (seq_of(i), 0, 0))],
        out_shape=[jax.ShapeDtypeStruct((T_ALL, W_HGRN), F32),
                   jax.ShapeDtypeStruct((N_SEQ, W_HGRN, W_HGRN), F32)],
        scratch_shapes=[pltpu.VMEM((W_HGRN, W_HGRN), F32)],
        compiler_params=_cparams(("arbitrary",)),
        name="hgrn_bwd" if reverse else "hgrn_fwd",
    )(*args)


def _states_to_bd_t(s):
    n = s.shape[0]
    st = jnp.swapaxes(s, -1, -2)
    eye = jnp.eye(H_HGRN, dtype=s.dtype)
    return jnp.einsum("nhvk,hg->nhvgk", st, eye).reshape(n, W_HGRN, W_HGRN)


def _bd_t_to_states(sbd):
    n = sbd.shape[0]
    s5 = sbd.reshape(n, H_HGRN, DV_HGRN, H_HGRN, DK_HGRN)
    diag = jnp.stack([s5[:, h, :, h, :] for h in range(H_HGRN)], axis=1)
    return jnp.swapaxes(diag, -1, -2)


def _rope(x, cos, sin_a, sin_b):
    return x * cos + pltpu.roll(x, HEAD_PAD - D_ROPE // 4, axis=1) * sin_a + pltpu.roll(x, D_ROPE // 4, axis=1) * sin_b


def _head_norm(x, g):
    return x * lax.rsqrt(jnp.sum(x * x, axis=-1, keepdims=True) * (1.0 / D_QK) + EPS) * g


def _qprep_kernel(cq_ref, w_ref, g_ref, cos_ref, sa_ref, sb_ref, q_ref):
    qa = _dot(cq_ref[...].astype(BF16), w_ref[...])
    g = g_ref[...]
    cos, sa, sb = cos_ref[...], sa_ref[...], sb_ref[...]
    scale = D_QK ** -0.5
    for h in range(H_MLA):
        hs = slice(h * HEAD_PAD, (h + 1) * HEAD_PAD)
        q = _rope(_head_norm(qa[:, hs], g), cos, sa, sb) * scale
        q_ref[:, hs] = q.astype(BF16)


def _rope_block(i):
    return jnp.where(i < N_CTX_TILES, 0, 1 + (i - N_CTX_TILES) % LAT_TILES_PER_SEQ)


def _qprep(cqn, wq_arr, gq128, tabs, layer):
    tab = pl.BlockSpec((None, TM, HEAD_PAD), lambda i: (_rope_block(i), 0, 0))
    return pl.pallas_call(
        _qprep_kernel,
        grid=(N_TILES,),
        in_specs=[
            pl.BlockSpec((TM, Q_RANK), lambda i: (i, 0)),
            _resident((None, Q_RANK, H_MLA * HEAD_PAD), lambda i: (layer, 0, 0)),
            pl.BlockSpec((None, 1, HEAD_PAD), lambda i: (layer, 0, 0)),
            tab, tab, tab,
        ],
        out_specs=pl.BlockSpec((TM, H_MLA * HEAD_PAD), lambda i: (i, 0)),
        out_shape=jax.ShapeDtypeStruct((T_ALL, H_MLA * HEAD_PAD), BF16),
        compiler_params=_cparams(("arbitrary",)),
        name=f"mla_q_l{layer}",
    )(cqn, wq_arr, gq128, *tabs)


def _kv_body(ckv, kr, wk_ref, wv_ref, g_ref, cos, sa, sb, k_ref, v_ref):
    cb = ckv.astype(BF16)
    ka = _dot(cb, wk_ref[...])
    v_ref[...] = _dot(cb, wv_ref[...]).astype(BF16)
    g = g_ref[...]
    for h in range(H_MLA):
        hs = slice(h * HEAD_PAD, (h + 1) * HEAD_PAD)
        k = _head_norm(ka[:, hs] + kr, g)
        if cos is not None:
            k = _rope(k, cos, sa, sb)
        k_ref[:, hs] = k.astype(BF16)


def _kvprep_ctx_kernel(ckv_ref, kr_ref, wk_ref, wv_ref, g_ref, k_ref, v_ref):
    _kv_body(ckv_ref[...], kr_ref[...], wk_ref, wv_ref, g_ref, None, None, None, k_ref, v_ref)


def _kvprep_lat_kernel(cckv_ref, ckr_ref, ckv_ref, kr_ref, wk_ref, wv_ref, g_ref, cos_ref, sa_ref, sb_ref,
                       k_ref, v_ref):
    past = pl.program_id(1) == 0
    ckv = jnp.where(past, cckv_ref[...], ckv_ref[...])
    kr = jnp.where(past, ckr_ref[...], kr_ref[...])
    _kv_body(ckv, kr, wk_ref, wv_ref, g_ref, cos_ref[...], sa_ref[...], sb_ref[...], k_ref, v_ref)


def _kv_weight_specs(layer, nargs):
    zero = (lambda *a: (layer, 0, 0))
    return [_resident((None, KV_RANK, H_MLA * HEAD_PAD), zero),
            _resident((None, KV_RANK, H_MLA * HEAD_PAD), zero),
            pl.BlockSpec((None, 1, HEAD_PAD), zero)]


def _kvprep_ctx(ckvn, kr128, wk_arr, wv_arr, gk128, layer):
    wide = H_MLA * HEAD_PAD
    return pl.pallas_call(
        _kvprep_ctx_kernel,
        grid=(N_CTX_TILES,),
        in_specs=[pl.BlockSpec((TM, KV_RANK), lambda i: (i, 0)),
                  pl.BlockSpec((TM, HEAD_PAD), lambda i: (i, 0))] + _kv_weight_specs(layer, 1),
        out_specs=[pl.BlockSpec((TM, wide), lambda i: (i, 0))] * 2,
        out_shape=[jax.ShapeDtypeStruct((T_CTX, wide), BF16)] * 2,
        compiler_params=_cparams(("arbitrary",)),
        name=f"mla_kv_ctx_l{layer}",
    )(ckvn, kr128, wk_arr, wv_arr, gk128)


LK_LAT = PAST + L_LAT
KV_TILES_LAT = LK_LAT // TM


def _kvprep_lat(cache_ckv_l, cache_kr128, ckvn, kr128, wk_arr, wv_arr, gk128, tabs, layer):
    wide = H_MLA * HEAD_PAD
    assert PAST == TM

    def new_rows(b, j):
        return (N_CTX_TILES + b * LAT_TILES_PER_SEQ + jnp.maximum(j - 1, 0), 0)
    tab = pl.BlockSpec((None, TM, HEAD_PAD), lambda b, j: (j, 0, 0))
    return pl.pallas_call(
        _kvprep_lat_kernel,
        grid=(N_LAT_SEQ, KV_TILES_LAT),
        in_specs=[pl.BlockSpec((None, TM, KV_RANK), lambda b, j: (b, 0, 0)),
                  pl.BlockSpec((None, TM, HEAD_PAD), lambda b, j: (b, 0, 0)),
                  pl.BlockSpec((TM, KV_RANK), new_rows),
                  pl.BlockSpec((TM, HEAD_PAD), new_rows)] + _kv_weight_specs(layer, 2) + [tab, tab, tab],
        out_specs=[pl.BlockSpec((None, TM, wide), lambda b, j: (b, j, 0))] * 2,
        out_shape=[jax.ShapeDtypeStruct((N_LAT_SEQ, LK_LAT, wide), BF16)] * 2,
        compiler_params=_cparams(("arbitrary", "arbitrary")),
        name=f"mla_kv_lat_l{layer}",
    )(cache_ckv_l, cache_kr128, ckvn, kr128, wk_arr, wv_arr, gk128, *tabs)


def _attn_kernel(*refs, aliased):
    q_ref, k_ref, v_ref = refs[:3]
    o_ref = refs[-1]
    outs = []
    for h in range(2):
        hs = slice(h * HEAD_PAD, (h + 1) * HEAD_PAD)
        s = _dot_nt(q_ref[:, hs], k_ref[:, hs])
        p = jnp.exp(s - jnp.max(s, axis=-1, keepdims=True))
        l = jnp.sum(p, axis=-1, keepdims=True)
        outs.append(_dot(p.astype(BF16), v_ref[:, hs]) / l)
    o_ref[...] = outs[0] + pltpu.roll(outs[1], D_V, axis=1)


def _attention_ctx(q, k, v):
    pair = 2 * HEAD_PAD
    blk = pl.BlockSpec((L_CTX, pair), lambda b, hp: (b, hp))
    return pl.pallas_call(
        functools.partial(_attn_kernel, aliased=False),
        grid=(N_CTX_SEQ, H_MLA // 2),
        in_specs=[blk, blk, blk],
        out_specs=pl.BlockSpec((L_CTX, 2 * D_V), lambda b, hp: (b, hp)),
        out_shape=jax.ShapeDtypeStruct((T_ALL, W_MLA), F32),
        compiler_params=_cparams(("arbitrary", "arbitrary")),
        name="attn_ctx",
    )(q, k, v)


def _attention_lat(q, k, v, o_all):
    pair = 2 * HEAD_PAD
    nq = L_LAT // TQ
    q0 = T_CTX // TQ

    def qrow(b, hp, qi):
        return (q0 + b * nq + qi, hp)
    kv = pl.BlockSpec((None, LK_LAT, pair), lambda b, hp, qi: (b, 0, hp))
    return pl.pallas_call(
        functools.partial(_attn_kernel, aliased=True),
        grid=(N_LAT_SEQ, H_MLA // 2, nq),
        in_specs=[pl.BlockSpec((TQ, pair), qrow), kv, kv, pl.BlockSpec(memory_space=pl.ANY)],
        out_specs=pl.BlockSpec((TQ, 2 * D_V), qrow),
        out_shape=jax.ShapeDtypeStruct((T_ALL, W_MLA), F32),
        input_output_aliases={3: 0},
        compiler_params=_cparams(("arbitrary", "arbitrary", "arbitrary")),
        name="attn_lat",
    )(q, k, v, o_all)


def _dft_tables(n):
    a = 2.0 * np.pi * np.outer(np.arange(n), np.arange(n)) / n
    return np.cos(a), np.sin(a)


def _fnet_ctx_kernel(u_ref, c_ref, s_ref, re_ref, im_ref):
    u = u_ref[...]
    re_ref[...] = _dot_exact(c_ref[...], u)
    im_ref[...] = -_dot_exact(s_ref[...], u)


def _fnet_pos_ctx(uf):
    c, s = _dft_tables(L_CTX)
    norm = 1.0 / np.sqrt(L_CTX * C_FNET)
    c = jnp.asarray(c * norm, F32)
    s = jnp.asarray(s * norm, F32)
    blk = pl.BlockSpec((L_CTX, W_FNET), lambda b: (b, 0))
    mat = pl.BlockSpec((L_CTX, L_CTX), lambda b: (0, 0))
    return pl.pallas_call(
        _fnet_ctx_kernel,
        grid=(N_CTX_SEQ,),
        in_specs=[blk, mat, mat],
        out_specs=[blk, blk],
        out_shape=[jax.ShapeDtypeStruct((T_CTX, W_FNET), F32)] * 2,
        compiler_params=_cparams(("arbitrary",)),
        name="fnet_pos_ctx",
    )(uf, c, s)


FN_COLS = GRID_W * W_FNET
FN_TILE = 2048


def _fnet_rows_kernel(x_ref, m_ref, tc_ref, ts_ref, re_ref, im_ref):
    w = _dot_exact(m_ref[...], x_ref[...])
    wr, wi = w[:GRID_W], w[GRID_W:]
    tc, ts = tc_ref[...], ts_ref[...]
    re_ref[...] = wr * tc + wi * ts
    im_ref[...] = wi * tc - wr * ts


def _fnet_cols_kernel(re_in, im_in, m_ref, re_ref, im_ref):
    x = jnp.concatenate([re_in[...], im_in[...]], axis=0)
    z = _dot_exact(m_ref[...], x)
    re_ref[...] = z[:GRID_W]
    im_ref[...] = z[GRID_W:]


def _fnet_pos_lat(uf_lat):
    c64, s64 = _dft_tables(GRID_W)
    norm = 1.0 / np.sqrt(L_LAT * C_FNET)
    m_rows = jnp.asarray(np.concatenate([c64, -s64], axis=0) * norm, F32)
    m_cols = jnp.asarray(np.block([[c64, s64], [-s64, c64]]), F32)
    ang = 2.0 * np.pi * np.outer(np.arange(GRID_W), np.arange(GRID_W)) / L_LAT
    tc = jnp.broadcast_to(jnp.asarray(np.cos(ang), F32)[:, :, None], (GRID_W, GRID_W, W_FNET)).reshape(GRID_W, FN_COLS)
    ts = jnp.broadcast_to(jnp.asarray(np.sin(ang), F32)[:, :, None], (GRID_W, GRID_W, W_FNET)).reshape(GRID_W, FN_COLS)
    x = uf_lat.reshape(N_LAT_SEQ, GRID_W, FN_COLS)
    blk = pl.BlockSpec((None, GRID_W, FN_TILE), lambda b, j: (b, 0, j))
    tw = pl.BlockSpec((GRID_W, FN_TILE), lambda b, j: (0, j))
    out = [jax.ShapeDtypeStruct((N_LAT_SEQ, GRID_W, FN_COLS), F32)] * 2
    grid = (N_LAT_SEQ, FN_COLS // FN_TILE)
    wr, wi = pl.pallas_call(
        _fnet_rows_kernel,
        grid=grid,
        in_specs=[blk, pl.BlockSpec((2 * GRID_W, GRID_W), lambda b, j: (0, 0)), tw, tw],
        out_specs=[blk, blk],
        out_shape=out,
        compiler_params=_cparams(("arbitrary", "arbitrary")),
        name="fnet_rows_lat",
    )(x, m_rows, tc, ts)

    def swap(a):
        return a.reshape(N_LAT_SEQ, GRID_W, GRID_W, W_FNET).transpose(0, 2, 1, 3).reshape(N_LAT_SEQ, GRID_W, FN_COLS)
    zr, zi = pl.pallas_call(
        _fnet_cols_kernel,
        grid=grid,
        in_specs=[blk, blk, pl.BlockSpec((2 * GRID_W, 2 * GRID_W), lambda b, j: (0, 0))],
        out_specs=[blk, blk],
        out_shape=out,
        compiler_params=_cparams(("arbitrary", "arbitrary")),
        name="fnet_cols_lat",
    )(swap(wr), swap(wi), m_cols)
    return zr.reshape(T_LAT, W_FNET), zi.reshape(T_LAT, W_FNET)


def _fnet_mix_kernel(re_ref, im_ref, cc_ref, sc_ref, w_ref, g_ref, o_ref):
    spec = _dot_exact(re_ref[...], cc_ref[...]) + _dot_exact(im_ref[...], sc_ref[...])
    o = _dot(spec.astype(BF16), w_ref[...])
    o_ref[...] = _rms_rows(o, g_ref[...])


def _fnet_mix(p_re, p_im, w_bd, g_row):
    c, s = _dft_tables(C_FNET)
    eye = np.eye(G_FNET)
    cc = jnp.asarray(np.kron(eye, c), F32)
    sc = jnp.asarray(np.kron(eye, s), F32)
    row = pl.BlockSpec((TM, W_FNET), lambda i: (i, 0))
    mat = pl.BlockSpec((W_FNET, W_FNET), lambda i: (0, 0))
    return pl.pallas_call(
        _fnet_mix_kernel,
        grid=(N_TILES,),
        in_specs=[row, row, mat, mat, mat, pl.BlockSpec((1, W_FNET), lambda i: (0, 0))],
        out_specs=row,
        out_shape=jax.ShapeDtypeStruct((T_ALL, W_FNET), F32),
        compiler_params=_cparams(("arbitrary",)),
        name="fnet_mix",
    )(p_re, p_im, cc, sc, w_bd, g_row)


def _outproj_kernel(x_ref, mod_ref, oh_ref, oa_ref, of_ref, ga_ref, w_ref, o_ref):
    gate = mod_ref[5:6, :]
    oa = _rms_rows(oa_ref[...], ga_ref[...])
    o = (_dot(oh_ref[...].astype(BF16), w_ref[0:W_HGRN, :])
         + _dot(oa.astype(BF16), w_ref[W_HGRN:W_HGRN + W_MLA, :])
         + _dot(of_ref[...].astype(BF16), w_ref[W_HGRN + W_MLA:, :]))
    o_ref[...] = x_ref[...] + gate * o


def _outproj(x, mods, o_hgrn, o_att, o_fnet, g_att, w_out, layer):
    def row(w):
        return pl.BlockSpec((TM, w), lambda i: (i, 0))
    return pl.pallas_call(
        _outproj_kernel,
        grid=(N_TILES,),
        in_specs=[row(D), _mod_spec(layer), row(W_HGRN), row(W_MLA), row(W_FNET),
                  pl.BlockSpec((None, 1, W_MLA), lambda i: (layer, 0, 0)),
                  _resident((None, D, D), lambda i: (layer, 0, 0))],
        out_specs=row(D),
        out_shape=jax.ShapeDtypeStruct((T_ALL, D), F32),
        compiler_params=_cparams(("arbitrary",)),
        name=f"outproj_l{layer}",
    )(x, mods, o_hgrn, o_att, o_fnet, g_att.reshape(DEPTH, 1, W_MLA), w_out)


def _pad_heads(w, d_in, d_head):
    w = w.reshape(DEPTH, d_in, H_MLA, d_head)
    w = jnp.pad(w, ((0, 0), (0, 0), (0, 0), (0, HEAD_PAD - d_head)))
    return w.reshape(DEPTH, d_in, H_MLA * HEAD_PAD)


def _rope_tables():
    n_freq = D_ROPE // 4
    t = np.arange(L_LAT)
    pos = np.stack([t // GRID_W, t % GRID_W], axis=-1).astype(np.float32)
    freq = (np.float32(ROPE_THETA) ** (-np.arange(n_freq, dtype=np.float32) / n_freq)).astype(np.float32)
    ang = (pos[:, :, None] * freq).astype(np.float32)
    cos = np.ones((L_LAT, HEAD_PAD), np.float32)
    sin_a = np.zeros((L_LAT, HEAD_PAD), np.float32)
    sin_b = np.zeros((L_LAT, HEAD_PAD), np.float32)
    for ax in range(2):
        base = D_NOPE + ax * 2 * n_freq
        c, s = np.cos(ang[:, ax, :]), np.sin(ang[:, ax, :])
        cos[:, base:base + n_freq] = c
        cos[:, base + n_freq:base + 2 * n_freq] = c
        sin_a[:, base:base + n_freq] = -s
        sin_b[:, base + n_freq:base + 2 * n_freq] = s
    def blocks(tab, ident):
        full = np.concatenate([np.full((TM, HEAD_PAD), ident, np.float32), tab], axis=0)
        return jnp.asarray(full.reshape(1 + LAT_TILES_PER_SEQ, TM, HEAD_PAD))
    return blocks(cos, 1.0), blocks(sin_a, 0.0), blocks(sin_b, 0.0)


def _place_gain(g):
    return jnp.pad(g, ((0, 0), (0, HEAD_PAD - D_QK))).reshape(DEPTH, 1, HEAD_PAD)


def kernel(x_prompt, x_sample, cache_ckv, cache_krope, state_hgrn, c, c_ctx, ada_w, ada_b, norm_g, ffn_w_gu, ffn_w_down, w_in, hgrn_lb, hgrn_norm_g, mla_q_norm_g, mla_w_q_up, mla_kv_norm_g, mla_w_kv_up, mla_qk_norm_g, mla_out_norm_g, fnet_w, fnet_norm_g, w_out):
    w_gu = ffn_w_gu.astype(BF16)
    w_down = ffn_w_down.astype(BF16)
    o = np.cumsum((0, 5 * W_HGRN, Q_RANK, KV_RANK, D_ROPE, W_FNET))
    w_kr = jnp.pad(w_in[:, :, o[3]:o[4]], ((0, 0), (0, 0), (D_NOPE, HEAD_PAD - D_NOPE - D_ROPE)))
    w_in_arr = jnp.concatenate([w_in[:, :, :o[3]], w_kr, w_in[:, :, o[4]:]], axis=-1).astype(BF16)
    wq_arr = _pad_heads(mla_w_q_up, Q_RANK, D_QK).astype(BF16)
    w_kv = mla_w_kv_up.reshape(DEPTH, KV_RANK, H_MLA, D_NOPE + D_V)
    wk_arr = _pad_heads(w_kv[..., :D_NOPE].reshape(DEPTH, KV_RANK, H_MLA * D_NOPE), KV_RANK, D_NOPE).astype(BF16)
    wv_arr = _pad_heads(w_kv[..., D_NOPE:].reshape(DEPTH, KV_RANK, H_MLA * D_V), KV_RANK, D_V).astype(BF16)
    w_out_b = w_out.astype(BF16)
    gq128 = _place_gain(mla_qk_norm_g[:, 0])
    gk128 = _place_gain(mla_qk_norm_g[:, 1])
    eye_g = jnp.eye(G_FNET, dtype=F32)
    fnet_bd = jnp.einsum("lgcd,gh->lgchd", fnet_w, eye_g).reshape(DEPTH, W_FNET, W_FNET).astype(BF16)
    tabs = _rope_tables()
    lbs = jnp.cumsum(jax.nn.softmax(hgrn_lb.astype(F32), axis=0), axis=0)
    lbs = lbs - lbs[:1]

    cond8 = jnp.zeros((8, D), F32).at[0].set(c_ctx).at[1:1 + N_LAT_SEQ].set(c)
    mods = _mods(cond8, ada_w, ada_b).reshape(DEPTH, 8, N_MOD, D)

    x = jnp.concatenate([x_prompt.reshape(T_CTX, D), x_sample.reshape(T_LAT, D)], axis=0)
    ckv_out, kr_out, st_out = [], [], []
    zero_states = jnp.zeros((N_CTX_SEQ, W_HGRN, W_HGRN), F32)
    for l in range(DEPTH):
        x = _ffn(x, mods, norm_g, w_gu, w_down, l, 0)
        hg, cqn, ckvn, kr128, uf = _inproj(x, mods, norm_g, w_in_arr, mla_q_norm_g, mla_kv_norm_g, l)

        s0 = [jnp.concatenate([zero_states, _states_to_bd_t(state_hgrn[:, l, d])], axis=0) for d in range(2)]
        o_f, s_f = _hgrn(hg, lbs[l, 0].reshape(1, W_HGRN), s0[0], reverse=False)
        gn_row = jnp.tile(hgrn_norm_g[l], H_HGRN).reshape(1, W_HGRN)
        o_hgrn, s_b = _hgrn(hg, lbs[l, 1].reshape(1, W_HGRN), s0[1], reverse=True, final=(o_f, gn_row))
        st_out.append(jnp.stack([_bd_t_to_states(s_f[:N_CTX_SEQ]), _bd_t_to_states(s_b[:N_CTX_SEQ])], axis=1))

        q = _qprep(cqn, wq_arr, gq128, tabs, l)
        k_ctx, v_ctx = _kvprep_ctx(ckvn, kr128, wk_arr, wv_arr, gk128, l)
        cache_kr128 = jnp.pad(cache_krope[:, l], ((0, 0), (0, 0), (D_NOPE, HEAD_PAD - D_NOPE - D_ROPE)))
        k_lat, v_lat = _kvprep_lat(cache_ckv[:, l], cache_kr128, ckvn, kr128, wk_arr, wv_arr, gk128, tabs, l)
        o_att = _attention_ctx(q, k_ctx, v_ctx)
        o_att = _attention_lat(q, k_lat, v_lat, o_att)

        re_c, im_c = _fnet_pos_ctx(uf)
        re_l, im_l = _fnet_pos_lat(uf[T_CTX:])
        o_fnet = _fnet_mix(jnp.concatenate([re_c, re_l], axis=0), jnp.concatenate([im_c, im_l], axis=0),
                           fnet_bd[l], fnet_norm_g[l].reshape(1, W_FNET))

        x = _outproj(x, mods, o_hgrn, o_att, o_fnet, mla_out_norm_g, w_out_b, l)
        x = _ffn(x, mods, norm_g, w_gu, w_down, l, 1)

        ckv_out.append(ckvn[:T_CTX].reshape(N_CTX_SEQ, L_CTX, KV_RANK))
        kr_out.append(kr128[:T_CTX, D_NOPE:D_NOPE + D_ROPE].reshape(N_CTX_SEQ, L_CTX, D_ROPE))

    y_prompt = x[:T_CTX].reshape(N_CTX_SEQ, L_CTX, D)
    y_sample = x[T_CTX:].reshape(N_LAT_SEQ, L_LAT, D)
    return (y_prompt, y_sample, jnp.stack(ckv_out, axis=1), jnp.stack(kr_out, axis=1), jnp.stack(st_out, axis=1))
```

```python
import functools

import numpy as np
import jax
import jax.numpy as jnp
from jax import lax
from jax.experimental import pallas as pl
from jax.experimental.pallas import tpu as pltpu

F32 = jnp.float32
BF16 = jnp.bfloat16
HIGHEST = lax.Precision.HIGHEST

D = 1024
N_CTX_SEQ, L_CTX = 32, 256
N_LAT_SEQ, L_LAT = 2, 4096
T_CTX = N_CTX_SEQ * L_CTX
T_LAT = N_LAT_SEQ * L_LAT
T_ALL = T_CTX + T_LAT
DEPTH = 2
PAST = 512
GRID_W = 64
N_GROUPS = 1 + N_LAT_SEQ
N_MOD = 9
EPS = 1e-6

H_HGRN, DK_HGRN, DV_HGRN = 4, 64, 64
W_HGRN = H_HGRN * DV_HGRN
CHUNK = 32
H_MLA, Q_RANK, KV_RANK = 8, 384, 256
D_NOPE, D_ROPE, D_V = 64, 32, 64
D_QK = D_NOPE + D_ROPE
HEAD_PAD = 128
W_MLA = H_MLA * D_V
G_FNET, C_FNET = 4, 64
W_FNET = G_FNET * C_FNET
D_FF = 2816
FF_CHUNK = 256
ROPE_THETA = 10000.0
LOG2_E = 1.4426950408889634

IN_HG = 5 * W_HGRN
IN_ARR = IN_HG + Q_RANK + KV_RANK + HEAD_PAD + W_FNET

TM = 512
N_TILES = T_ALL // TM
N_CTX_TILES = T_CTX // TM
LAT_TILES_PER_SEQ = L_LAT // TM
T_BLK = 256
TQ = 256

VMEM_LIMIT = 56 * 1024 * 1024


def _cparams(sem):
    return pltpu.CompilerParams(dimension_semantics=sem, vmem_limit_bytes=VMEM_LIMIT)


def _tile_group(i):
    return (i >= N_CTX_TILES).astype(jnp.int32) + (i >= N_CTX_TILES + LAT_TILES_PER_SEQ).astype(jnp.int32)


def _silu(x):
    return x * (1.0 / (1.0 + jnp.exp(-x)))


def _rms_rows(x, g):
    return x * lax.rsqrt(jnp.mean(x * x, axis=-1, keepdims=True) + EPS) * g


def _dot(a, b):
    return jnp.dot(a, b, preferred_element_type=F32)


def _dot_nt(a, b):
    return lax.dot_general(a, b, (((1,), (1,)), ((), ())), preferred_element_type=F32)


def _dot_tn(a, b):
    return lax.dot_general(a, b, (((0,), (0,)), ((), ())), preferred_element_type=F32)


def _dot_exact(a, b):
    return jnp.dot(a, b, preferred_element_type=F32, precision=HIGHEST)


def _mods_kernel(c_ref, w_ref, b_ref, o_ref):
    a = _silu(c_ref[...]).astype(BF16)
    o_ref[...] = _dot(a, w_ref[...].astype(BF16)) + b_ref[...]


def _mods(cond8, ada_w, ada_b):
    tn = 1024
    return pl.pallas_call(
        _mods_kernel,
        grid=(DEPTH, N_MOD * D // tn),
        in_specs=[
            pl.BlockSpec((8, D), lambda l, j: (0, 0)),
            pl.BlockSpec((None, D, tn), lambda l, j: (l, 0, j)),
            pl.BlockSpec((None, 1, tn), lambda l, j: (l, 0, j)),
        ],
        out_specs=pl.BlockSpec((None, 8, tn), lambda l, j: (l, 0, j)),
        out_shape=jax.ShapeDtypeStruct((DEPTH, 8, N_MOD * D), F32),
        compiler_params=_cparams(("arbitrary", "arbitrary")),
        name="ada_mods",
    )(cond8, ada_w, ada_b.reshape(DEPTH, 1, N_MOD * D))


def _mod_spec(layer):
    return pl.BlockSpec((None, None, N_MOD, D), lambda i: (layer, _tile_group(i), 0, 0))


def _resident(shape, index_map):
    return pl.BlockSpec(shape, index_map, pipeline_mode=pl.Buffered(1))


def _ffn_kernel(x_ref, mod_ref, g_ref, wg_ref, wu_ref, wd_ref, *rest, mi):
    o_ref = rest[-1]
    x = x_ref[...]
    shift = mod_ref[mi:mi + 1, :]
    scale = mod_ref[mi + 1:mi + 2, :]
    gate = mod_ref[mi + 2:mi + 3, :]
    hb = (_rms_rows(x, g_ref[...]) * (1.0 + scale) + shift).astype(BF16)
    acc = jnp.zeros(x.shape, F32)
    for j in range(D_FF // FF_CHUNK):
        cs = slice(j * FF_CHUNK, (j + 1) * FF_CHUNK)
        a = _silu(_dot(hb, wg_ref[:, cs])) * _dot(hb, wu_ref[:, cs])
        acc = acc + _dot(a.astype(BF16), wd_ref[cs, :])
    o_ref[...] = x + 0.5 * gate * acc


def _ffn(x, mods, norm_g, w_gu, w_down, layer, which, *, src_tile0=0, n_tiles=N_TILES, stream_tile0=0,
         out_rows=T_ALL, out_tile0=0, into=None):
    mi = 0 if which == 0 else 6
    gi = 0 if which == 0 else 2
    in_specs = [
        pl.BlockSpec((TM, D), lambda i: (i + src_tile0, 0)),
        pl.BlockSpec((None, None, N_MOD, D), lambda i: (layer, _tile_group(i + stream_tile0), 0, 0)),
        pl.BlockSpec((None, None, 1, D), lambda i: (layer, gi, 0, 0)),
        _resident((None, None, D, D_FF), lambda i: (layer, which, 0, 0)),
        _resident((None, None, D, D_FF), lambda i: (layer, which, 0, 1)),
        _resident((None, None, D_FF, D), lambda i: (layer, which, 0, 0)),
    ]
    args = [x, mods, norm_g.reshape(DEPTH, 3, 1, D), w_gu, w_gu, w_down]
    aliases = {}
    if into is not None:
        in_specs.append(pl.BlockSpec(memory_space=pl.ANY))
        args.append(into)
        aliases = {len(args) - 1: 0}
    return pl.pallas_call(
        functools.partial(_ffn_kernel, mi=mi),
        grid=(n_tiles,),
        in_specs=in_specs,
        out_specs=pl.BlockSpec((TM, D), lambda i: (i + out_tile0, 0)),
        out_shape=jax.ShapeDtypeStruct((out_rows, D), F32),
        input_output_aliases=aliases,
        compiler_params=_cparams(("arbitrary",)),
        name=f"ffn_l{layer}_{which}",
    )(*args)


def _inproj_kernel(x_ref, mod_ref, g_ref, w_ref, gq_ref, gkv_ref, hg_ref, cq_ref, ckv_ref, kr_ref, uf_ref):
    x = x_ref[...]
    shift = mod_ref[3:4, :]
    scale = mod_ref[4:5, :]
    hb = (_rms_rows(x, g_ref[...]) * (1.0 + scale) + shift).astype(BF16)
    u = _dot(hb, w_ref[...])
    o = 0
    hg_ref[...] = u[:, o:o + IN_HG]
    o += IN_HG
    cq_ref[...] = _rms_rows(u[:, o:o + Q_RANK], gq_ref[...])
    o += Q_RANK
    ckv_ref[...] = _rms_rows(u[:, o:o + KV_RANK], gkv_ref[...])
    o += KV_RANK
    kr_ref[...] = u[:, o:o + HEAD_PAD]
    o += HEAD_PAD
    uf_ref[...] = u[:, o:o + W_FNET]


def _inproj(x, mods, norm_g, w_in_arr, gq, gkv, layer):
    def row(w):
        return pl.BlockSpec((TM, w), lambda i: (i, 0))
    widths = (IN_HG, Q_RANK, KV_RANK, HEAD_PAD, W_FNET)
    return pl.pallas_call(
        _inproj_kernel,
        grid=(N_TILES,),
        in_specs=[
            row(D),
            _mod_spec(layer),
            pl.BlockSpec((None, None, 1, D), lambda i: (layer, 1, 0, 0)),
            _resident((None, D, IN_ARR), lambda i: (layer, 0, 0)),
            pl.BlockSpec((None, 1, Q_RANK), lambda i: (layer, 0, 0)),
            pl.BlockSpec((None, 1, KV_RANK), lambda i: (layer, 0, 0)),
        ],
        out_specs=[row(w) for w in widths],
        out_shape=[jax.ShapeDtypeStruct((T_ALL, w), F32) for w in widths],
        compiler_params=_cparams(("arbitrary",)),
        name=f"inproj_l{layer}",
    )(x, mods, norm_g.reshape(DEPTH, 3, 1, D), w_in_arr, gq.reshape(DEPTH, 1, Q_RANK),
      gkv.reshape(DEPTH, 1, KV_RANK))


N_SEQ = N_CTX_SEQ + N_LAT_SEQ
BLK_PER_LAT = L_LAT // T_BLK
N_HGRN_STEPS = N_CTX_SEQ + N_LAT_SEQ * BLK_PER_LAT


def _hgrn_seq(i):
    return jnp.where(i < N_CTX_SEQ, i, N_CTX_SEQ + (i - N_CTX_SEQ) // BLK_PER_LAT)


def _hgrn_blk(i, reverse):
    j = (i - N_CTX_SEQ) % BLK_PER_LAT
    if reverse:
        j = BLK_PER_LAT - 1 - j
    lat = N_CTX_SEQ + ((i - N_CTX_SEQ) // BLK_PER_LAT) * BLK_PER_LAT + j
    return jnp.where(i < N_CTX_SEQ, i, lat)


def _hgrn_chunk(q_raw, x, v, loglb, log1mlb, tri, tri4, hm, bd, st, reverse):
    end_row = 0 if reverse else CHUNK - 1
    mid_row = CHUNK // 2 if reverse else CHUNK // 2 - 1
    q = _silu(q_raw)
    y = log1mlb + (jnp.minimum(x, 0.0) - jnp.log(1.0 + jnp.exp(-jnp.abs(x))))
    g = jnp.maximum(loglb, y) + jnp.log(1.0 + jnp.exp(-jnp.abs(loglb - y)))
    kk = 1.0 - jnp.exp(g)
    b = _dot_exact(tri, g)
    b_end = b[end_row:end_row + 1, :]
    b_mid = b[mid_row:mid_row + 1, :]
    q_in = (q * jnp.exp(b)).astype(BF16)
    q_t = (q * jnp.exp(b - b_mid)).astype(BF16)
    k_t = kk * jnp.exp(b_mid - b)
    k_e = (kk * jnp.exp(b_end - b)).astype(BF16)
    k_bd = (jnp.concatenate([k_t] * H_HGRN, axis=0) * hm).astype(BF16)
    v_bd = (jnp.concatenate([v] * H_HGRN, axis=0) * hm).astype(BF16)
    sc = _dot_nt(q_t, k_bd) * tri4
    o = _dot(sc.astype(BF16), v_bd) + _dot_nt(q_in, st.astype(BF16))
    st_new = st * jnp.exp(b_end) + _dot_tn(v.astype(BF16), k_e) * bd
    return o, st_new


def _hgrn_kernel(qf_ref, ff_ref, vf_ref, qb_ref, fb_ref, vb_ref, lb_ref, s0f_ref, s0b_ref,
                 tri_ref, tri4_ref, hm_ref, bd_ref, of_ref, ob_ref, sf_ref, sb_ref, stf_scr, stb_scr):
    i = pl.program_id(0)
    first = jnp.logical_or(i < N_CTX_SEQ, (i - N_CTX_SEQ) % BLK_PER_LAT == 0)
    heads = [slice(h * DK_HGRN, (h + 1) * DK_HGRN) for h in range(H_HGRN)]

    @pl.when(first)
    def _():
        for s0_ref, scr in ((s0f_ref, stf_scr), (s0b_ref, stb_scr)):
            scr[...] = jnp.zeros(scr.shape, F32)
            for h, hs in enumerate(heads):
                scr[hs, hs] = s0_ref[h]

    lb = lb_ref[...]
    loglb = jnp.log(lb)
    log1mlb = jnp.log(1.0 - lb)
    n_chunks = T_BLK // CHUNK
    dirs = ((qf_ref, ff_ref, vf_ref, of_ref, stf_scr), (qb_ref, fb_ref, vb_ref, ob_ref, stb_scr))
    for c in range(n_chunks):
        for d, (q_ref, f_ref, v_ref, o_ref, scr) in enumerate(dirs):
            cc = c if d == 0 else n_chunks - 1 - c
            rows = slice(cc * CHUNK, (cc + 1) * CHUNK)
            o, st_new = _hgrn_chunk(q_ref[rows, :], f_ref[rows, :], v_ref[rows, :], loglb[d:d + 1], log1mlb[d:d + 1],
                                    tri_ref[d], tri4_ref[d], hm_ref[...], bd_ref[...], scr[...], reverse=d == 1)
            o_ref[rows, :] = o
            scr[...] = st_new
    for h, hs in enumerate(heads):
        sf_ref[h] = stf_scr[hs, hs]
        sb_ref[h] = stb_scr[hs, hs]


def _hgrn_consts():
    t = np.arange(CHUNK)
    tri = np.stack([t[:, None] >= t[None, :], t[:, None] <= t[None, :]]).astype(np.float32)
    tri4 = np.tile(tri, (1, 1, H_HGRN))
    r = np.arange(H_HGRN * CHUNK)
    lane = np.arange(W_HGRN)
    hm = (r[:, None] // CHUNK == lane[None, :] // DK_HGRN).astype(np.float32)
    bd = (lane[:, None] // DV_HGRN == lane[None, :] // DK_HGRN).astype(np.float32)
    return jnp.asarray(tri), jnp.asarray(tri4), jnp.asarray(hm), jnp.asarray(bd)


def _hgrn(hg, lb2, s0f, s0b):
    tri, tri4, hm, bd = _hgrn_consts()

    def col(cb, reverse):
        return pl.BlockSpec((T_BLK, W_HGRN), lambda i: (_hgrn_blk(i, reverse), cb))

    def const(shape):
        return pl.BlockSpec(shape, lambda i: (0,) * len(shape))
    state = pl.BlockSpec((None, H_HGRN, DV_HGRN, DK_HGRN), lambda i: (_hgrn_seq(i), 0, 0, 0))
    return pl.pallas_call(
        _hgrn_kernel,
        grid=(N_HGRN_STEPS,),
        in_specs=[col(0, False), col(1, False), col(3, False), col(0, True), col(2, True), col(3, True),
                  const((2, W_HGRN)), state, state,
                  const(tri.shape), const(tri4.shape), const(hm.shape), const(bd.shape)],
        out_specs=[col(0, False), col(0, True), state, state],
        out_shape=[jax.ShapeDtypeStruct((T_ALL, W_HGRN), F32)] * 2
        + [jax.ShapeDtypeStruct((N_SEQ, H_HGRN, DV_HGRN, DK_HGRN), F32)] * 2,
        scratch_shapes=[pltpu.VMEM((W_HGRN, W_HGRN), F32)] * 2,
        compiler_params=_cparams(("arbitrary",)),
        name="hgrn",
    )(hg, hg, hg, hg, hg, hg, lb2, s0f, s0b, tri, tri4, hm, bd)


def _rope(x, cos, sin_a, sin_b):
    return x * cos + pltpu.roll(x, HEAD_PAD - D_ROPE // 4, axis=1) * sin_a + pltpu.roll(x, D_ROPE // 4, axis=1) * sin_b


def _head_norm(x, g):
    return x * lax.rsqrt(jnp.sum(x * x, axis=-1, keepdims=True) * (1.0 / D_QK) + EPS) * g


def _qprep_kernel(cq_ref, w_ref, g_ref, cos_ref, sa_ref, sb_ref, q_ref):
    qa = _dot(cq_ref[...].astype(BF16), w_ref[...])
    g = g_ref[...]
    cos, sa, sb = cos_ref[...], sa_ref[...], sb_ref[...]
    scale = D_QK ** -0.5 * LOG2_E
    for h in range(H_MLA):
        hs = slice(h * HEAD_PAD, (h + 1) * HEAD_PAD)
        q = _rope(_head_norm(qa[:, hs], g), cos, sa, sb) * scale
        q_ref[:, hs] = q.astype(BF16)


def _rope_block(i):
    return jnp.where(i < N_CTX_TILES, 0, 1 + (i - N_CTX_TILES) % LAT_TILES_PER_SEQ)


def _qprep(cqn, wq_arr, gq128, tabs, layer):
    tab = pl.BlockSpec((None, TM, HEAD_PAD), lambda i: (_rope_block(i), 0, 0))
    return pl.pallas_call(
        _qprep_kernel,
        grid=(N_TILES,),
        in_specs=[
            pl.BlockSpec((TM, Q_RANK), lambda i: (i, 0)),
            _resident((None, Q_RANK, H_MLA * HEAD_PAD), lambda i: (layer, 0, 0)),
            pl.BlockSpec((None, 1, HEAD_PAD), lambda i: (layer, 0, 0)),
            tab, tab, tab,
        ],
        out_specs=pl.BlockSpec((TM, H_MLA * HEAD_PAD), lambda i: (i, 0)),
        out_shape=jax.ShapeDtypeStruct((T_ALL, H_MLA * HEAD_PAD), BF16),
        compiler_params=_cparams(("arbitrary",)),
        name=f"mla_q_l{layer}",
    )(cqn, wq_arr, gq128, *tabs)


def _kv_body(ckv, kr, wk_ref, wv_ref, g_ref, cos, sa, sb, k_ref, v_ref):
    cb = ckv.astype(BF16)
    ka = _dot(cb, wk_ref[...])
    lane = lax.broadcasted_iota(jnp.int32, (1, H_MLA * HEAD_PAD), 1) % HEAD_PAD
    v_ref[...] = jnp.where(lane == D_V, 1.0, _dot(cb, wv_ref[...])).astype(BF16)
    g = g_ref[...]
    for h in range(H_MLA):
        hs = slice(h * HEAD_PAD, (h + 1) * HEAD_PAD)
        k = _head_norm(ka[:, hs] + kr, g)
        if cos is not None:
            k = _rope(k, cos, sa, sb)
        k_ref[:, hs] = k.astype(BF16)


def _kvprep_ctx_kernel(ckv_ref, kr_ref, wk_ref, wv_ref, g_ref, k_ref, v_ref):
    _kv_body(ckv_ref[...], kr_ref[...], wk_ref, wv_ref, g_ref, None, None, None, k_ref, v_ref)


def _kvprep_lat_kernel(cckv_ref, ckr_ref, ckv_ref, kr_ref, wk_ref, wv_ref, g_ref, cos_ref, sa_ref, sb_ref,
                       k_ref, v_ref):
    past = pl.program_id(1) == 0
    ckv = jnp.where(past, cckv_ref[...], ckv_ref[...])
    kr = jnp.where(past, ckr_ref[...], kr_ref[...])
    _kv_body(ckv, kr, wk_ref, wv_ref, g_ref, cos_ref[...], sa_ref[...], sb_ref[...], k_ref, v_ref)


def _kv_weight_specs(layer, nargs):
    zero = (lambda *a: (layer, 0, 0))
    return [_resident((None, KV_RANK, H_MLA * HEAD_PAD), zero),
            _resident((None, KV_RANK, H_MLA * HEAD_PAD), zero),
            pl.BlockSpec((None, 1, HEAD_PAD), zero)]


def _kvprep_ctx(ckvn, kr128, wk_arr, wv_arr, gk128, layer):
    wide = H_MLA * HEAD_PAD
    return pl.pallas_call(
        _kvprep_ctx_kernel,
        grid=(N_CTX_TILES,),
        in_specs=[pl.BlockSpec((TM, KV_RANK), lambda i: (i, 0)),
                  pl.BlockSpec((TM, HEAD_PAD), lambda i: (i, 0))] + _kv_weight_specs(layer, 1),
        out_specs=[pl.BlockSpec((TM, wide), lambda i: (i, 0))] * 2,
        out_shape=[jax.ShapeDtypeStruct((T_CTX, wide), BF16)] * 2,
        compiler_params=_cparams(("arbitrary",)),
        name=f"mla_kv_ctx_l{layer}",
    )(ckvn, kr128, wk_arr, wv_arr, gk128)


LK_LAT = PAST + L_LAT
KV_TILES_LAT = LK_LAT // TM


def _kvprep_lat(cache_ckv_l, cache_kr128, ckvn, kr128, wk_arr, wv_arr, gk128, tabs, layer):
    wide = H_MLA * HEAD_PAD
    assert PAST == TM

    def new_rows(b, j):
        return (N_CTX_TILES + b * LAT_TILES_PER_SEQ + jnp.maximum(j - 1, 0), 0)
    tab = pl.BlockSpec((None, TM, HEAD_PAD), lambda b, j: (j, 0, 0))
    return pl.pallas_call(
        _kvprep_lat_kernel,
        grid=(N_LAT_SEQ, KV_TILES_LAT),
        in_specs=[pl.BlockSpec((None, TM, KV_RANK), lambda b, j: (b, 0, 0)),
                  pl.BlockSpec((None, TM, HEAD_PAD), lambda b, j: (b, 0, 0)),
                  pl.BlockSpec((TM, KV_RANK), new_rows),
                  pl.BlockSpec((TM, HEAD_PAD), new_rows)] + _kv_weight_specs(layer, 2) + [tab, tab, tab],
        out_specs=[pl.BlockSpec((None, TM, wide), lambda b, j: (b, j, 0))] * 2,
        out_shape=[jax.ShapeDtypeStruct((N_LAT_SEQ, LK_LAT, wide), BF16)] * 2,
        compiler_params=_cparams(("arbitrary", "arbitrary")),
        name=f"mla_kv_lat_l{layer}",
    )(cache_ckv_l, cache_kr128, ckvn, kr128, wk_arr, wv_arr, gk128, *tabs)


def _attn_kernel(*refs):
    q_ref, k_ref, v_ref = refs[:3]
    o_ref = refs[-1]
    lane = lax.broadcasted_iota(jnp.int32, (1, HEAD_PAD), 1)
    for hp in range(q_ref.shape[1] // (2 * HEAD_PAD)):
        outs = []
        for h in (2 * hp, 2 * hp + 1):
            hs = slice(h * HEAD_PAD, (h + 1) * HEAD_PAD)
            s = _dot_nt(q_ref[:, hs], k_ref[:, hs])
            p = jnp.exp2(s - jnp.max(s, axis=-1, keepdims=True))
            pv = _dot(p.astype(BF16), v_ref[:, hs])
            o = pv * (1.0 / pv[:, D_V:D_V + 1])
            outs.append(jnp.where(lane < D_V, o, 0.0))
        o_ref[:, hp * 2 * D_V:(hp + 1) * 2 * D_V] = outs[0] + pltpu.roll(outs[1], D_V, axis=1)


def _attention_ctx(q, k, v):
    wide = H_MLA * HEAD_PAD
    blk = pl.BlockSpec((L_CTX, wide), lambda b: (b, 0))
    return pl.pallas_call(
        _attn_kernel,
        grid=(N_CTX_SEQ,),
        in_specs=[blk, blk, blk],
        out_specs=pl.BlockSpec((L_CTX, W_MLA), lambda b: (b, 0)),
        out_shape=jax.ShapeDtypeStruct((T_ALL, W_MLA), F32),
        compiler_params=_cparams(("arbitrary",)),
        name="attn_ctx",
    )(q, k, v)


def _attention_lat(q, k, v, o_all):
    pair = 2 * HEAD_PAD
    nq = L_LAT // TQ
    q0 = T_CTX // TQ

    def qrow(b, hp, qi):
        return (q0 + b * nq + qi, hp)
    kv = pl.BlockSpec((None, LK_LAT, pair), lambda b, hp, qi: (b, 0, hp))
    return pl.pallas_call(
        _attn_kernel,
        grid=(N_LAT_SEQ, H_MLA // 2, nq),
        in_specs=[pl.BlockSpec((TQ, pair), qrow), kv, kv, pl.BlockSpec(memory_space=pl.ANY)],
        out_specs=pl.BlockSpec((TQ, 2 * D_V), qrow),
        out_shape=jax.ShapeDtypeStruct((T_ALL, W_MLA), F32),
        input_output_aliases={3: 0},
        compiler_params=_cparams(("arbitrary", "arbitrary", "arbitrary")),
        name="attn_lat",
    )(q, k, v, o_all)


def _dft_tables(n):
    a = 2.0 * np.pi * np.outer(np.arange(n), np.arange(n)) / n
    return np.cos(a), np.sin(a)


def _fnet_ctx_kernel(u_ref, c_ref, s_ref, re_ref, im_ref):
    u = u_ref[...]
    re_ref[...] = _dot_exact(c_ref[...], u)
    im_ref[...] = -_dot_exact(s_ref[...], u)


def _fnet_pos_ctx(uf):
    c, s = _dft_tables(L_CTX)
    norm = 1.0 / np.sqrt(L_CTX * C_FNET)
    c = jnp.asarray(c * norm, F32)
    s = jnp.asarray(s * norm, F32)
    blk = pl.BlockSpec((L_CTX, W_FNET), lambda b: (b, 0))
    mat = pl.BlockSpec((L_CTX, L_CTX), lambda b: (0, 0))
    return pl.pallas_call(
        _fnet_ctx_kernel,
        grid=(N_CTX_SEQ,),
        in_specs=[blk, mat, mat],
        out_specs=[blk, blk],
        out_shape=[jax.ShapeDtypeStruct((T_CTX, W_FNET), F32)] * 2,
        compiler_params=_cparams(("arbitrary",)),
        name="fnet_pos_ctx",
    )(uf, c, s)


FN_COLS = GRID_W * W_FNET
FN_TILE = 2048


def _fnet_rows_kernel(x_ref, m_ref, tc_ref, ts_ref, re_ref, im_ref):
    w = _dot_exact(m_ref[...], x_ref[...])
    wr, wi = w[:GRID_W], w[GRID_W:]
    tc, ts = tc_ref[...], ts_ref[...]
    re_ref[...] = wr * tc + wi * ts
    im_ref[...] = wi * tc - wr * ts


def _fnet_cols_kernel(re_in, im_in, m_ref, re_ref, im_ref):
    x = jnp.concatenate([re_in[...], im_in[...]], axis=0)
    z = _dot_exact(m_ref[...], x)
    re_ref[...] = z[:GRID_W]
    im_ref[...] = z[GRID_W:]


def _fnet_pos_lat(uf_lat):
    c64, s64 = _dft_tables(GRID_W)
    norm = 1.0 / np.sqrt(L_LAT * C_FNET)
    m_rows = jnp.asarray(np.concatenate([c64, -s64], axis=0) * norm, F32)
    m_cols = jnp.asarray(np.block([[c64, s64], [-s64, c64]]), F32)
    ang = 2.0 * np.pi * np.outer(np.arange(GRID_W), np.arange(GRID_W)) / L_LAT
    tc = jnp.broadcast_to(jnp.asarray(np.cos(ang), F32)[:, :, None], (GRID_W, GRID_W, W_FNET)).reshape(GRID_W, FN_COLS)
    ts = jnp.broadcast_to(jnp.asarray(np.sin(ang), F32)[:, :, None], (GRID_W, GRID_W, W_FNET)).reshape(GRID_W, FN_COLS)
    x = uf_lat.reshape(N_LAT_SEQ, GRID_W, FN_COLS)
    blk = pl.BlockSpec((None, GRID_W, FN_TILE), lambda b, j: (b, 0, j))
    tw = pl.BlockSpec((GRID_W, FN_TILE), lambda b, j: (0, j))
    out = [jax.ShapeDtypeStruct((N_LAT_SEQ, GRID_W, FN_COLS), F32)] * 2
    grid = (N_LAT_SEQ, FN_COLS // FN_TILE)
    wr, wi = pl.pallas_call(
        _fnet_rows_kernel,
        grid=grid,
        in_specs=[blk, pl.BlockSpec((2 * GRID_W, GRID_W), lambda b, j: (0, 0)), tw, tw],
        out_specs=[blk, blk],
        out_shape=out,
        compiler_params=_cparams(("arbitrary", "arbitrary")),
        name="fnet_rows_lat",
    )(x, m_rows, tc, ts)

    def swap(a):
        return a.reshape(N_LAT_SEQ, GRID_W, GRID_W, W_FNET).transpose(0, 2, 1, 3).reshape(N_LAT_SEQ, GRID_W, FN_COLS)
    zr, zi = pl.pallas_call(
        _fnet_cols_kernel,
        grid=grid,
        in_specs=[blk, blk, pl.BlockSpec((2 * GRID_W, 2 * GRID_W), lambda b, j: (0, 0))],
        out_specs=[blk, blk],
        out_shape=out,
        compiler_params=_cparams(("arbitrary", "arbitrary")),
        name="fnet_cols_lat",
    )(swap(wr), swap(wi), m_cols)
    return zr.reshape(T_LAT, W_FNET), zi.reshape(T_LAT, W_FNET)


def _fnet_mix_kernel(rc_ref, ic_ref, rl_ref, il_ref, cc_ref, sc_ref, w_ref, g_ref, o_ref):
    ctx = pl.program_id(0) < N_CTX_TILES
    re = jnp.where(ctx, rc_ref[...], rl_ref[...])
    im = jnp.where(ctx, ic_ref[...], il_ref[...])
    spec = _dot_exact(re, cc_ref[...]) + _dot_exact(im, sc_ref[...])
    o = _dot(spec.astype(BF16), w_ref[...])
    o_ref[...] = _rms_rows(o, g_ref[...])


def _fnet_mix(re_ctx, im_ctx, re_lat, im_lat, w_bd, g_row):
    c, s = _dft_tables(C_FNET)
    eye = np.eye(G_FNET)
    cc = jnp.asarray(np.kron(eye, c), F32)
    sc = jnp.asarray(np.kron(eye, s), F32)
    ctx_row = pl.BlockSpec((TM, W_FNET), lambda i: (jnp.minimum(i, N_CTX_TILES - 1), 0))
    lat_row = pl.BlockSpec((TM, W_FNET), lambda i: (jnp.maximum(i - N_CTX_TILES, 0), 0))
    mat = pl.BlockSpec((W_FNET, W_FNET), lambda i: (0, 0))
    return pl.pallas_call(
        _fnet_mix_kernel,
        grid=(N_TILES,),
        in_specs=[ctx_row, ctx_row, lat_row, lat_row, mat, mat, mat, pl.BlockSpec((1, W_FNET), lambda i: (0, 0))],
        out_specs=pl.BlockSpec((TM, W_FNET), lambda i: (i, 0)),
        out_shape=jax.ShapeDtypeStruct((T_ALL, W_FNET), F32),
        compiler_params=_cparams(("arbitrary",)),
        name="fnet_mix",
    )(re_ctx, im_ctx, re_lat, im_lat, cc, sc, w_bd, g_row)


def _outproj_kernel(x_ref, mod_ref, hf_ref, hb_ref, hgate_ref, gn_ref, ones_ref, oa_ref, of_ref, ga_ref, w_ref, o_ref):
    gate = mod_ref[5:6, :]
    oh = hf_ref[...] + hb_ref[...]
    ms = _dot_exact(oh * oh, ones_ref[...])
    oh = oh * lax.rsqrt(ms + EPS) * gn_ref[...] * _silu(hgate_ref[...])
    oa = _rms_rows(oa_ref[...], ga_ref[...])
    o = (_dot(oh.astype(BF16), w_ref[0:W_HGRN, :])
         + _dot(oa.astype(BF16), w_ref[W_HGRN:W_HGRN + W_MLA, :])
         + _dot(of_ref[...].astype(BF16), w_ref[W_HGRN + W_MLA:, :]))
    o_ref[...] = x_ref[...] + gate * o


def _outproj(x, mods, h_fwd, h_bwd, hg, gn_row, o_att, o_fnet, g_att, w_out, layer):
    def row(w, cb=0):
        return pl.BlockSpec((TM, w), lambda i: (i, cb))
    lane = np.arange(W_HGRN)
    ones_bd = jnp.asarray((lane[:, None] // DV_HGRN == lane[None, :] // DV_HGRN).astype(np.float32) / DV_HGRN)
    return pl.pallas_call(
        _outproj_kernel,
        grid=(N_TILES,),
        in_specs=[row(D), _mod_spec(layer), row(W_HGRN), row(W_HGRN), row(W_HGRN, 4),
                  pl.BlockSpec((1, W_HGRN), lambda i: (0, 0)), pl.BlockSpec((W_HGRN, W_HGRN), lambda i: (0, 0)),
                  row(W_MLA), row(W_FNET),
                  pl.BlockSpec((None, 1, W_MLA), lambda i: (layer, 0, 0)),
                  _resident((None, D, D), lambda i: (layer, 0, 0))],
        out_specs=row(D),
        out_shape=jax.ShapeDtypeStruct((T_ALL, D), F32),
        compiler_params=_cparams(("arbitrary",)),
        name=f"outproj_l{layer}",
    )(x, mods, h_fwd, h_bwd, hg, gn_row, ones_bd, o_att, o_fnet, g_att.reshape(DEPTH, 1, W_MLA), w_out)


def _pad_heads(w, d_in, d_head):
    w = w.reshape(DEPTH, d_in, H_MLA, d_head)
    w = jnp.pad(w, ((0, 0), (0, 0), (0, 0), (0, HEAD_PAD - d_head)))
    return w.reshape(DEPTH, d_in, H_MLA * HEAD_PAD)


def _rope_tables():
    n_freq = D_ROPE // 4
    t = np.arange(L_LAT)
    pos = np.stack([t // GRID_W, t % GRID_W], axis=-1).astype(np.float32)
    freq = (np.float32(ROPE_THETA) ** (-np.arange(n_freq, dtype=np.float32) / n_freq)).astype(np.float32)
    ang = (pos[:, :, None] * freq).astype(np.float32)
    cos = np.ones((L_LAT, HEAD_PAD), np.float32)
    sin_a = np.zeros((L_LAT, HEAD_PAD), np.float32)
    sin_b = np.zeros((L_LAT, HEAD_PAD), np.float32)
    for ax in range(2):
        base = D_NOPE + ax * 2 * n_freq
        c, s = np.cos(ang[:, ax, :]), np.sin(ang[:, ax, :])
        cos[:, base:base + n_freq] = c
        cos[:, base + n_freq:base + 2 * n_freq] = c
        sin_a[:, base:base + n_freq] = -s
        sin_b[:, base + n_freq:base + 2 * n_freq] = s
    def blocks(tab, ident):
        full = np.concatenate([np.full((TM, HEAD_PAD), ident, np.float32), tab], axis=0)
        return jnp.asarray(full.reshape(1 + LAT_TILES_PER_SEQ, TM, HEAD_PAD))
    return blocks(cos, 1.0), blocks(sin_a, 0.0), blocks(sin_b, 0.0)


def _place_gain(g):
    return jnp.pad(g, ((0, 0), (0, HEAD_PAD - D_QK))).reshape(DEPTH, 1, HEAD_PAD)


def kernel(x_prompt, x_sample, cache_ckv, cache_krope, state_hgrn, c, c_ctx, ada_w, ada_b, norm_g, ffn_w_gu, ffn_w_down, w_in, hgrn_lb, hgrn_norm_g, mla_q_norm_g, mla_w_q_up, mla_kv_norm_g, mla_w_kv_up, mla_qk_norm_g, mla_out_norm_g, fnet_w, fnet_norm_g, w_out):
    w_gu = ffn_w_gu.astype(BF16)
    w_down = ffn_w_down.astype(BF16)
    o = np.cumsum((0, 5 * W_HGRN, Q_RANK, KV_RANK, D_ROPE, W_FNET))
    w_kr = jnp.pad(w_in[:, :, o[3]:o[4]], ((0, 0), (0, 0), (D_NOPE, HEAD_PAD - D_NOPE - D_ROPE)))
    w_in_arr = jnp.concatenate([w_in[:, :, :o[3]], w_kr, w_in[:, :, o[4]:]], axis=-1).astype(BF16)
    wq_arr = _pad_heads(mla_w_q_up, Q_RANK, D_QK).astype(BF16)
    w_kv = mla_w_kv_up.reshape(DEPTH, KV_RANK, H_MLA, D_NOPE + D_V)
    wk_arr = _pad_heads(w_kv[..., :D_NOPE].reshape(DEPTH, KV_RANK, H_MLA * D_NOPE), KV_RANK, D_NOPE).astype(BF16)
    wv_arr = _pad_heads(w_kv[..., D_NOPE:].reshape(DEPTH, KV_RANK, H_MLA * D_V), KV_RANK, D_V).astype(BF16)
    w_out_b = w_out.astype(BF16)
    gq128 = _place_gain(mla_qk_norm_g[:, 0])
    gk128 = _place_gain(mla_qk_norm_g[:, 1])
    eye_g = jnp.eye(G_FNET, dtype=F32)
    fnet_bd = jnp.einsum("lgcd,gh->lgchd", fnet_w, eye_g).reshape(DEPTH, W_FNET, W_FNET).astype(BF16)
    tabs = _rope_tables()
    lbs = jnp.cumsum(jax.nn.softmax(hgrn_lb.astype(F32), axis=0), axis=0)
    lbs = lbs - lbs[:1]

    cond8 = jnp.zeros((8, D), F32).at[0].set(c_ctx).at[1:1 + N_LAT_SEQ].set(c)
    mods = _mods(cond8, ada_w, ada_b).reshape(DEPTH, 8, N_MOD, D)

    ckv_out, kr_out, st_out = [], [], []
    zero_states = jnp.zeros((N_CTX_SEQ, H_HGRN, DV_HGRN, DK_HGRN), F32)
    for l in range(DEPTH):
        if l == 0:
            x = _ffn(x_prompt.reshape(T_CTX, D), mods, norm_g, w_gu, w_down, l, 0, n_tiles=N_CTX_TILES)
            x = _ffn(x_sample.reshape(T_LAT, D), mods, norm_g, w_gu, w_down, l, 0, n_tiles=N_TILES - N_CTX_TILES,
                     stream_tile0=N_CTX_TILES, out_tile0=N_CTX_TILES, into=x)
        else:
            x = _ffn(x, mods, norm_g, w_gu, w_down, l, 0)
        hg, cqn, ckvn, kr128, uf = _inproj(x, mods, norm_g, w_in_arr, mla_q_norm_g, mla_kv_norm_g, l)

        s0 = [jnp.concatenate([zero_states, jnp.swapaxes(state_hgrn[:, l, d], -1, -2)], axis=0) for d in range(2)]
        h_fwd, h_bwd, s_f, s_b = _hgrn(hg, lbs[l], s0[0], s0[1])
        gn_row = jnp.tile(hgrn_norm_g[l], H_HGRN).reshape(1, W_HGRN)
        st_out.append(jnp.swapaxes(jnp.stack([s_f[:N_CTX_SEQ], s_b[:N_CTX_SEQ]], axis=1), -1, -2))

        q = _qprep(cqn, wq_arr, gq128, tabs, l)
        k_ctx, v_ctx = _kvprep_ctx(ckvn, kr128, wk_arr, wv_arr, gk128, l)
        cache_kr128 = jnp.pad(cache_krope[:, l], ((0, 0), (0, 0), (D_NOPE, HEAD_PAD - D_NOPE - D_ROPE)))
        k_lat, v_lat = _kvprep_lat(cache_ckv[:, l], cache_kr128, ckvn, kr128, wk_arr, wv_arr, gk128, tabs, l)
        o_att = _attention_ctx(q, k_ctx, v_ctx)
        o_att = _attention_lat(q, k_lat, v_lat, o_att)

        re_c, im_c = _fnet_pos_ctx(uf)
        re_l, im_l = _fnet_pos_lat(uf[T_CTX:])
        o_fnet = _fnet_mix(re_c, im_c, re_l, im_l, fnet_bd[l], fnet_norm_g[l].reshape(1, W_FNET))

        x = _outproj(x, mods, h_fwd, h_bwd, hg, gn_row, o_att, o_fnet, mla_out_norm_g, w_out_b, l)
        if l == DEPTH - 1:
            y_prompt = _ffn(x, mods, norm_g, w_gu, w_down, l, 1, n_tiles=N_CTX_TILES, out_rows=T_CTX)
            y_sample = _ffn(x, mods, norm_g, w_gu, w_down, l, 1, src_tile0=N_CTX_TILES, n_tiles=N_TILES - N_CTX_TILES,
                            stream_tile0=N_CTX_TILES, out_rows=T_LAT)
        else:
            x = _ffn(x, mods, norm_g, w_gu, w_down, l, 1)

        ckv_out.append(ckvn[:T_CTX].reshape(N_CTX_SEQ, L_CTX, KV_RANK))
        kr_out.append(kr128[:T_CTX, D_NOPE:D_NOPE + D_ROPE].reshape(N_CTX_SEQ, L_CTX, D_ROPE))

    return (y_prompt.reshape(N_CTX_SEQ, L_CTX, D), y_sample.reshape(N_LAT_SEQ, L_LAT, D),
            jnp.stack(ckv_out, axis=1), jnp.stack(kr_out, axis=1), jnp.stack(st_out, axis=1))
```

```python
import functools

import numpy as np
import jax
import jax.numpy as jnp
from jax import lax
from jax.experimental import pallas as pl
from jax.experimental.pallas import tpu as pltpu

F32 = jnp.float32
BF16 = jnp.bfloat16
HIGHEST = lax.Precision.HIGHEST

D = 1024
N_CTX_SEQ, L_CTX = 32, 256
N_LAT_SEQ, L_LAT = 2, 4096
T_CTX = N_CTX_SEQ * L_CTX
T_LAT = N_LAT_SEQ * L_LAT
T_ALL = T_CTX + T_LAT
DEPTH = 2
PAST = 512
GRID_W = 64
N_MOD = 9
EPS = 1e-6

H_HGRN, DK_HGRN, DV_HGRN = 4, 64, 64
W_HGRN = H_HGRN * DV_HGRN
CHUNK = 32
H_MLA, Q_RANK, KV_RANK = 8, 384, 256
D_NOPE, D_ROPE, D_V = 64, 32, 64
D_QK = D_NOPE + D_ROPE
HEAD_PAD = 128
W_HEADS = H_MLA * HEAD_PAD
W_MLA = H_MLA * D_V
G_FNET, C_FNET = 4, 64
W_FNET = G_FNET * C_FNET
D_FF = 2816
FF_CHUNK = 256
ROPE_THETA = 10000.0
LOG2_E = 1.4426950408889634

IN_HG = 5 * W_HGRN
IN_ARR = IN_HG + Q_RANK + KV_RANK + 2 * HEAD_PAD + W_FNET

TM = 512
N_TILES = T_ALL // TM
N_CTX_TILES = T_CTX // TM
LAT_TILES_PER_SEQ = L_LAT // TM
T_BLK = 256
TQ = 256

VMEM_LIMIT = 56 * 1024 * 1024


def _cparams(sem):
    return pltpu.CompilerParams(dimension_semantics=sem, vmem_limit_bytes=VMEM_LIMIT)


def _tile_group(i):
    return (i >= N_CTX_TILES).astype(jnp.int32) + (i >= N_CTX_TILES + LAT_TILES_PER_SEQ).astype(jnp.int32)


def _silu(x):
    return x * (1.0 / (1.0 + jnp.exp(-x)))


def _rms_rows(x, g):
    return x * lax.rsqrt(jnp.mean(x * x, axis=-1, keepdims=True) + EPS) * g


def _dot(a, b):
    return jnp.dot(a, b, preferred_element_type=F32)


def _dot_nt(a, b):
    return lax.dot_general(a, b, (((1,), (1,)), ((), ())), preferred_element_type=F32)


def _dot_tn(a, b):
    return lax.dot_general(a, b, (((0,), (0,)), ((), ())), preferred_element_type=F32)


def _dot_exact(a, b):
    return jnp.dot(a, b, preferred_element_type=F32, precision=HIGHEST)


def _row_spec(width, col_block=0):
    return pl.BlockSpec((TM, width), lambda i: (i, col_block))


def _ctx_row_spec(width):
    return pl.BlockSpec((TM, width), lambda i: (jnp.minimum(i, N_CTX_TILES - 1), 0))


def _lat_row_spec(width):
    return pl.BlockSpec((TM, width), lambda i: (jnp.maximum(i - N_CTX_TILES, 0), 0))


def _mod_spec(layer):
    return pl.BlockSpec((None, None, N_MOD, D), lambda i: (layer, _tile_group(i), 0, 0))


def _resident(shape, index_map):
    return pl.BlockSpec(shape, index_map, pipeline_mode=pl.Buffered(1))


def _mods_kernel(c_ref, w_ref, b_ref, o_ref):
    a = _silu(c_ref[...]).astype(BF16)
    o_ref[...] = _dot(a, w_ref[...].astype(BF16)) + b_ref[...]


def _mods(cond8, ada_w, ada_b):
    tn = 1024
    return pl.pallas_call(
        _mods_kernel,
        grid=(DEPTH, N_MOD * D // tn),
        in_specs=[
            pl.BlockSpec((8, D), lambda l, j: (0, 0)),
            pl.BlockSpec((None, D, tn), lambda l, j: (l, 0, j)),
            pl.BlockSpec((None, 1, tn), lambda l, j: (l, 0, j)),
        ],
        out_specs=pl.BlockSpec((None, 8, tn), lambda l, j: (l, 0, j)),
        out_shape=jax.ShapeDtypeStruct((DEPTH, 8, N_MOD * D), F32),
        compiler_params=_cparams(("arbitrary", "arbitrary")),
        name="ada_mods",
    )(cond8, ada_w, ada_b.reshape(DEPTH, 1, N_MOD * D))


def _ffn_kernel(*refs, mi, two_sources):
    if two_sources:
        xc_ref, xl_ref, mod_ref, g_ref, wg_ref, wu_ref, wd_ref, o_ref = refs
        x = jnp.where(pl.program_id(0) < N_CTX_TILES, xc_ref[...], xl_ref[...])
    else:
        x_ref, mod_ref, g_ref, wg_ref, wu_ref, wd_ref, o_ref = refs
        x = x_ref[...]
    shift = mod_ref[mi:mi + 1, :]
    scale = mod_ref[mi + 1:mi + 2, :]
    gate = mod_ref[mi + 2:mi + 3, :]
    hb = (_rms_rows(x, g_ref[...]) * (1.0 + scale) + shift).astype(BF16)
    acc = jnp.zeros(x.shape, F32)
    for j in range(D_FF // FF_CHUNK):
        cs = slice(j * FF_CHUNK, (j + 1) * FF_CHUNK)
        a = _silu(_dot(hb, wg_ref[:, cs])) * _dot(hb, wu_ref[:, cs])
        acc = acc + _dot(a.astype(BF16), wd_ref[cs, :])
    o_ref[...] = x + 0.5 * gate * acc


def _ffn(xs, mods, norm_g, w_gu, w_down, layer, which, *, src_tile0=0, n_tiles=N_TILES, out_rows=T_ALL):
    mi = 0 if which == 0 else 6
    gi = 0 if which == 0 else 2
    two = isinstance(xs, tuple)
    if two:
        x_specs = [_ctx_row_spec(D), _lat_row_spec(D)]
        xs = list(xs)
    else:
        x_specs = [pl.BlockSpec((TM, D), lambda i: (i + src_tile0, 0))]
        xs = [xs]
    return pl.pallas_call(
        functools.partial(_ffn_kernel, mi=mi, two_sources=two),
        grid=(n_tiles,),
        in_specs=x_specs + [
            pl.BlockSpec((None, None, N_MOD, D), lambda i: (layer, _tile_group(i + src_tile0), 0, 0)),
            pl.BlockSpec((None, None, 1, D), lambda i: (layer, gi, 0, 0)),
            _resident((None, None, D, D_FF), lambda i: (layer, which, 0, 0)),
            _resident((None, None, D, D_FF), lambda i: (layer, which, 0, 1)),
            _resident((None, None, D_FF, D), lambda i: (layer, which, 0, 0)),
        ],
        out_specs=pl.BlockSpec((TM, D), lambda i: (i, 0)),
        out_shape=jax.ShapeDtypeStruct((out_rows, D), F32),
        compiler_params=_cparams(("arbitrary",)),
        name=f"ffn_l{layer}_{which}",
    )(*xs, mods, norm_g.reshape(DEPTH, 3, 1, D), w_gu, w_gu, w_down)


def _mla_heads(qa, ka, va, kr, kr_sw, tabs, q_ref, k_ref, v_ref):
    cq, sq, ck, sk = tabs
    lane = lax.broadcasted_iota(jnp.int32, (1, W_HEADS), 1) % HEAD_PAD
    v_ref[...] = jnp.where(lane == D_V, 1.0, va).astype(BF16)
    k_rot = kr_sw * sk
    for h in range(H_MLA):
        hs = slice(h * HEAD_PAD, (h + 1) * HEAD_PAD)
        if qa is not None:
            q = qa[:, hs]
            rs = lax.rsqrt(jnp.sum(q * q, axis=-1, keepdims=True) * (1.0 / D_QK) + EPS)
            q_ref[:, hs] = (rs * (q * cq + qa[:, W_HEADS + h * HEAD_PAD:W_HEADS + (h + 1) * HEAD_PAD] * sq)).astype(BF16)
        k = ka[:, hs] + kr
        rs = lax.rsqrt(jnp.sum(k * k, axis=-1, keepdims=True) * (1.0 / D_QK) + EPS)
        k_ref[:, hs] = (rs * (k * ck + k_rot)).astype(BF16)


def _inproj_kernel(x_ref, mod_ref, g_ref, w_ref, gq_ref, gkv_ref, wq_ref, wk_ref, wv_ref, cq_ref, sq_ref, ck_ref,
                   sk_ref, hg_ref, ckv_ref, kr_ref, uf_ref, q_ref, k_ref, v_ref):
    x = x_ref[...]
    shift = mod_ref[3:4, :]
    scale = mod_ref[4:5, :]
    hb = (_rms_rows(x, g_ref[...]) * (1.0 + scale) + shift).astype(BF16)
    u = _dot(hb, w_ref[...])
    o = 0
    hg_ref[...] = u[:, o:o + IN_HG]
    o += IN_HG
    cqn = _rms_rows(u[:, o:o + Q_RANK], gq_ref[...])
    o += Q_RANK
    ckvn = _rms_rows(u[:, o:o + KV_RANK], gkv_ref[...])
    ckv_ref[...] = ckvn
    o += KV_RANK
    kr = u[:, o:o + HEAD_PAD]
    kr_ref[...] = kr
    o += HEAD_PAD
    kr_sw = u[:, o:o + HEAD_PAD]
    o += HEAD_PAD
    uf_ref[...] = u[:, o:o + W_FNET]
    cb = ckvn.astype(BF16)
    _mla_heads(_dot(cqn.astype(BF16), wq_ref[...]), _dot(cb, wk_ref[...]), _dot(cb, wv_ref[...]), kr, kr_sw,
               (cq_ref[...], sq_ref[...], ck_ref[...], sk_ref[...]), q_ref, k_ref, v_ref)


def _rope_block(i):
    return jnp.where(i < N_CTX_TILES, 0, 1 + (i - N_CTX_TILES) % LAT_TILES_PER_SEQ)


def _inproj(x, mods, norm_g, w_in_arr, gq, gkv, wq2, wk_arr, wv_arr, tabs, layer):
    tab = pl.BlockSpec((None, None, TM, HEAD_PAD), lambda i: (layer, _rope_block(i), 0, 0))
    f32_widths = (IN_HG, KV_RANK, HEAD_PAD, W_FNET)
    return pl.pallas_call(
        _inproj_kernel,
        grid=(N_TILES,),
        in_specs=[
            _row_spec(D),
            _mod_spec(layer),
            pl.BlockSpec((None, None, 1, D), lambda i: (layer, 1, 0, 0)),
            _resident((None, D, IN_ARR), lambda i: (layer, 0, 0)),
            pl.BlockSpec((None, 1, Q_RANK), lambda i: (layer, 0, 0)),
            pl.BlockSpec((None, 1, KV_RANK), lambda i: (layer, 0, 0)),
            _resident((None, Q_RANK, 2 * W_HEADS), lambda i: (layer, 0, 0)),
            _resident((None, KV_RANK, W_HEADS), lambda i: (layer, 0, 0)),
            _resident((None, KV_RANK, W_HEADS), lambda i: (layer, 0, 0)),
            tab, tab, tab, tab,
        ],
        out_specs=[_row_spec(w) for w in f32_widths] + [_row_spec(W_HEADS)] * 3,
        out_shape=[jax.ShapeDtypeStruct((T_ALL, w), F32) for w in f32_widths]
        + [jax.ShapeDtypeStruct((T_ALL, W_HEADS), BF16)] * 3,
        compiler_params=_cparams(("arbitrary",)),
        name=f"inproj_l{layer}",
    )(x, mods, norm_g.reshape(DEPTH, 3, 1, D), w_in_arr, gq.reshape(DEPTH, 1, Q_RANK),
      gkv.reshape(DEPTH, 1, KV_RANK), wq2, wk_arr, wv_arr, *tabs)


def _kv_cache_kernel(ckv_ref, kr_ref, krsw_ref, wk_ref, wv_ref, ck_ref, sk_ref, k_ref, v_ref):
    cb = ckv_ref[...].astype(BF16)
    _mla_heads(None, _dot(cb, wk_ref[...]), _dot(cb, wv_ref[...]), kr_ref[...], krsw_ref[...],
               (None, None, ck_ref[...], sk_ref[...]), None, k_ref, v_ref)


def _kv_cache(cache_ckv_l, cache_kr, cache_kr_sw, wk_arr, wv_arr, tabs, layer):
    assert PAST == TM
    blk = lambda w: pl.BlockSpec((None, PAST, w), lambda b: (b, 0, 0))
    tab = pl.BlockSpec((None, None, TM, HEAD_PAD), lambda b: (layer, 0, 0, 0))
    return pl.pallas_call(
        _kv_cache_kernel,
        grid=(N_LAT_SEQ,),
        in_specs=[blk(KV_RANK), blk(HEAD_PAD), blk(HEAD_PAD),
                  _resident((None, KV_RANK, W_HEADS), lambda b: (layer, 0, 0)),
                  _resident((None, KV_RANK, W_HEADS), lambda b: (layer, 0, 0)), tab, tab],
        out_specs=[blk(W_HEADS)] * 2,
        out_shape=[jax.ShapeDtypeStruct((N_LAT_SEQ, PAST, W_HEADS), BF16)] * 2,
        compiler_params=_cparams(("arbitrary",)),
        name=f"mla_kv_cache_l{layer}",
    )(cache_ckv_l, cache_kr, cache_kr_sw, wk_arr, wv_arr, tabs[2], tabs[3])


N_SEQ = N_CTX_SEQ + N_LAT_SEQ
BLK_PER_LAT = L_LAT // T_BLK
N_HGRN_STEPS = N_CTX_SEQ + N_LAT_SEQ * BLK_PER_LAT


def _hgrn_seq(i):
    return jnp.where(i < N_CTX_SEQ, i, N_CTX_SEQ + (i - N_CTX_SEQ) // BLK_PER_LAT)


def _hgrn_blk(i, reverse):
    j = (i - N_CTX_SEQ) % BLK_PER_LAT
    if reverse:
        j = BLK_PER_LAT - 1 - j
    lat = N_CTX_SEQ + ((i - N_CTX_SEQ) // BLK_PER_LAT) * BLK_PER_LAT + j
    return jnp.where(i < N_CTX_SEQ, i, lat)


def _hgrn_chunk(q_raw, x, v, loglb, log1mlb, tri, tri4, hm, bd, st, reverse):
    end_row = 0 if reverse else CHUNK - 1
    mid_row = CHUNK // 2 if reverse else CHUNK // 2 - 1
    q = _silu(q_raw)
    y = log1mlb + (jnp.minimum(x, 0.0) - jnp.log(1.0 + jnp.exp(-jnp.abs(x))))
    g = jnp.maximum(loglb, y) + jnp.log(1.0 + jnp.exp(-jnp.abs(loglb - y)))
    kk = 1.0 - jnp.exp(g)
    b = _dot_exact(tri, g)
    b_end = b[end_row:end_row + 1, :]
    b_mid = b[mid_row:mid_row + 1, :]
    q_in = (q * jnp.exp(b)).astype(BF16)
    q_t = (q * jnp.exp(b - b_mid)).astype(BF16)
    k_t = kk * jnp.exp(b_mid - b)
    k_e = (kk * jnp.exp(b_end - b)).astype(BF16)
    k_bd = (jnp.concatenate([k_t] * H_HGRN, axis=0) * hm).astype(BF16)
    v_bd = (jnp.concatenate([v] * H_HGRN, axis=0) * hm).astype(BF16)
    sc = _dot_nt(q_t, k_bd) * tri4
    o = _dot(sc.astype(BF16), v_bd) + _dot_nt(q_in, st.astype(BF16))
    st_new = st * jnp.exp(b_end) + _dot_tn(v.astype(BF16), k_e) * bd
    return o, st_new


def _hgrn_kernel(qf_ref, ff_ref, vf_ref, qb_ref, fb_ref, vb_ref, lb_ref, s0f_ref, s0b_ref,
                 tri_ref, tri4_ref, hm_ref, bd_ref, of_ref, ob_ref, sf_ref, sb_ref, stf_scr, stb_scr):
    i = pl.program_id(0)
    first = jnp.logical_or(i < N_CTX_SEQ, (i - N_CTX_SEQ) % BLK_PER_LAT == 0)
    heads = [slice(h * DK_HGRN, (h + 1) * DK_HGRN) for h in range(H_HGRN)]

    @pl.when(first)
    def _():
        for s0_ref, scr in ((s0f_ref, stf_scr), (s0b_ref, stb_scr)):
            scr[...] = jnp.zeros(scr.shape, F32)
            for h, hs in enumerate(heads):
                scr[hs, hs] = s0_ref[h]

    lb = lb_ref[...]
    loglb = jnp.log(lb)
    log1mlb = jnp.log(1.0 - lb)
    n_chunks = T_BLK // CHUNK
    dirs = ((qf_ref, ff_ref, vf_ref, of_ref, stf_scr), (qb_ref, fb_ref, vb_ref, ob_ref, stb_scr))
    for c in range(n_chunks):
        for d, (q_ref, f_ref, v_ref, o_ref, scr) in enumerate(dirs):
            cc = c if d == 0 else n_chunks - 1 - c
            rows = slice(cc * CHUNK, (cc + 1) * CHUNK)
            o, st_new = _hgrn_chunk(q_ref[rows, :], f_ref[rows, :], v_ref[rows, :], loglb[d:d + 1], log1mlb[d:d + 1],
                                    tri_ref[d], tri4_ref[d], hm_ref[...], bd_ref[...], scr[...], reverse=d == 1)
            o_ref[rows, :] = o
            scr[...] = st_new
    for h, hs in enumerate(heads):
        sf_ref[h] = stf_scr[hs, hs]
        sb_ref[h] = stb_scr[hs, hs]


def _hgrn_consts():
    t = np.arange(CHUNK)
    tri = np.stack([t[:, None] >= t[None, :], t[:, None] <= t[None, :]]).astype(np.float32)
    tri4 = np.tile(tri, (1, 1, H_HGRN))
    r = np.arange(H_HGRN * CHUNK)
    lane = np.arange(W_HGRN)
    hm = (r[:, None] // CHUNK == lane[None, :] // DK_HGRN).astype(np.float32)
    bd = (lane[:, None] // DV_HGRN == lane[None, :] // DK_HGRN).astype(np.float32)
    return jnp.asarray(tri), jnp.asarray(tri4), jnp.asarray(hm), jnp.asarray(bd)


def _hgrn(hg, lb2, s0f, s0b):
    tri, tri4, hm, bd = _hgrn_consts()

    def col(cb, reverse):
        return pl.BlockSpec((T_BLK, W_HGRN), lambda i: (_hgrn_blk(i, reverse), cb))

    def const(shape):
        return pl.BlockSpec(shape, lambda i: (0,) * len(shape))
    state = pl.BlockSpec((None, H_HGRN, DV_HGRN, DK_HGRN), lambda i: (_hgrn_seq(i), 0, 0, 0))
    return pl.pallas_call(
        _hgrn_kernel,
        grid=(N_HGRN_STEPS,),
        in_specs=[col(0, False), col(1, False), col(3, False), col(0, True), col(2, True), col(3, True),
                  const((2, W_HGRN)), state, state,
                  const(tri.shape), const(tri4.shape), const(hm.shape), const(bd.shape)],
        out_specs=[col(0, False), col(0, True), state, state],
        out_shape=[jax.ShapeDtypeStruct((T_ALL, W_HGRN), F32)] * 2
        + [jax.ShapeDtypeStruct((N_SEQ, H_HGRN, DV_HGRN, DK_HGRN), F32)] * 2,
        scratch_shapes=[pltpu.VMEM((W_HGRN, W_HGRN), F32)] * 2,
        compiler_params=_cparams(("arbitrary",)),
        name="hgrn",
    )(hg, hg, hg, hg, hg, hg, lb2, s0f, s0b, tri, tri4, hm, bd)


def _attn_kernel(q_ref, k_ref, v_ref, o_ref):
    lane = lax.broadcasted_iota(jnp.int32, (1, HEAD_PAD), 1)
    for hp in range(q_ref.shape[1] // (2 * HEAD_PAD)):
        outs = []
        for h in (2 * hp, 2 * hp + 1):
            hs = slice(h * HEAD_PAD, (h + 1) * HEAD_PAD)
            s = _dot_nt(q_ref[:, hs], k_ref[:, hs])
            p = jnp.exp2(s - jnp.max(s, axis=-1, keepdims=True))
            pv = _dot(p.astype(BF16), v_ref[:, hs])
            o = pv * (1.0 / pv[:, D_V:D_V + 1])
            outs.append(jnp.where(lane < D_V, o, 0.0))
        o_ref[:, hp * 2 * D_V:(hp + 1) * 2 * D_V] = outs[0] + pltpu.roll(outs[1], D_V, axis=1)


def _attn_lat_kernel(q_ref, kp_ref, vp_ref, kn_ref, vn_ref, o_ref, k_scr, v_scr):
    @pl.when(pl.program_id(2) == 0)
    def _():
        k_scr[0:PAST, :] = kp_ref[...]
        k_scr[PAST:, :] = kn_ref[...]
        v_scr[0:PAST, :] = vp_ref[...]
        v_scr[PAST:, :] = vn_ref[...]
    _attn_kernel(q_ref, k_scr, v_scr, o_ref)


def _attention_ctx(q, k, v):
    blk = pl.BlockSpec((L_CTX, W_HEADS), lambda b: (b, 0))
    return pl.pallas_call(
        _attn_kernel,
        grid=(N_CTX_SEQ,),
        in_specs=[blk, blk, blk],
        out_specs=pl.BlockSpec((L_CTX, W_MLA), lambda b: (b, 0)),
        out_shape=jax.ShapeDtypeStruct((T_CTX, W_MLA), F32),
        compiler_params=_cparams(("arbitrary",)),
        name="attn_ctx",
    )(q, k, v)


def _attention_lat(q, k, v, k_past, v_past):
    pair = 2 * HEAD_PAD
    nq = L_LAT // TQ
    q0 = T_CTX // TQ
    seq0 = T_CTX // L_LAT
    new = pl.BlockSpec((L_LAT, pair), lambda b, hp, qi: (seq0 + b, hp))
    past = pl.BlockSpec((None, PAST, pair), lambda b, hp, qi: (b, 0, hp))
    return pl.pallas_call(
        _attn_lat_kernel,
        grid=(N_LAT_SEQ, H_MLA // 2, nq),
        in_specs=[pl.BlockSpec((TQ, pair), lambda b, hp, qi: (q0 + b * nq + qi, hp)), past, past, new, new],
        out_specs=pl.BlockSpec((TQ, 2 * D_V), lambda b, hp, qi: (b * nq + qi, hp)),
        out_shape=jax.ShapeDtypeStruct((T_LAT, W_MLA), F32),
        scratch_shapes=[pltpu.VMEM((PAST + L_LAT, pair), BF16)] * 2,
        compiler_params=_cparams(("arbitrary", "arbitrary", "arbitrary")),
        name="attn_lat",
    )(q, k_past, v_past, k, v)


def _dft_tables(n):
    a = 2.0 * np.pi * np.outer(np.arange(n), np.arange(n)) / n
    return np.cos(a), np.sin(a)


def _fnet_ctx_kernel(u_ref, c_ref, s_ref, re_ref, im_ref):
    u = u_ref[...]
    re_ref[...] = _dot_exact(c_ref[...], u)
    im_ref[...] = -_dot_exact(s_ref[...], u)


def _fnet_pos_ctx(uf):
    c, s = _dft_tables(L_CTX)
    norm = 1.0 / np.sqrt(L_CTX * C_FNET)
    c = jnp.asarray(c * norm, F32)
    s = jnp.asarray(s * norm, F32)
    blk = pl.BlockSpec((L_CTX, W_FNET), lambda b: (b, 0))
    mat = pl.BlockSpec((L_CTX, L_CTX), lambda b: (0, 0))
    return pl.pallas_call(
        _fnet_ctx_kernel,
        grid=(N_CTX_SEQ,),
        in_specs=[blk, mat, mat],
        out_specs=[blk, blk],
        out_shape=[jax.ShapeDtypeStruct((T_CTX, W_FNET), F32)] * 2,
        compiler_params=_cparams(("arbitrary",)),
        name="fnet_pos_ctx",
    )(uf, c, s)


FN_COLS = GRID_W * W_FNET
FN_TILE = 2048


def _fnet_rows_kernel(x_ref, m_ref, tc_ref, ts_ref, re_ref, im_ref):
    w = _dot_exact(m_ref[...], x_ref[...])
    wr, wi = w[:GRID_W], w[GRID_W:]
    tc, ts = tc_ref[...], ts_ref[...]
    re_ref[...] = wr * tc + wi * ts
    im_ref[...] = wi * tc - wr * ts


def _fnet_cols_kernel(re_in, im_in, m_ref, re_ref, im_ref):
    x = jnp.concatenate([re_in[...], im_in[...]], axis=0)
    z = _dot_exact(m_ref[...], x)
    re_ref[...] = z[:GRID_W]
    im_ref[...] = z[GRID_W:]


def _fnet_pos_lat(uf_lat):
    c64, s64 = _dft_tables(GRID_W)
    norm = 1.0 / np.sqrt(L_LAT * C_FNET)
    m_rows = jnp.asarray(np.concatenate([c64, -s64], axis=0) * norm, F32)
    m_cols = jnp.asarray(np.block([[c64, s64], [-s64, c64]]), F32)
    ang = 2.0 * np.pi * np.outer(np.arange(GRID_W), np.arange(GRID_W)) / L_LAT
    tc = jnp.broadcast_to(jnp.asarray(np.cos(ang), F32)[:, :, None], (GRID_W, GRID_W, W_FNET)).reshape(GRID_W, FN_COLS)
    ts = jnp.broadcast_to(jnp.asarray(np.sin(ang), F32)[:, :, None], (GRID_W, GRID_W, W_FNET)).reshape(GRID_W, FN_COLS)
    x = uf_lat.reshape(N_LAT_SEQ, GRID_W, FN_COLS)
    blk = pl.BlockSpec((None, GRID_W, FN_TILE), lambda b, j: (b, 0, j))
    tw = pl.BlockSpec((GRID_W, FN_TILE), lambda b, j: (0, j))
    out = [jax.ShapeDtypeStruct((N_LAT_SEQ, GRID_W, FN_COLS), F32)] * 2
    grid = (N_LAT_SEQ, FN_COLS // FN_TILE)
    wr, wi = pl.pallas_call(
        _fnet_rows_kernel,
        grid=grid,
        in_specs=[blk, pl.BlockSpec((2 * GRID_W, GRID_W), lambda b, j: (0, 0)), tw, tw],
        out_specs=[blk, blk],
        out_shape=out,
        compiler_params=_cparams(("arbitrary", "arbitrary")),
        name="fnet_rows_lat",
    )(x, m_rows, tc, ts)

    def swap(a):
        return a.reshape(N_LAT_SEQ, GRID_W, GRID_W, W_FNET).transpose(0, 2, 1, 3).reshape(N_LAT_SEQ, GRID_W, FN_COLS)
    zr, zi = pl.pallas_call(
        _fnet_cols_kernel,
        grid=grid,
        in_specs=[blk, blk, pl.BlockSpec((2 * GRID_W, 2 * GRID_W), lambda b, j: (0, 0))],
        out_specs=[blk, blk],
        out_shape=out,
        compiler_params=_cparams(("arbitrary", "arbitrary")),
        name="fnet_cols_lat",
    )(swap(wr), swap(wi), m_cols)
    return zr.reshape(T_LAT, W_FNET), zi.reshape(T_LAT, W_FNET)


def _fnet_mix_kernel(rc_ref, ic_ref, rl_ref, il_ref, cc_ref, sc_ref, w_ref, g_ref, o_ref):
    ctx = pl.program_id(0) < N_CTX_TILES
    re = jnp.where(ctx, rc_ref[...], rl_ref[...])
    im = jnp.where(ctx, ic_ref[...], il_ref[...])
    spec = _dot_exact(re, cc_ref[...]) + _dot_exact(im, sc_ref[...])
    o = _dot(spec.astype(BF16), w_ref[...])
    o_ref[...] = _rms_rows(o, g_ref[...])


def _fnet_mix(re_ctx, im_ctx, re_lat, im_lat, w_bd, g_row):
    c, s = _dft_tables(C_FNET)
    eye = np.eye(G_FNET)
    cc = jnp.asarray(np.kron(eye, c), F32)
    sc = jnp.asarray(np.kron(eye, s), F32)
    mat = pl.BlockSpec((W_FNET, W_FNET), lambda i: (0, 0))
    return pl.pallas_call(
        _fnet_mix_kernel,
        grid=(N_TILES,),
        in_specs=[_ctx_row_spec(W_FNET), _ctx_row_spec(W_FNET), _lat_row_spec(W_FNET), _lat_row_spec(W_FNET),
                  mat, mat, mat, pl.BlockSpec((1, W_FNET), lambda i: (0, 0))],
        out_specs=_row_spec(W_FNET),
        out_shape=jax.ShapeDtypeStruct((T_ALL, W_FNET), F32),
        compiler_params=_cparams(("arbitrary",)),
        name="fnet_mix",
    )(re_ctx, im_ctx, re_lat, im_lat, cc, sc, w_bd, g_row)


def _outproj_kernel(x_ref, mod_ref, hf_ref, hb_ref, hgate_ref, gn_ref, ones_ref, ac_ref, al_ref, of_ref, ga_ref,
                    w_ref, o_ref):
    gate = mod_ref[5:6, :]
    oh = hf_ref[...] + hb_ref[...]
    ms = _dot_exact(oh * oh, ones_ref[...])
    oh = oh * lax.rsqrt(ms + EPS) * gn_ref[...] * _silu(hgate_ref[...])
    oa = jnp.where(pl.program_id(0) < N_CTX_TILES, ac_ref[...], al_ref[...])
    oa = _rms_rows(oa, ga_ref[...])
    o = (_dot(oh.astype(BF16), w_ref[0:W_HGRN, :])
         + _dot(oa.astype(BF16), w_ref[W_HGRN:W_HGRN + W_MLA, :])
         + _dot(of_ref[...].astype(BF16), w_ref[W_HGRN + W_MLA:, :]))
    o_ref[...] = x_ref[...] + gate * o


def _outproj(x, mods, h_fwd, h_bwd, hg, gn_row, att_ctx, att_lat, o_fnet, g_att, w_out, layer):
    lane = np.arange(W_HGRN)
    ones_bd = jnp.asarray((lane[:, None] // DV_HGRN == lane[None, :] // DV_HGRN).astype(np.float32) / DV_HGRN)
    return pl.pallas_call(
        _outproj_kernel,
        grid=(N_TILES,),
        in_specs=[_row_spec(D), _mod_spec(layer), _row_spec(W_HGRN), _row_spec(W_HGRN), _row_spec(W_HGRN, 4),
                  pl.BlockSpec((1, W_HGRN), lambda i: (0, 0)), pl.BlockSpec((W_HGRN, W_HGRN), lambda i: (0, 0)),
                  _ctx_row_spec(W_MLA), _lat_row_spec(W_MLA), _row_spec(W_FNET),
                  pl.BlockSpec((None, 1, W_MLA), lambda i: (layer, 0, 0)),
                  _resident((None, D, D), lambda i: (layer, 0, 0))],
        out_specs=_row_spec(D),
        out_shape=jax.ShapeDtypeStruct((T_ALL, D), F32),
        compiler_params=_cparams(("arbitrary",)),
        name=f"outproj_l{layer}",
    )(x, mods, h_fwd, h_bwd, hg, gn_row, ones_bd, att_ctx, att_lat, o_fnet, g_att.reshape(DEPTH, 1, W_MLA), w_out)


_ROPE_LANES = np.arange(D_NOPE, D_NOPE + D_ROPE)
_ROPE_PARTNER = np.where((_ROPE_LANES - D_NOPE) % (D_ROPE // 2) < D_ROPE // 4, _ROPE_LANES + D_ROPE // 4,
                         _ROPE_LANES - D_ROPE // 4)


def _pad_heads(w, d_head):
    lead = w.shape[:-1]
    w = w.reshape(lead + (H_MLA, d_head))
    w = jnp.pad(w, [(0, 0)] * len(lead) + [(0, 0), (0, HEAD_PAD - d_head)])
    return w.reshape(lead + (W_HEADS,))


def _swap_rope(a):
    out = jnp.zeros_like(a)
    return out.at[..., _ROPE_LANES].set(a[..., _ROPE_PARTNER])


def _rope_tables(g_q, g_k):
    n_freq = D_ROPE // 4
    t = np.arange(L_LAT)
    pos = np.stack([t // GRID_W, t % GRID_W], axis=-1).astype(np.float32)
    freq = (np.float32(ROPE_THETA) ** (-np.arange(n_freq, dtype=np.float32) / n_freq)).astype(np.float32)
    ang = (pos[:, :, None] * freq).astype(np.float32)
    cos = np.ones((L_LAT, HEAD_PAD), np.float32)
    sin = np.zeros((L_LAT, HEAD_PAD), np.float32)
    for ax in range(2):
        base = D_NOPE + ax * 2 * n_freq
        c, s = np.cos(ang[:, ax, :]), np.sin(ang[:, ax, :])
        cos[:, base:base + n_freq] = c
        cos[:, base + n_freq:base + 2 * n_freq] = c
        sin[:, base:base + n_freq] = -s
        sin[:, base + n_freq:base + 2 * n_freq] = s
    cos = np.concatenate([np.ones((TM, HEAD_PAD), np.float32), cos]).reshape(1, 1 + LAT_TILES_PER_SEQ, TM, HEAD_PAD)
    sin = np.concatenate([np.zeros((TM, HEAD_PAD), np.float32), sin]).reshape(1, 1 + LAT_TILES_PER_SEQ, TM, HEAD_PAD)

    def pair(g, scale):
        g128 = jnp.pad(g, ((0, 0), (0, HEAD_PAD - D_QK))) * scale
        g_partner = _swap_rope(g128)
        return (cos * g128[:, None, None, :], sin * g_partner[:, None, None, :])
    return pair(g_q, D_QK ** -0.5 * LOG2_E) + pair(g_k, 1.0)


def kernel(x_prompt, x_sample, cache_ckv, cache_krope, state_hgrn, c, c_ctx, ada_w, ada_b, norm_g, ffn_w_gu, ffn_w_down, w_in, hgrn_lb, hgrn_norm_g, mla_q_norm_g, mla_w_q_up, mla_kv_norm_g, mla_w_kv_up, mla_qk_norm_g, mla_out_norm_g, fnet_w, fnet_norm_g, w_out):
    w_gu = ffn_w_gu.astype(BF16)
    w_down = ffn_w_down.astype(BF16)
    o = np.cumsum((0, 5 * W_HGRN, Q_RANK, KV_RANK, D_ROPE, W_FNET))
    w_kr = jnp.pad(w_in[:, :, o[3]:o[4]], ((0, 0), (0, 0), (D_NOPE, HEAD_PAD - D_NOPE - D_ROPE)))
    w_in_arr = jnp.concatenate([w_in[:, :, :o[3]], w_kr, _swap_rope(w_kr), w_in[:, :, o[4]:]], axis=-1).astype(BF16)
    wq = _pad_heads(mla_w_q_up, D_QK)
    wq_sw = _swap_rope(wq.reshape(DEPTH, Q_RANK, H_MLA, HEAD_PAD)).reshape(DEPTH, Q_RANK, W_HEADS)
    wq2 = jnp.concatenate([wq, wq_sw], axis=-1).astype(BF16)
    w_kv = mla_w_kv_up.reshape(DEPTH, KV_RANK, H_MLA, D_NOPE + D_V)
    wk_arr = _pad_heads(w_kv[..., :D_NOPE].reshape(DEPTH, KV_RANK, H_MLA * D_NOPE), D_NOPE).astype(BF16)
    wv_arr = _pad_heads(w_kv[..., D_NOPE:].reshape(DEPTH, KV_RANK, H_MLA * D_V), D_V).astype(BF16)
    w_out_b = w_out.astype(BF16)
    eye_g = jnp.eye(G_FNET, dtype=F32)
    fnet_bd = jnp.einsum("lgcd,gh->lgchd", fnet_w, eye_g).reshape(DEPTH, W_FNET, W_FNET).astype(BF16)
    tabs = _rope_tables(mla_qk_norm_g[:, 0], mla_qk_norm_g[:, 1])
    lbs = jnp.cumsum(jax.nn.softmax(hgrn_lb.astype(F32), axis=0), axis=0)
    lbs = lbs - lbs[:1]

    cond8 = jnp.zeros((8, D), F32).at[0].set(c_ctx).at[1:1 + N_LAT_SEQ].set(c)
    mods = _mods(cond8, ada_w, ada_b).reshape(DEPTH, 8, N_MOD, D)

    x = (x_prompt.reshape(T_CTX, D), x_sample.reshape(T_LAT, D))
    ckv_out, kr_out, st_out = [], [], []
    zero_states = jnp.zeros((N_CTX_SEQ, H_HGRN, DV_HGRN, DK_HGRN), F32)
    for l in range(DEPTH):
        x = _ffn(x, mods, norm_g, w_gu, w_down, l, 0)
        hg, ckvn, kr128, uf, q, k, v = _inproj(x, mods, norm_g, w_in_arr, mla_q_norm_g, mla_kv_norm_g, wq2, wk_arr,
                                               wv_arr, tabs, l)

        s0 = [jnp.concatenate([zero_states, jnp.swapaxes(state_hgrn[:, l, d], -1, -2)], axis=0) for d in range(2)]
        h_fwd, h_bwd, s_f, s_b = _hgrn(hg, lbs[l], s0[0], s0[1])
        gn_row = jnp.tile(hgrn_norm_g[l], H_HGRN).reshape(1, W_HGRN)
        st_out.append(jnp.swapaxes(jnp.stack([s_f[:N_CTX_SEQ], s_b[:N_CTX_SEQ]], axis=1), -1, -2))

        cache_kr = jnp.pad(cache_krope[:, l], ((0, 0), (0, 0), (D_NOPE, HEAD_PAD - D_NOPE - D_ROPE)))
        k_past, v_past = _kv_cache(cache_ckv[:, l], cache_kr, _swap_rope(cache_kr), wk_arr, wv_arr, tabs, l)
        att_ctx = _attention_ctx(q, k, v)
        att_lat = _attention_lat(q, k, v, k_past, v_past)

        re_c, im_c = _fnet_pos_ctx(uf)
        re_l, im_l = _fnet_pos_lat(uf[T_CTX:])
        o_fnet = _fnet_mix(re_c, im_c, re_l, im_l, fnet_bd[l], fnet_norm_g[l].reshape(1, W_FNET))

        x = _outproj(x, mods, h_fwd, h_bwd, hg, gn_row, att_ctx, att_lat, o_fnet, mla_out_norm_g, w_out_b, l)
        if l == DEPTH - 1:
            y_prompt = _ffn(x, mods, norm_g, w_gu, w_down, l, 1, n_tiles=N_CTX_TILES, out_rows=T_CTX)
            y_sample = _ffn(x, mods, norm_g, w_gu, w_down, l, 1, src_tile0=N_CTX_TILES, n_tiles=N_TILES - N_CTX_TILES,
                            out_rows=T_LAT)
        else:
            x = _ffn(x, mods, norm_g, w_gu, w_down, l, 1)

        ckv_out.append(ckvn[:T_CTX].reshape(N_CTX_SEQ, L_CTX, KV_RANK))
        kr_out.append(kr128[:T_CTX, D_NOPE:D_NOPE + D_ROPE].reshape(N_CTX_SEQ, L_CTX, D_ROPE))

    return (y_prompt.reshape(N_CTX_SEQ, L_CTX, D), y_sample.reshape(N_LAT_SEQ, L_LAT, D),
            jnp.stack(ckv_out, axis=1), jnp.stack(kr_out, axis=1), jnp.stack(st_out, axis=1))
```

```python
import functools

import numpy as np
import jax
import jax.numpy as jnp
from jax import lax
from jax.experimental import pallas as pl
from jax.experimental.pallas import tpu as pltpu

F32 = jnp.float32
BF16 = jnp.bfloat16
HIGHEST = lax.Precision.HIGHEST

D = 1024
N_CTX_SEQ, L_CTX = 32, 256
N_LAT_SEQ, L_LAT = 2, 4096
T_CTX = N_CTX_SEQ * L_CTX
T_LAT = N_LAT_SEQ * L_LAT
T_ALL = T_CTX + T_LAT
DEPTH = 2
PAST = 512
GRID_W = 64
N_MOD = 9
EPS = 1e-6

H_HGRN, DK_HGRN, DV_HGRN = 4, 64, 64
W_HGRN = H_HGRN * DV_HGRN
CHUNK = 32
H_MLA, Q_RANK, KV_RANK = 8, 384, 256
D_NOPE, D_ROPE, D_V = 64, 32, 64
D_QK = D_NOPE + D_ROPE
HEAD_PAD = 128
W_HEADS = H_MLA * HEAD_PAD
W_MLA = H_MLA * D_V
G_FNET, C_FNET = 4, 64
W_FNET = G_FNET * C_FNET
D_FF = 2816
FF_CHUNK = 256
ROPE_THETA = 10000.0
LOG2_E = 1.4426950408889634

IN_HG = 5 * W_HGRN
IN_ARR = IN_HG + Q_RANK + KV_RANK + 2 * HEAD_PAD + W_FNET

TM = 512
N_TILES = T_ALL // TM
N_CTX_TILES = T_CTX // TM
LAT_TILES_PER_SEQ = L_LAT // TM
T_BLK = 256
TQ = 256

VMEM_LIMIT = 56 * 1024 * 1024


def _cparams(sem):
    return pltpu.CompilerParams(dimension_semantics=sem, vmem_limit_bytes=VMEM_LIMIT)


def _tile_group(i):
    return (i >= N_CTX_TILES).astype(jnp.int32) + (i >= N_CTX_TILES + LAT_TILES_PER_SEQ).astype(jnp.int32)


def _silu(x):
    return x * (1.0 / (1.0 + jnp.exp(-x)))


def _rms_rows(x, g):
    return x * lax.rsqrt(jnp.mean(x * x, axis=-1, keepdims=True) + EPS) * g


def _dot(a, b):
    return jnp.dot(a, b, preferred_element_type=F32)


def _dot_nt(a, b):
    return lax.dot_general(a, b, (((1,), (1,)), ((), ())), preferred_element_type=F32)


def _dot_tn(a, b):
    return lax.dot_general(a, b, (((0,), (0,)), ((), ())), preferred_element_type=F32)


def _dot_exact(a, b):
    return jnp.dot(a, b, preferred_element_type=F32, precision=HIGHEST)


def _row_spec(width, col_block=0, tile0=0):
    return pl.BlockSpec((TM, width), lambda i: (i + tile0, col_block))


def _ctx_row_spec(width, tile0=0):
    return pl.BlockSpec((TM, width), lambda i: (jnp.minimum(i + tile0, N_CTX_TILES - 1), 0))


def _lat_row_spec(width, tile0=0):
    return pl.BlockSpec((TM, width), lambda i: (jnp.maximum(i + tile0 - N_CTX_TILES, 0), 0))


def _mod_spec(layer, tile0=0):
    return pl.BlockSpec((None, None, N_MOD, D), lambda i: (layer, _tile_group(i + tile0), 0, 0))


def _resident(shape, index_map):
    return pl.BlockSpec(shape, index_map, pipeline_mode=pl.Buffered(1))


def _mods_kernel(c_ref, w_ref, b_ref, o_ref):
    a = _silu(c_ref[...]).astype(BF16)
    o_ref[...] = _dot(a, w_ref[...].astype(BF16)) + b_ref[...]


def _mods(cond8, ada_w, ada_b):
    tn = 1024
    return pl.pallas_call(
        _mods_kernel,
        grid=(DEPTH, N_MOD * D // tn),
        in_specs=[
            pl.BlockSpec((8, D), lambda l, j: (0, 0)),
            pl.BlockSpec((None, D, tn), lambda l, j: (l, 0, j)),
            pl.BlockSpec((None, 1, tn), lambda l, j: (l, 0, j)),
        ],
        out_specs=pl.BlockSpec((None, 8, tn), lambda l, j: (l, 0, j)),
        out_shape=jax.ShapeDtypeStruct((DEPTH, 8, N_MOD * D), F32),
        compiler_params=_cparams(("arbitrary", "arbitrary")),
        name="ada_mods",
    )(cond8, ada_w, ada_b.reshape(DEPTH, 1, N_MOD * D))


def _mixer_out(x, mod_ref, ctx, hf_ref, hb_ref, hgate_ref, gn_ref, ones_ref, ac_ref, al_ref, fc_ref, fl_ref, ga_ref,
               w_ref):
    gate = mod_ref[5:6, :]
    oh = hf_ref[...].astype(F32) + hb_ref[...].astype(F32)
    ms = _dot_exact(oh * oh, ones_ref[...])
    oh = oh * lax.rsqrt(ms + EPS) * gn_ref[...] * _silu(hgate_ref[...])
    oa = _rms_rows(jnp.where(ctx, ac_ref[...], al_ref[...]).astype(F32), ga_ref[...])
    of = jnp.where(ctx, fc_ref[...], fl_ref[...])
    o = (_dot(oh.astype(BF16), w_ref[0:W_HGRN, :])
         + _dot(oa.astype(BF16), w_ref[W_HGRN:W_HGRN + W_MLA, :])
         + _dot(of.astype(BF16), w_ref[W_HGRN + W_MLA:, :]))
    return x + gate * o


def _outproj_kernel(x_ref, mod_ref, *rest):
    o_ref = rest[-1]
    o_ref[...] = _mixer_out(x_ref[...], mod_ref, pl.program_id(0) < N_CTX_TILES, *rest[:-1])


def _outproj(x, mods, h_fwd, h_bwd, hg, gn_row, att_ctx, att_lat, fn_ctx, fn_lat, g_att, w_out, layer):
    lane = np.arange(W_HGRN)
    ones_bd = jnp.asarray((lane[:, None] // DV_HGRN == lane[None, :] // DV_HGRN).astype(np.float32) / DV_HGRN)
    return pl.pallas_call(
        _outproj_kernel,
        grid=(N_TILES,),
        in_specs=[_row_spec(D), _mod_spec(layer), _row_spec(W_HGRN), _row_spec(W_HGRN), _row_spec(W_HGRN, 4),
                  pl.BlockSpec((1, W_HGRN), lambda i: (0, 0)), pl.BlockSpec((W_HGRN, W_HGRN), lambda i: (0, 0)),
                  _ctx_row_spec(W_MLA), _lat_row_spec(W_MLA), _ctx_row_spec(W_FNET), _lat_row_spec(W_FNET),
                  pl.BlockSpec((None, 1, W_MLA), lambda i: (layer, 0, 0)),
                  _resident((None, D, D), lambda i: (layer, 0, 0))],
        out_specs=_row_spec(D),
        out_shape=jax.ShapeDtypeStruct((T_ALL, D), F32),
        compiler_params=_cparams(("arbitrary",)),
        name=f"outproj_l{layer}",
    )(x, mods, h_fwd, h_bwd, hg, gn_row, ones_bd, att_ctx, att_lat, fn_ctx, fn_lat,
      g_att.reshape(DEPTH, 1, W_MLA), w_out)


def _ffn_kernel(*refs, mi, two_sources):
    if two_sources:
        xc_ref, xl_ref, mod_ref, g_ref, wg_ref, wu_ref, wd_ref, o_ref = refs
        x = jnp.where(pl.program_id(0) < N_CTX_TILES, xc_ref[...], xl_ref[...])
    else:
        x_ref, mod_ref, g_ref, wg_ref, wu_ref, wd_ref, o_ref = refs
        x = x_ref[...]
    shift = mod_ref[mi:mi + 1, :]
    scale = mod_ref[mi + 1:mi + 2, :]
    gate = mod_ref[mi + 2:mi + 3, :]
    hb = (_rms_rows(x, g_ref[...]) * (1.0 + scale) + shift).astype(BF16)
    acc = jnp.zeros(x.shape, F32)
    for j in range(D_FF // FF_CHUNK):
        cs = slice(j * FF_CHUNK, (j + 1) * FF_CHUNK)
        a = _silu(_dot(hb, wg_ref[:, cs])) * _dot(hb, wu_ref[:, cs])
        acc = acc + _dot(a.astype(BF16), wd_ref[cs, :])
    o_ref[...] = x + 0.5 * gate * acc


def _ffn(xs, mods, norm_g, w_gu, w_down, layer, which, *, src_tile0=0, n_tiles=N_TILES, out_rows=T_ALL):
    mi = 0 if which == 0 else 6
    gi = 0 if which == 0 else 2
    two = isinstance(xs, tuple)
    if two:
        in_specs = [_ctx_row_spec(D), _lat_row_spec(D)]
        args = list(xs)
    else:
        in_specs = [_row_spec(D, tile0=src_tile0)]
        args = [xs]
    in_specs += [
        _mod_spec(layer, src_tile0),
        pl.BlockSpec((None, None, 1, D), lambda i: (layer, gi, 0, 0)),
        _resident((None, None, D, D_FF), lambda i: (layer, which, 0, 0)),
        _resident((None, None, D, D_FF), lambda i: (layer, which, 0, 1)),
        _resident((None, None, D_FF, D), lambda i: (layer, which, 0, 0)),
    ]
    args += [mods, norm_g.reshape(DEPTH, 3, 1, D), w_gu, w_gu, w_down]
    return pl.pallas_call(
        functools.partial(_ffn_kernel, mi=mi, two_sources=two),
        grid=(n_tiles,),
        in_specs=in_specs,
        out_specs=pl.BlockSpec((TM, D), lambda i: (i, 0)),
        out_shape=jax.ShapeDtypeStruct((out_rows, D), F32),
        compiler_params=_cparams(("arbitrary",)),
        name=f"ffn_l{layer}_{which}",
    )(*args)


def _mla_heads(qa, ka, va, kr, kr_sw, tabs, q_ref, k_ref, v_ref):
    cq, sq, ck, sk = tabs
    lane = lax.broadcasted_iota(jnp.int32, (1, W_HEADS), 1) % HEAD_PAD
    v_ref[...] = jnp.where(lane == D_V, 1.0, va).astype(BF16)
    k_rot = kr_sw * sk
    heads = [slice(h * HEAD_PAD, (h + 1) * HEAD_PAD) for h in range(H_MLA)]

    def inv_rms(xs):
        return [lax.rsqrt(jnp.sum(x * x, axis=-1, keepdims=True) * (1.0 / D_QK) + EPS) for x in xs]
    if qa is not None:
        q = [qa[:, hs] for hs in heads]
        q_sw = [qa[:, W_HEADS + hs.start:W_HEADS + hs.stop] for hs in heads]
        for hs, x, x_sw, rs in zip(heads, q, q_sw, inv_rms(q)):
            q_ref[:, hs] = (rs * (x * cq + x_sw * sq)).astype(BF16)
    k = [ka[:, hs] + kr for hs in heads]
    for hs, x, rs in zip(heads, k, inv_rms(k)):
        k_ref[:, hs] = (rs * (x * ck + k_rot)).astype(BF16)


def _inproj_kernel(x_ref, mod_ref, g_ref, w_ref, gq_ref, gkv_ref, wq_ref, wk_ref, wv_ref, cq_ref, sq_ref, ck_ref,
                   sk_ref, hg_ref, ckv_ref, kr_ref, uf_ref, q_ref, k_ref, v_ref):
    x = x_ref[...]
    shift = mod_ref[3:4, :]
    scale = mod_ref[4:5, :]
    hb = (_rms_rows(x, g_ref[...]) * (1.0 + scale) + shift).astype(BF16)
    u = _dot(hb, w_ref[...])
    o = 0
    hg_ref[...] = u[:, o:o + IN_HG]
    o += IN_HG
    cqn = _rms_rows(u[:, o:o + Q_RANK], gq_ref[...])
    o += Q_RANK
    ckvn = _rms_rows(u[:, o:o + KV_RANK], gkv_ref[...])
    ckv_ref[...] = ckvn
    o += KV_RANK
    kr = u[:, o:o + HEAD_PAD]
    kr_ref[...] = kr
    o += HEAD_PAD
    kr_sw = u[:, o:o + HEAD_PAD]
    o += HEAD_PAD
    uf_ref[...] = u[:, o:o + W_FNET]
    cb = ckvn.astype(BF16)
    _mla_heads(_dot(cqn.astype(BF16), wq_ref[...]), _dot(cb, wk_ref[...]), _dot(cb, wv_ref[...]), kr, kr_sw,
               (cq_ref[...], sq_ref[...], ck_ref[...], sk_ref[...]), q_ref, k_ref, v_ref)


def _rope_block(i):
    return jnp.where(i < N_CTX_TILES, 0, 1 + (i - N_CTX_TILES) % LAT_TILES_PER_SEQ)


def _inproj(x, mods, norm_g, w_in_arr, gq, gkv, wq2, wk_arr, wv_arr, tabs, layer):
    tab = pl.BlockSpec((None, None, TM, HEAD_PAD), lambda i: (layer, _rope_block(i), 0, 0))
    f32_widths = (IN_HG, KV_RANK, HEAD_PAD, W_FNET)
    return pl.pallas_call(
        _inproj_kernel,
        grid=(N_TILES,),
        in_specs=[
            _row_spec(D),
            _mod_spec(layer),
            pl.BlockSpec((None, None, 1, D), lambda i: (layer, 1, 0, 0)),
            _resident((None, D, IN_ARR), lambda i: (layer, 0, 0)),
            pl.BlockSpec((None, 1, Q_RANK), lambda i: (layer, 0, 0)),
            pl.BlockSpec((None, 1, KV_RANK), lambda i: (layer, 0, 0)),
            _resident((None, Q_RANK, 2 * W_HEADS), lambda i: (layer, 0, 0)),
            _resident((None, KV_RANK, W_HEADS), lambda i: (layer, 0, 0)),
            _resident((None, KV_RANK, W_HEADS), lambda i: (layer, 0, 0)),
            tab, tab, tab, tab,
        ],
        out_specs=[_row_spec(w) for w in f32_widths] + [_row_spec(W_HEADS)] * 3,
        out_shape=[jax.ShapeDtypeStruct((T_ALL, w), F32) for w in f32_widths]
        + [jax.ShapeDtypeStruct((T_ALL, W_HEADS), BF16)] * 3,
        compiler_params=_cparams(("arbitrary",)),
        name=f"inproj_l{layer}",
    )(x, mods, norm_g.reshape(DEPTH, 3, 1, D), w_in_arr, gq.reshape(DEPTH, 1, Q_RANK),
      gkv.reshape(DEPTH, 1, KV_RANK), wq2, wk_arr, wv_arr, *tabs)


def _kv_cache_kernel(ckv_ref, kr_ref, krsw_ref, wk_ref, wv_ref, ck_ref, sk_ref, k_ref, v_ref):
    cb = ckv_ref[...].astype(BF16)
    _mla_heads(None, _dot(cb, wk_ref[...]), _dot(cb, wv_ref[...]), kr_ref[...], krsw_ref[...],
               (None, None, ck_ref[...], sk_ref[...]), None, k_ref, v_ref)


def _kv_cache(cache_ckv_l, cache_kr, cache_kr_sw, wk_arr, wv_arr, tabs, layer):
    assert PAST == TM
    blk = lambda w: pl.BlockSpec((None, PAST, w), lambda b: (b, 0, 0))
    tab = pl.BlockSpec((None, None, TM, HEAD_PAD), lambda b: (layer, 0, 0, 0))
    return pl.pallas_call(
        _kv_cache_kernel,
        grid=(N_LAT_SEQ,),
        in_specs=[blk(KV_RANK), blk(HEAD_PAD), blk(HEAD_PAD),
                  _resident((None, KV_RANK, W_HEADS), lambda b: (layer, 0, 0)),
                  _resident((None, KV_RANK, W_HEADS), lambda b: (layer, 0, 0)), tab, tab],
        out_specs=[blk(W_HEADS)] * 2,
        out_shape=[jax.ShapeDtypeStruct((N_LAT_SEQ, PAST, W_HEADS), BF16)] * 2,
        compiler_params=_cparams(("arbitrary",)),
        name=f"mla_kv_cache_l{layer}",
    )(cache_ckv_l, cache_kr, cache_kr_sw, wk_arr, wv_arr, tabs[2], tabs[3])


N_SEQ = N_CTX_SEQ + N_LAT_SEQ
BLK_PER_LAT = L_LAT // T_BLK
N_HGRN_STEPS = N_CTX_SEQ + N_LAT_SEQ * BLK_PER_LAT


def _hgrn_seq(i):
    return jnp.where(i < N_CTX_SEQ, i, N_CTX_SEQ + (i - N_CTX_SEQ) // BLK_PER_LAT)


def _hgrn_blk(i, reverse):
    j = (i - N_CTX_SEQ) % BLK_PER_LAT
    if reverse:
        j = BLK_PER_LAT - 1 - j
    lat = N_CTX_SEQ + ((i - N_CTX_SEQ) // BLK_PER_LAT) * BLK_PER_LAT + j
    return jnp.where(i < N_CTX_SEQ, i, lat)


def _hgrn_kernel(qf_ref, ff_ref, vf_ref, qb_ref, fb_ref, vb_ref, lb_ref, s0f_ref, s0b_ref,
                 tri_ref, tri4_ref, hm_ref, bd_ref, of_ref, ob_ref, sf_ref, sb_ref, stf_scr, stb_scr):
    i = pl.program_id(0)
    first = jnp.logical_or(i < N_CTX_SEQ, (i - N_CTX_SEQ) % BLK_PER_LAT == 0)
    heads = [slice(h * DK_HGRN, (h + 1) * DK_HGRN) for h in range(H_HGRN)]

    @pl.when(first)
    def _():
        for s0_ref, scr in ((s0f_ref, stf_scr), (s0b_ref, stb_scr)):
            scr[...] = jnp.zeros(scr.shape, F32)
            for h, hs in enumerate(heads):
                scr[hs, hs] = s0_ref[h]

    lb = lb_ref[...]
    loglb = jnp.log(lb)
    log1mlb = jnp.log(1.0 - lb)
    n_chunks = T_BLK // CHUNK
    dirs = ((qf_ref, ff_ref, vf_ref, of_ref, stf_scr), (qb_ref, fb_ref, vb_ref, ob_ref, stb_scr))
    units = [(d, c if d == 0 else n_chunks - 1 - c) for c in range(n_chunks) for d in (0, 1)]
    rows = [slice(cc * CHUNK, (cc + 1) * CHUNK) for _, cc in units]
    end_row = (CHUNK - 1, 0)
    mid_row = (CHUNK // 2 - 1, CHUNK // 2)
    hm, bd = hm_ref[...], bd_ref[...]

    q = [_silu(dirs[d][0][r, :]) for (d, _), r in zip(units, rows)]
    v = [dirs[d][2][r, :] for (d, _), r in zip(units, rows)]
    g = []
    for (d, _), r in zip(units, rows):
        x = dirs[d][1][r, :]
        y = log1mlb[d:d + 1] + (jnp.minimum(x, 0.0) - jnp.log(1.0 + jnp.exp(-jnp.abs(x))))
        g.append(jnp.maximum(loglb[d:d + 1], y) + jnp.log(1.0 + jnp.exp(-jnp.abs(loglb[d:d + 1] - y))))
    kk = [1.0 - jnp.exp(gu) for gu in g]
    b = [_dot_exact(tri_ref[d], gu) for (d, _), gu in zip(units, g)]
    b_end = [bu[end_row[d]:end_row[d] + 1, :] for (d, _), bu in zip(units, b)]
    b_mid = [bu[mid_row[d]:mid_row[d] + 1, :] for (d, _), bu in zip(units, b)]
    q_in = [(qu * jnp.exp(bu)).astype(BF16) for qu, bu in zip(q, b)]
    q_t = [(qu * jnp.exp(bu - bm)).astype(BF16) for qu, bu, bm in zip(q, b, b_mid)]
    k_t = [ku * jnp.exp(bm - bu) for ku, bu, bm in zip(kk, b, b_mid)]
    k_e = [(ku * jnp.exp(be - bu)).astype(BF16) for ku, bu, be in zip(kk, b, b_end)]
    decay = [jnp.exp(be) for be in b_end]
    k_bd = [(jnp.concatenate([ku] * H_HGRN, axis=0) * hm).astype(BF16) for ku in k_t]
    v_bd = [(jnp.concatenate([vu] * H_HGRN, axis=0) * hm).astype(BF16) for vu in v]
    sc = [(_dot_nt(qu, ku) * tri4_ref[d]).astype(BF16) for (d, _), qu, ku in zip(units, q_t, k_bd)]
    o_intra = [_dot(su, vu) for su, vu in zip(sc, v_bd)]
    kv = [_dot_tn(vu.astype(BF16), ku) * bd for vu, ku in zip(v, k_e)]

    st = [stf_scr[...], stb_scr[...]]
    for u, (d, _) in enumerate(units):
        dirs[d][3][rows[u], :] = (o_intra[u] + _dot_nt(q_in[u], st[d].astype(BF16))).astype(BF16)
        st[d] = st[d] * decay[u] + kv[u]
    stf_scr[...] = st[0]
    stb_scr[...] = st[1]
    for h, hs in enumerate(heads):
        sf_ref[h] = st[0][hs, hs]
        sb_ref[h] = st[1][hs, hs]


def _hgrn_consts():
    t = np.arange(CHUNK)
    tri = np.stack([t[:, None] >= t[None, :], t[:, None] <= t[None, :]]).astype(np.float32)
    tri4 = np.tile(tri, (1, 1, H_HGRN))
    r = np.arange(H_HGRN * CHUNK)
    lane = np.arange(W_HGRN)
    hm = (r[:, None] // CHUNK == lane[None, :] // DK_HGRN).astype(np.float32)
    bd = (lane[:, None] // DV_HGRN == lane[None, :] // DK_HGRN).astype(np.float32)
    return jnp.asarray(tri), jnp.asarray(tri4), jnp.asarray(hm), jnp.asarray(bd)


def _hgrn(hg, lb2, s0f, s0b):
    tri, tri4, hm, bd = _hgrn_consts()

    def col(cb, reverse):
        return pl.BlockSpec((T_BLK, W_HGRN), lambda i: (_hgrn_blk(i, reverse), cb))

    def const(shape):
        return pl.BlockSpec(shape, lambda i: (0,) * len(shape))
    state = pl.BlockSpec((None, H_HGRN, DV_HGRN, DK_HGRN), lambda i: (_hgrn_seq(i), 0, 0, 0))
    return pl.pallas_call(
        _hgrn_kernel,
        grid=(N_HGRN_STEPS,),
        in_specs=[col(0, False), col(1, False), col(3, False), col(0, True), col(2, True), col(3, True),
                  const((2, W_HGRN)), state, state,
                  const(tri.shape), const(tri4.shape), const(hm.shape), const(bd.shape)],
        out_specs=[col(0, False), col(0, True), state, state],
        out_shape=[jax.ShapeDtypeStruct((T_ALL, W_HGRN), BF16)] * 2
        + [jax.ShapeDtypeStruct((N_SEQ, H_HGRN, DV_HGRN, DK_HGRN), F32)] * 2,
        scratch_shapes=[pltpu.VMEM((W_HGRN, W_HGRN), F32)] * 2,
        compiler_params=_cparams(("arbitrary",)),
        name="hgrn",
    )(hg, hg, hg, hg, hg, hg, lb2, s0f, s0b, tri, tri4, hm, bd)


def _attn_kernel(q_ref, k_ref, v_ref, o_ref):
    lane = lax.broadcasted_iota(jnp.int32, (1, HEAD_PAD), 1)
    heads = [slice(h * HEAD_PAD, (h + 1) * HEAD_PAD) for h in range(q_ref.shape[1] // HEAD_PAD)]
    s = [_dot_nt(q_ref[:, hs], k_ref[:, hs]) for hs in heads]
    p = [jnp.exp2(sh - jnp.max(sh, axis=-1, keepdims=True)).astype(BF16) for sh in s]
    pv = [_dot(ph, v_ref[:, hs]) for ph, hs in zip(p, heads)]
    o = [jnp.where(lane < D_V, x * (1.0 / x[:, D_V:D_V + 1]), 0.0) for x in pv]
    for hp in range(len(heads) // 2):
        o_ref[:, hp * 2 * D_V:(hp + 1) * 2 * D_V] = (o[2 * hp] + pltpu.roll(o[2 * hp + 1], D_V, axis=1)).astype(BF16)


def _attn_lat_kernel(q_ref, kp_ref, vp_ref, kn_ref, vn_ref, o_ref, k_scr, v_scr):
    @pl.when(pl.program_id(2) == 0)
    def _():
        k_scr[0:PAST, :] = kp_ref[...]
        k_scr[PAST:, :] = kn_ref[...]
        v_scr[0:PAST, :] = vp_ref[...]
        v_scr[PAST:, :] = vn_ref[...]
    _attn_kernel(q_ref, k_scr, v_scr, o_ref)


def _attention_ctx(q, k, v):
    blk = pl.BlockSpec((L_CTX, W_HEADS), lambda b: (b, 0))
    return pl.pallas_call(
        _attn_kernel,
        grid=(N_CTX_SEQ,),
        in_specs=[blk, blk, blk],
        out_specs=pl.BlockSpec((L_CTX, W_MLA), lambda b: (b, 0)),
        out_shape=jax.ShapeDtypeStruct((T_CTX, W_MLA), BF16),
        compiler_params=_cparams(("arbitrary",)),
        name="attn_ctx",
    )(q, k, v)


def _attention_lat(q, k, v, k_past, v_past):
    pair = 2 * HEAD_PAD
    nq = L_LAT // TQ
    q0 = T_CTX // TQ
    seq0 = T_CTX // L_LAT
    new = pl.BlockSpec((L_LAT, pair), lambda b, hp, qi: (seq0 + b, hp))
    past = pl.BlockSpec((None, PAST, pair), lambda b, hp, qi: (b, 0, hp))
    return pl.pallas_call(
        _attn_lat_kernel,
        grid=(N_LAT_SEQ, H_MLA // 2, nq),
        in_specs=[pl.BlockSpec((TQ, pair), lambda b, hp, qi: (q0 + b * nq + qi, hp)), past, past, new, new],
        out_specs=pl.BlockSpec((TQ, 2 * D_V), lambda b, hp, qi: (b * nq + qi, hp)),
        out_shape=jax.ShapeDtypeStruct((T_LAT, W_MLA), BF16),
        scratch_shapes=[pltpu.VMEM((PAST + L_LAT, pair), BF16)] * 2,
        compiler_params=_cparams(("arbitrary", "arbitrary", "arbitrary")),
        name="attn_lat",
    )(q, k_past, v_past, k, v)


def _dft_tables(n):
    a = 2.0 * np.pi * np.outer(np.arange(n), np.arange(n)) / n
    return np.cos(a), np.sin(a)


def _bf16_operand(table):
    return jnp.asarray(table, F32).astype(BF16)


def _fnet_finish(re, im, cs_ref, w_ref, g_ref):
    spec = _dot(jnp.concatenate([re, im], axis=1).astype(BF16), cs_ref[...])
    return _rms_rows(_dot(spec.astype(BF16), w_ref[...]), g_ref[...])


def _fnet_channel_consts(w_bd, g_row):
    c, s = _dft_tables(C_FNET)
    eye = np.eye(G_FNET)
    return [_bf16_operand(np.concatenate([np.kron(eye, c), np.kron(eye, s)], axis=0)), w_bd, g_row]


def _fnet_channel_specs():
    zero = lambda *a: (0, 0)
    return [pl.BlockSpec((2 * W_FNET, W_FNET), zero), pl.BlockSpec((W_FNET, W_FNET), zero),
            pl.BlockSpec((1, W_FNET), zero)]


def _fnet_ctx_kernel(u_ref, m_ref, cs_ref, w_ref, g_ref, o_ref):
    p = _dot(m_ref[...], u_ref[...].astype(BF16))
    o_ref[...] = _fnet_finish(p[:L_CTX], p[L_CTX:], cs_ref, w_ref, g_ref)


def _fnet_ctx(uf, w_bd, g_row):
    c, s = _dft_tables(L_CTX)
    norm = 1.0 / np.sqrt(L_CTX * C_FNET)
    blk = pl.BlockSpec((L_CTX, W_FNET), lambda b: (b, 0))
    return pl.pallas_call(
        _fnet_ctx_kernel,
        grid=(N_CTX_SEQ,),
        in_specs=[blk, pl.BlockSpec((2 * L_CTX, L_CTX), lambda b: (0, 0))] + _fnet_channel_specs(),
        out_specs=blk,
        out_shape=jax.ShapeDtypeStruct((T_CTX, W_FNET), F32),
        compiler_params=_cparams(("arbitrary",)),
        name="fnet_ctx",
    )(uf, _bf16_operand(np.concatenate([c, -s], axis=0) * norm), *_fnet_channel_consts(w_bd, g_row))


FN_SUB = 8


def _fnet_rows_kernel(x_ref, m_ref, re_ref, im_ref):
    for s in range(FN_SUB):
        y = _dot(m_ref[s], x_ref[:, s, :].astype(BF16))
        re_ref[s] = y[:GRID_W]
        im_ref[s] = y[GRID_W:]


def _fnet_cols_kernel(re_ref, im_ref, m_ref, cs_ref, w_ref, g_ref, o_ref):
    z = [_dot(m_ref[...], jnp.concatenate([re_ref[:, s, :], im_ref[:, s, :]], axis=0).astype(BF16))
         for s in range(FN_SUB)]
    out = _fnet_finish(jnp.concatenate([zs[:GRID_W] for zs in z], axis=0),
                       jnp.concatenate([zs[GRID_W:] for zs in z], axis=0), cs_ref, w_ref, g_ref)
    for s in range(FN_SUB):
        o_ref[:, s, :] = out[s * GRID_W:(s + 1) * GRID_W]


def _fnet_lat(uf, w_bd, g_row):
    p1 = np.arange(GRID_W)[None, :, None]
    l = GRID_W * np.arange(GRID_W)[None, None, :] + np.arange(GRID_W)[:, None, None]
    ang = 2.0 * np.pi * p1 * l / L_LAT
    norm = 1.0 / np.sqrt(L_LAT * C_FNET)
    m_rows = _bf16_operand(np.concatenate([np.cos(ang), -np.sin(ang)], axis=1) * norm)
    c64, s64 = _dft_tables(GRID_W)
    m_cols = _bf16_operand(np.block([[c64, s64], [-s64, c64]]))
    x4 = uf.reshape(T_ALL // L_LAT, GRID_W, GRID_W, W_FNET)
    seq0 = T_CTX // L_LAT
    grid = (N_LAT_SEQ, GRID_W // FN_SUB)
    mid = pl.BlockSpec((None, FN_SUB, GRID_W, W_FNET), lambda b, j: (b, j, 0, 0))
    mid_shape = jax.ShapeDtypeStruct((N_LAT_SEQ, GRID_W, GRID_W, W_FNET), F32)
    re, im = pl.pallas_call(
        _fnet_rows_kernel,
        grid=grid,
        in_specs=[pl.BlockSpec((None, GRID_W, FN_SUB, W_FNET), lambda b, j: (seq0 + b, 0, j, 0)),
                  pl.BlockSpec((FN_SUB, 2 * GRID_W, GRID_W), lambda b, j: (j, 0, 0))],
        out_specs=[mid, mid],
        out_shape=[mid_shape, mid_shape],
        compiler_params=_cparams(("arbitrary", "arbitrary")),
        name="fnet_rows_lat",
    )(x4, m_rows)
    strided = pl.BlockSpec((None, GRID_W, FN_SUB, W_FNET), lambda b, j: (b, 0, j, 0))
    out = pl.pallas_call(
        _fnet_cols_kernel,
        grid=grid,
        in_specs=[strided, strided, pl.BlockSpec((2 * GRID_W, 2 * GRID_W), lambda b, j: (0, 0))]
        + _fnet_channel_specs(),
        out_specs=strided,
        out_shape=mid_shape,
        compiler_params=_cparams(("arbitrary", "arbitrary")),
        name="fnet_cols_lat",
    )(re, im, m_cols, *_fnet_channel_consts(w_bd, g_row))
    return out.reshape(T_LAT, W_FNET)


def _pad_heads(w, d_head):
    lead = w.shape[:-1]
    w = w.reshape(lead + (H_MLA, d_head))
    w = jnp.pad(w, [(0, 0)] * len(lead) + [(0, 0), (0, HEAD_PAD - d_head)])
    return w.reshape(lead + (W_HEADS,))


def _swap_rope(a):
    n = D_ROPE // 4
    parts = [jnp.zeros(a.shape[:-1] + (D_NOPE,), a.dtype)]
    for ax in range(2):
        base = D_NOPE + ax * 2 * n
        parts += [a[..., base + n:base + 2 * n], a[..., base:base + n]]
    parts.append(jnp.zeros(a.shape[:-1] + (HEAD_PAD - D_QK,), a.dtype))
    return jnp.concatenate(parts, axis=-1)


def _rope_tables(g_q, g_k):
    n_freq = D_ROPE // 4
    t = np.arange(L_LAT)
    pos = np.stack([t // GRID_W, t % GRID_W], axis=-1).astype(np.float32)
    freq = (np.float32(ROPE_THETA) ** (-np.arange(n_freq, dtype=np.float32) / n_freq)).astype(np.float32)
    ang = (pos[:, :, None] * freq).astype(np.float32)
    cos = np.ones((L_LAT, HEAD_PAD), np.float32)
    sin = np.zeros((L_LAT, HEAD_PAD), np.float32)
    for ax in range(2):
        base = D_NOPE + ax * 2 * n_freq
        c, s = np.cos(ang[:, ax, :]), np.sin(ang[:, ax, :])
        cos[:, base:base + n_freq] = c
        cos[:, base + n_freq:base + 2 * n_freq] = c
        sin[:, base:base + n_freq] = -s
        sin[:, base + n_freq:base + 2 * n_freq] = s
    cos = np.concatenate([np.ones((TM, HEAD_PAD), np.float32), cos]).reshape(1, 1 + LAT_TILES_PER_SEQ, TM, HEAD_PAD)
    sin = np.concatenate([np.zeros((TM, HEAD_PAD), np.float32), sin]).reshape(1, 1 + LAT_TILES_PER_SEQ, TM, HEAD_PAD)

    def pair(g, scale):
        g128 = jnp.pad(g, ((0, 0), (0, HEAD_PAD - D_QK))) * scale
        g_partner = _swap_rope(g128)
        return (cos * g128[:, None, None, :], sin * g_partner[:, None, None, :])
    return pair(g_q, D_QK ** -0.5 * LOG2_E) + pair(g_k, 1.0)


def kernel(x_prompt, x_sample, cache_ckv, cache_krope, state_hgrn, c, c_ctx, ada_w, ada_b, norm_g, ffn_w_gu, ffn_w_down, w_in, hgrn_lb, hgrn_norm_g, mla_q_norm_g, mla_w_q_up, mla_kv_norm_g, mla_w_kv_up, mla_qk_norm_g, mla_out_norm_g, fnet_w, fnet_norm_g, w_out):
    w_gu = ffn_w_gu.astype(BF16)
    w_down = ffn_w_down.astype(BF16)
    o = np.cumsum((0, 5 * W_HGRN, Q_RANK, KV_RANK, D_ROPE, W_FNET))
    w_kr = jnp.pad(w_in[:, :, o[3]:o[4]], ((0, 0), (0, 0), (D_NOPE, HEAD_PAD - D_NOPE - D_ROPE)))
    w_in_arr = jnp.concatenate([w_in[:, :, :o[3]], w_kr, _swap_rope(w_kr), w_in[:, :, o[4]:]], axis=-1).astype(BF16)
    wq = _pad_heads(mla_w_q_up, D_QK)
    wq_sw = _swap_rope(wq.reshape(DEPTH, Q_RANK, H_MLA, HEAD_PAD)).reshape(DEPTH, Q_RANK, W_HEADS)
    wq2 = jnp.concatenate([wq, wq_sw], axis=-1).astype(BF16)
    w_kv = mla_w_kv_up.reshape(DEPTH, KV_RANK, H_MLA, D_NOPE + D_V)
    wk_arr = _pad_heads(w_kv[..., :D_NOPE].reshape(DEPTH, KV_RANK, H_MLA * D_NOPE), D_NOPE).astype(BF16)
    wv_arr = _pad_heads(w_kv[..., D_NOPE:].reshape(DEPTH, KV_RANK, H_MLA * D_V), D_V).astype(BF16)
    w_out_b = w_out.astype(BF16)
    eye_g = jnp.eye(G_FNET, dtype=F32)
    fnet_bd = jnp.einsum("lgcd,gh->lgchd", fnet_w, eye_g).reshape(DEPTH, W_FNET, W_FNET).astype(BF16)
    tabs = _rope_tables(mla_qk_norm_g[:, 0], mla_qk_norm_g[:, 1])
    lbs = jnp.cumsum(jax.nn.softmax(hgrn_lb.astype(F32), axis=0), axis=0)
    lbs = lbs - lbs[:1]

    cond8 = jnp.zeros((8, D), F32).at[0].set(c_ctx).at[1:1 + N_LAT_SEQ].set(c)
    mods = _mods(cond8, ada_w, ada_b).reshape(DEPTH, 8, N_MOD, D)

    x = (x_prompt.reshape(T_CTX, D), x_sample.reshape(T_LAT, D))
    ckv_out, kr_out, st_out = [], [], []
    zero_states = jnp.zeros((N_CTX_SEQ, H_HGRN, DV_HGRN, DK_HGRN), F32)
    for l in range(DEPTH):
        x = _ffn(x, mods, norm_g, w_gu, w_down, l, 0)
        hg, ckvn, kr128, uf, q, k, v = _inproj(x, mods, norm_g, w_in_arr, mla_q_norm_g, mla_kv_norm_g, wq2, wk_arr,
                                               wv_arr, tabs, l)

        s0 = [jnp.concatenate([zero_states, jnp.swapaxes(state_hgrn[:, l, d], -1, -2)], axis=0) for d in range(2)]
        h_fwd, h_bwd, s_f, s_b = _hgrn(hg, lbs[l], s0[0], s0[1])
        gn_row = jnp.tile(hgrn_norm_g[l], H_HGRN).reshape(1, W_HGRN)
        st_out.append(jnp.swapaxes(jnp.stack([s_f[:N_CTX_SEQ], s_b[:N_CTX_SEQ]], axis=1), -1, -2))

        cache_kr = jnp.pad(cache_krope[:, l], ((0, 0), (0, 0), (D_NOPE, HEAD_PAD - D_NOPE - D_ROPE)))
        k_past, v_past = _kv_cache(cache_ckv[:, l], cache_kr, _swap_rope(cache_kr), wk_arr, wv_arr, tabs, l)
        att_ctx = _attention_ctx(q, k, v)
        att_lat = _attention_lat(q, k, v, k_past, v_past)

        fn_gain = fnet_norm_g[l].reshape(1, W_FNET)
        fn_ctx = _fnet_ctx(uf, fnet_bd[l], fn_gain)
        fn_lat = _fnet_lat(uf, fnet_bd[l], fn_gain)

        x = _outproj(x, mods, h_fwd, h_bwd, hg, gn_row, att_ctx, att_lat, fn_ctx, fn_lat, mla_out_norm_g, w_out_b, l)
        if l == DEPTH - 1:
            y_prompt = _ffn(x, mods, norm_g, w_gu, w_down, l, 1, n_tiles=N_CTX_TILES, out_rows=T_CTX)
            y_sample = _ffn(x, mods, norm_g, w_gu, w_down, l, 1, src_tile0=N_CTX_TILES, n_tiles=N_TILES - N_CTX_TILES,
                            out_rows=T_LAT)
        else:
            x = _ffn(x, mods, norm_g, w_gu, w_down, l, 1)

        ckv_out.append(ckvn[:T_CTX].reshape(N_CTX_SEQ, L_CTX, KV_RANK))
        kr_out.append(kr128[:T_CTX, D_NOPE:D_NOPE + D_ROPE].reshape(N_CTX_SEQ, L_CTX, D_ROPE))

    return (y_prompt.reshape(N_CTX_SEQ, L_CTX, D), y_sample.reshape(N_LAT_SEQ, L_LAT, D),
            jnp.stack(ckv_out, axis=1), jnp.stack(kr_out, axis=1), jnp.stack(st_out, axis=1))
```

```python
import functools

import numpy as np
import jax
import jax.numpy as jnp
from jax import lax
from jax.experimental import pallas as pl
from jax.experimental.pallas import tpu as pltpu

F32 = jnp.float32
BF16 = jnp.bfloat16
HIGHEST = lax.Precision.HIGHEST

D = 1024
N_CTX_SEQ, L_CTX = 32, 256
N_LAT_SEQ, L_LAT = 2, 4096
T_CTX = N_CTX_SEQ * L_CTX
T_LAT = N_LAT_SEQ * L_LAT
T_ALL = T_CTX + T_LAT
DEPTH = 2
PAST = 512
GRID_W = 64
N_MOD = 9
EPS = 1e-6

H_HGRN, DK_HGRN, DV_HGRN = 4, 64, 64
W_HGRN = H_HGRN * DV_HGRN
CHUNK = 32
HGRN_SAFE_LOG_DECAY = 60.0
H_MLA, Q_RANK, KV_RANK = 8, 384, 256
D_NOPE, D_ROPE, D_V = 64, 32, 64
D_QK = D_NOPE + D_ROPE
HEAD_PAD = 128
W_HEADS = H_MLA * HEAD_PAD
W_MLA = H_MLA * D_V
G_FNET, C_FNET = 4, 64
W_FNET = G_FNET * C_FNET
D_FF = 2816
FF_CHUNK = 256
ROPE_THETA = 10000.0
LOG2_E = 1.4426950408889634

IN_HG = 5 * W_HGRN
IN_ARR = IN_HG + Q_RANK + KV_RANK + 2 * HEAD_PAD + W_FNET

TM = 512
N_TILES = T_ALL // TM
N_CTX_TILES = T_CTX // TM
LAT_TILES_PER_SEQ = L_LAT // TM
T_BLK = 256
TQ = 256
ATT_HEADS = 4

VMEM_LIMIT = 56 * 1024 * 1024


def _cparams(sem):
    return pltpu.CompilerParams(dimension_semantics=sem, vmem_limit_bytes=VMEM_LIMIT)


def _tile_group(i):
    return (i >= N_CTX_TILES).astype(jnp.int32) + (i >= N_CTX_TILES + LAT_TILES_PER_SEQ).astype(jnp.int32)


def _silu(x):
    return x * (1.0 / (1.0 + jnp.exp(-x)))


def _rms_rows(x, g):
    return x * lax.rsqrt(jnp.mean(x * x, axis=-1, keepdims=True) + EPS) * g


def _dot(a, b):
    return jnp.dot(a, b, preferred_element_type=F32)


def _dot_nt(a, b):
    return lax.dot_general(a, b, (((1,), (1,)), ((), ())), preferred_element_type=F32)


def _dot_tn(a, b):
    return lax.dot_general(a, b, (((0,), (0,)), ((), ())), preferred_element_type=F32)


def _dot_exact(a, b):
    return jnp.dot(a, b, preferred_element_type=F32, precision=HIGHEST)


def _step_tile(i):
    return i


def _row_spec(width, col_block=0, tile=_step_tile):
    return pl.BlockSpec((TM, width), lambda i: (tile(i), col_block))


def _ctx_row_spec(width, tile=_step_tile):
    return pl.BlockSpec((TM, width), lambda i: (jnp.minimum(tile(i), N_CTX_TILES - 1), 0))


def _lat_row_spec(width, tile=_step_tile):
    return pl.BlockSpec((TM, width), lambda i: (jnp.maximum(tile(i) - N_CTX_TILES, 0), 0))


def _mod_spec(layer, tile=_step_tile):
    return pl.BlockSpec((None, None, N_MOD, D), lambda i: (layer, _tile_group(tile(i)), 0, 0))


def _resident(shape, index_map):
    return pl.BlockSpec(shape, index_map, pipeline_mode=pl.Buffered(1))


def _mods_kernel(c_ref, w_ref, b_ref, o_ref):
    a = _silu(c_ref[...]).astype(BF16)
    o_ref[...] = _dot(a, w_ref[...].astype(BF16)) + b_ref[...]


def _mods(cond8, ada_w, ada_b):
    tn = 1024
    return pl.pallas_call(
        _mods_kernel,
        grid=(DEPTH, N_MOD * D // tn),
        in_specs=[
            pl.BlockSpec((8, D), lambda l, j: (0, 0)),
            pl.BlockSpec((None, D, tn), lambda l, j: (l, 0, j)),
            pl.BlockSpec((None, 1, tn), lambda l, j: (l, 0, j)),
        ],
        out_specs=pl.BlockSpec((None, 8, tn), lambda l, j: (l, 0, j)),
        out_shape=jax.ShapeDtypeStruct((DEPTH, 8, N_MOD * D), F32),
        compiler_params=_cparams(("arbitrary", "arbitrary")),
        name="ada_mods",
    )(cond8, ada_w, ada_b.reshape(DEPTH, 1, N_MOD * D))


def _mixer_out(x, mod_ref, ctx, hf_ref, hb_ref, hgate_ref, gn_ref, ones_ref, ac_ref, al_ref, fc_ref, fl_ref, ga_ref,
               w_ref):
    gate = mod_ref[5:6, :]
    oh = hf_ref[...].astype(F32) + hb_ref[...].astype(F32)
    ms = _dot_exact(oh * oh, ones_ref[...])
    oh = oh * lax.rsqrt(ms + EPS) * gn_ref[...] * _silu(hgate_ref[...])
    oa = _rms_rows(jnp.where(ctx, ac_ref[...], al_ref[...]).astype(F32), ga_ref[...])
    of = jnp.where(ctx, fc_ref[...], fl_ref[...])
    o = (_dot(oh.astype(BF16), w_ref[0:W_HGRN, :])
         + _dot(oa.astype(BF16), w_ref[W_HGRN:W_HGRN + W_MLA, :])
         + _dot(of.astype(BF16), w_ref[W_HGRN + W_MLA:, :]))
    return x + gate * o


def _outproj_kernel(x_ref, mod_ref, *rest):
    o_ref = rest[-1]
    o_ref[...] = _mixer_out(x_ref[...], mod_ref, pl.program_id(0) < N_CTX_TILES, *rest[:-1])


def _outproj(x, mods, h_fwd, h_bwd, hg, gn_row, att_ctx, att_lat, fn_ctx, fn_lat, g_att, w_out, layer):
    lane = np.arange(W_HGRN)
    ones_bd = jnp.asarray((lane[:, None] // DV_HGRN == lane[None, :] // DV_HGRN).astype(np.float32) / DV_HGRN)
    return pl.pallas_call(
        _outproj_kernel,
        grid=(N_TILES,),
        in_specs=[_row_spec(D), _mod_spec(layer), _row_spec(W_HGRN), _row_spec(W_HGRN), _row_spec(W_HGRN, 4),
                  pl.BlockSpec((1, W_HGRN), lambda i: (0, 0)), pl.BlockSpec((W_HGRN, W_HGRN), lambda i: (0, 0)),
                  _ctx_row_spec(W_MLA), _lat_row_spec(W_MLA), _ctx_row_spec(W_FNET), _lat_row_spec(W_FNET),
                  pl.BlockSpec((None, 1, W_MLA), lambda i: (layer, 0, 0)),
                  _resident((None, D, D), lambda i: (layer, 0, 0))],
        out_specs=_row_spec(D),
        out_shape=jax.ShapeDtypeStruct((T_ALL, D), F32),
        compiler_params=_cparams(("arbitrary",)),
        name=f"outproj_l{layer}",
    )(x, mods, h_fwd, h_bwd, hg, gn_row, ones_bd, att_ctx, att_lat, fn_ctx, fn_lat,
      g_att.reshape(DEPTH, 1, W_MLA), w_out)


def _ffn_kernel(*refs, mi, two_sources):
    if two_sources:
        xc_ref, xl_ref, mod_ref, g_ref, wg_ref, wu_ref, wd_ref, o_ref = refs
        x = jnp.where(pl.program_id(0) < N_CTX_TILES, xc_ref[...], xl_ref[...])
    else:
        x_ref, mod_ref, g_ref, wg_ref, wu_ref, wd_ref, o_ref = refs
        x = x_ref[...]
    shift = mod_ref[mi:mi + 1, :]
    scale = mod_ref[mi + 1:mi + 2, :]
    gate = mod_ref[mi + 2:mi + 3, :]
    hb = (_rms_rows(x, g_ref[...]) * (1.0 + scale) + shift).astype(BF16)
    acc = jnp.zeros(x.shape, F32)
    for j in range(D_FF // FF_CHUNK):
        cs = slice(j * FF_CHUNK, (j + 1) * FF_CHUNK)
        a = _silu(_dot(hb, wg_ref[:, cs])) * _dot(hb, wu_ref[:, cs])
        acc = acc + _dot(a.astype(BF16), wd_ref[cs, :])
    o_ref[...] = x + 0.5 * gate * acc


def _ffn(xs, mods, norm_g, w_gu, w_down, layer, which, *, src_tile0=0, n_tiles=N_TILES, out_rows=T_ALL):
    mi = 0 if which == 0 else 6
    gi = 0 if which == 0 else 2

    def tile(i):
        return i + src_tile0
    two = isinstance(xs, tuple)
    if two:
        in_specs = [_ctx_row_spec(D), _lat_row_spec(D)]
        args = list(xs)
    else:
        in_specs = [_row_spec(D, tile=tile)]
        args = [xs]
    in_specs += [
        _mod_spec(layer, tile),
        pl.BlockSpec((None, None, 1, D), lambda i: (layer, gi, 0, 0)),
        _resident((None, None, D, D_FF), lambda i: (layer, which, 0, 0)),
        _resident((None, None, D, D_FF), lambda i: (layer, which, 0, 1)),
        _resident((None, None, D_FF, D), lambda i: (layer, which, 0, 0)),
    ]
    args += [mods, norm_g.reshape(DEPTH, 3, 1, D), w_gu, w_gu, w_down]
    return pl.pallas_call(
        functools.partial(_ffn_kernel, mi=mi, two_sources=two),
        grid=(n_tiles,),
        in_specs=in_specs,
        out_specs=pl.BlockSpec((TM, D), lambda i: (i, 0)),
        out_shape=jax.ShapeDtypeStruct((out_rows, D), F32),
        compiler_params=_cparams(("arbitrary",)),
        name=f"ffn_l{layer}_{which}",
    )(*args)


def _mla_heads(qa, ka, va_t, kr, kr_sw, tabs, q_ref, k_ref, vt_ref):
    cq, sq, ck, sk = tabs
    row = lax.broadcasted_iota(jnp.int32, (W_HEADS, 1), 0) % HEAD_PAD
    vt_ref[...] = jnp.where(row == D_V, 1.0, va_t).astype(BF16)
    k_rot = kr_sw * sk
    heads = [slice(h * HEAD_PAD, (h + 1) * HEAD_PAD) for h in range(H_MLA)]

    def inv_rms(xs):
        return [lax.rsqrt(jnp.sum(x * x, axis=-1, keepdims=True) * (1.0 / D_QK) + EPS) for x in xs]
    if qa is not None:
        q = [qa[:, hs] for hs in heads]
        q_sw = [qa[:, W_HEADS + hs.start:W_HEADS + hs.stop] for hs in heads]
        for hs, x, x_sw, rs in zip(heads, q, q_sw, inv_rms(q)):
            q_ref[:, hs] = (rs * (x * cq + x_sw * sq)).astype(BF16)
    k = [ka[:, hs] + kr for hs in heads]
    for hs, x, rs in zip(heads, k, inv_rms(k)):
        k_ref[:, hs] = (rs * (x * ck + k_rot)).astype(BF16)


def _inproj_kernel(x_ref, mod_ref, g_ref, w_ref, gq_ref, gkv_ref, wq_ref, wk_ref, wv_ref, cq_ref, sq_ref, ck_ref,
                   sk_ref, hg_ref, ckv_ref, kr_ref, uf_ref, q_ref, k_ref, vt_ref):
    x = x_ref[...]
    shift = mod_ref[3:4, :]
    scale = mod_ref[4:5, :]
    hb = (_rms_rows(x, g_ref[...]) * (1.0 + scale) + shift).astype(BF16)
    u = _dot(hb, w_ref[...])
    o = 0
    hg_ref[...] = u[:, o:o + IN_HG]
    o += IN_HG
    cqn = _rms_rows(u[:, o:o + Q_RANK], gq_ref[...])
    o += Q_RANK
    ckvn = _rms_rows(u[:, o:o + KV_RANK], gkv_ref[...])
    ckv_ref[...] = ckvn
    o += KV_RANK
    kr = u[:, o:o + HEAD_PAD]
    kr_ref[...] = kr
    o += HEAD_PAD
    kr_sw = u[:, o:o + HEAD_PAD]
    o += HEAD_PAD
    uf_ref[...] = u[:, o:o + W_FNET]
    cb = ckvn.astype(BF16)
    _mla_heads(_dot(cqn.astype(BF16), wq_ref[...]), _dot(cb, wk_ref[...]), _dot_nt(wv_ref[...], cb), kr, kr_sw,
               (cq_ref[...], sq_ref[...], ck_ref[...], sk_ref[...]), q_ref, k_ref, vt_ref)


def _rope_block(i):
    return jnp.where(i < N_CTX_TILES, 0, 1 + (i - N_CTX_TILES) % LAT_TILES_PER_SEQ)


def _inproj(x, mods, norm_g, w_in_arr, gq, gkv, wq2, wk_arr, wv_t, tabs, layer):
    tab = pl.BlockSpec((None, None, TM, HEAD_PAD), lambda i: (layer, _rope_block(i), 0, 0))
    f32_widths = (IN_HG, KV_RANK, HEAD_PAD, W_FNET)
    return pl.pallas_call(
        _inproj_kernel,
        grid=(N_TILES,),
        in_specs=[
            _row_spec(D),
            _mod_spec(layer),
            pl.BlockSpec((None, None, 1, D), lambda i: (layer, 1, 0, 0)),
            _resident((None, D, IN_ARR), lambda i: (layer, 0, 0)),
            pl.BlockSpec((None, 1, Q_RANK), lambda i: (layer, 0, 0)),
            pl.BlockSpec((None, 1, KV_RANK), lambda i: (layer, 0, 0)),
            _resident((None, Q_RANK, 2 * W_HEADS), lambda i: (layer, 0, 0)),
            _resident((None, KV_RANK, W_HEADS), lambda i: (layer, 0, 0)),
            _resident((None, W_HEADS, KV_RANK), lambda i: (layer, 0, 0)),
            tab, tab, tab, tab,
        ],
        out_specs=[_row_spec(w) for w in f32_widths] + [_row_spec(W_HEADS)] * 2
        + [pl.BlockSpec((W_HEADS, TM), lambda i: (0, i))],
        out_shape=[jax.ShapeDtypeStruct((T_ALL, w), F32) for w in f32_widths]
        + [jax.ShapeDtypeStruct((T_ALL, W_HEADS), BF16)] * 2 + [jax.ShapeDtypeStruct((W_HEADS, T_ALL), BF16)],
        compiler_params=_cparams(("arbitrary",)),
        name=f"inproj_l{layer}",
    )(x, mods, norm_g.reshape(DEPTH, 3, 1, D), w_in_arr, gq.reshape(DEPTH, 1, Q_RANK),
      gkv.reshape(DEPTH, 1, KV_RANK), wq2, wk_arr, wv_t, *tabs)


def _kv_cache_kernel(ckv_ref, kr_ref, krsw_ref, wk_ref, wv_ref, ck_ref, sk_ref, k_ref, vt_ref):
    cb = ckv_ref[...].astype(BF16)
    _mla_heads(None, _dot(cb, wk_ref[...]), _dot_nt(wv_ref[...], cb), kr_ref[...], krsw_ref[...],
               (None, None, ck_ref[...], sk_ref[...]), None, k_ref, vt_ref)


def _kv_cache(cache_ckv_l, cache_kr, cache_kr_sw, wk_arr, wv_t, tabs, layer):
    assert PAST == TM
    blk = lambda w: pl.BlockSpec((None, PAST, w), lambda b: (b, 0, 0))
    tab = pl.BlockSpec((None, None, TM, HEAD_PAD), lambda b: (layer, 0, 0, 0))
    return pl.pallas_call(
        _kv_cache_kernel,
        grid=(N_LAT_SEQ,),
        in_specs=[blk(KV_RANK), blk(HEAD_PAD), blk(HEAD_PAD),
                  _resident((None, KV_RANK, W_HEADS), lambda b: (layer, 0, 0)),
                  _resident((None, W_HEADS, KV_RANK), lambda b: (layer, 0, 0)), tab, tab],
        out_specs=[blk(W_HEADS), pl.BlockSpec((None, W_HEADS, PAST), lambda b: (b, 0, 0))],
        out_shape=[jax.ShapeDtypeStruct((N_LAT_SEQ, PAST, W_HEADS), BF16),
                   jax.ShapeDtypeStruct((N_LAT_SEQ, W_HEADS, PAST), BF16)],
        compiler_params=_cparams(("arbitrary",)),
        name=f"mla_kv_cache_l{layer}",
    )(cache_ckv_l, cache_kr, cache_kr_sw, wk_arr, wv_t, tabs[2], tabs[3])


N_SEQ = N_CTX_SEQ + N_LAT_SEQ
BLK_PER_LAT = L_LAT // T_BLK
N_HGRN_STEPS = N_CTX_SEQ + N_LAT_SEQ * BLK_PER_LAT


def _hgrn_seq(i):
    return jnp.where(i < N_CTX_SEQ, i, N_CTX_SEQ + (i - N_CTX_SEQ) // BLK_PER_LAT)


def _hgrn_blk(i, reverse):
    j = (i - N_CTX_SEQ) % BLK_PER_LAT
    if reverse:
        j = BLK_PER_LAT - 1 - j
    lat = N_CTX_SEQ + ((i - N_CTX_SEQ) // BLK_PER_LAT) * BLK_PER_LAT + j
    return jnp.where(i < N_CTX_SEQ, i, lat)


def _hgrn_kernel(qf_ref, ff_ref, vf_ref, qb_ref, fb_ref, vb_ref, lb_ref, s0f_ref, s0b_ref,
                 tri_ref, tri4_ref, hm_ref, bd_ref, of_ref, ob_ref, sf_ref, sb_ref, stf_scr, stb_scr,
                 ks_scr, bs_scr, vs_scr, oi_scr):
    i = pl.program_id(0)
    first = jnp.logical_or(i < N_CTX_SEQ, (i - N_CTX_SEQ) % BLK_PER_LAT == 0)
    heads = [slice(h * DK_HGRN, (h + 1) * DK_HGRN) for h in range(H_HGRN)]

    @pl.when(first)
    def _():
        for s0_ref, scr in ((s0f_ref, stf_scr), (s0b_ref, stb_scr)):
            scr[...] = jnp.zeros(scr.shape, F32)
            for h, hs in enumerate(heads):
                scr[hs, hs] = s0_ref[h]

    lb = lb_ref[...]
    loglb = jnp.log(lb)
    log1mlb = jnp.log(1.0 - lb)
    n_chunks = T_BLK // CHUNK
    dirs = ((qf_ref, ff_ref, vf_ref, of_ref, stf_scr), (qb_ref, fb_ref, vb_ref, ob_ref, stb_scr))
    units = [(d, c if d == 0 else n_chunks - 1 - c) for c in range(n_chunks) for d in (0, 1)]
    rows = [slice(cc * CHUNK, (cc + 1) * CHUNK) for _, cc in units]
    end_row = (CHUNK - 1, 0)
    mid_row = (CHUNK // 2 - 1, CHUNK // 2)
    hm, bd = hm_ref[...], bd_ref[...]

    q = [_silu(dirs[d][0][r, :]) for (d, _), r in zip(units, rows)]
    v = [dirs[d][2][r, :] for (d, _), r in zip(units, rows)]
    g = []
    for (d, _), r in zip(units, rows):
        x = dirs[d][1][r, :]
        y = log1mlb[d:d + 1] + (jnp.minimum(x, 0.0) - jnp.log(1.0 + jnp.exp(-jnp.abs(x))))
        g.append(jnp.maximum(loglb[d:d + 1], y) + jnp.log(1.0 + jnp.exp(-jnp.abs(loglb[d:d + 1] - y))))
    kk = [1.0 - jnp.exp(gu) for gu in g]
    b = [_dot_exact(tri_ref[d], gu) for (d, _), gu in zip(units, g)]
    b_end = [bu[end_row[d]:end_row[d] + 1, :] for (d, _), bu in zip(units, b)]

    def finish(o_intra):
        q_in = [(qu * jnp.exp(bu)).astype(BF16) for qu, bu in zip(q, b)]
        k_e = [(ku * jnp.exp(be - bu)).astype(BF16) for ku, bu, be in zip(kk, b, b_end)]
        decay = [jnp.exp(be) for be in b_end]
        kv = [_dot_tn(vu.astype(BF16), ku) * bd for vu, ku in zip(v, k_e)]
        st = [stf_scr[...], stb_scr[...]]
        for u, (d, _) in enumerate(units):
            dirs[d][3][rows[u], :] = (o_intra[u] + _dot_nt(q_in[u], st[d].astype(BF16))).astype(BF16)
            st[d] = st[d] * decay[u] + kv[u]
        stf_scr[...] = st[0]
        stb_scr[...] = st[1]
        for h, hs in enumerate(heads):
            sf_ref[h] = st[0][hs, hs]
            sb_ref[h] = st[1][hs, hs]

    safe = jnp.max(jnp.concatenate([jnp.abs(be) for be in b_end], axis=0)) <= HGRN_SAFE_LOG_DECAY

    @pl.when(safe)
    def _():
        b_mid = [bu[mid_row[d]:mid_row[d] + 1, :] for (d, _), bu in zip(units, b)]
        q_t = [(qu * jnp.exp(bu - bm)).astype(BF16) for qu, bu, bm in zip(q, b, b_mid)]
        k_t = [ku * jnp.exp(bm - bu) for ku, bu, bm in zip(kk, b, b_mid)]
        k_bd = [(jnp.concatenate([ku] * H_HGRN, axis=0) * hm).astype(BF16) for ku in k_t]
        v_bd = [(jnp.concatenate([vu] * H_HGRN, axis=0) * hm).astype(BF16) for vu in v]
        sc = [(_dot_nt(qu, ku) * tri4_ref[d]).astype(BF16) for (d, _), qu, ku in zip(units, q_t, k_bd)]
        finish([_dot(su, vu) for su, vu in zip(sc, v_bd)])

    @pl.when(jnp.logical_not(safe))
    def _():
        t = lax.broadcasted_iota(jnp.int32, (CHUNK, 1), 0)
        head_sum = bd.astype(BF16)
        for u in range(len(units)):
            ks_scr[u] = kk[u]
            bs_scr[u] = b[u]
            vs_scr[u] = v[u]
            oi_scr[u] = jnp.zeros((CHUNK, W_HGRN), F32)

        def source_row(s, carry):
            for u, (d, _) in enumerate(units):
                seen = (t >= s) if d == 0 else (t <= s)
                e = jnp.exp(jnp.where(seen, b[u] - bs_scr[u, pl.ds(s, 1), :], -jnp.inf))
                p = (q[u] * ks_scr[u, pl.ds(s, 1), :] * e).astype(BF16)
                oi_scr[u] += _dot(p, head_sum) * vs_scr[u, pl.ds(s, 1), :]
            return carry
        lax.fori_loop(0, CHUNK, source_row, 0)
        finish([oi_scr[u] for u in range(len(units))])


def _hgrn_consts():
    t = np.arange(CHUNK)
    tri = np.stack([t[:, None] >= t[None, :], t[:, None] <= t[None, :]]).astype(np.float32)
    tri4 = np.tile(tri, (1, 1, H_HGRN))
    r = np.arange(H_HGRN * CHUNK)
    lane = np.arange(W_HGRN)
    hm = (r[:, None] // CHUNK == lane[None, :] // DK_HGRN).astype(np.float32)
    bd = (lane[:, None] // DV_HGRN == lane[None, :] // DK_HGRN).astype(np.float32)
    return jnp.asarray(tri), jnp.asarray(tri4), jnp.asarray(hm), jnp.asarray(bd)


def _hgrn(hg, lb2, s0f, s0b):
    tri, tri4, hm, bd = _hgrn_consts()

    def col(cb, reverse):
        return pl.BlockSpec((T_BLK, W_HGRN), lambda i: (_hgrn_blk(i, reverse), cb))

    def const(shape):
        return pl.BlockSpec(shape, lambda i: (0,) * len(shape))
    state = pl.BlockSpec((None, H_HGRN, DV_HGRN, DK_HGRN), lambda i: (_hgrn_seq(i), 0, 0, 0))
    return pl.pallas_call(
        _hgrn_kernel,
        grid=(N_HGRN_STEPS,),
        in_specs=[col(0, False), col(1, False), col(3, False), col(0, True), col(2, True), col(3, True),
                  const((2, W_HGRN)), state, state,
                  const(tri.shape), const(tri4.shape), const(hm.shape), const(bd.shape)],
        out_specs=[col(0, False), col(0, True), state, state],
        out_shape=[jax.ShapeDtypeStruct((T_ALL, W_HGRN), BF16)] * 2
        + [jax.ShapeDtypeStruct((N_SEQ, H_HGRN, DV_HGRN, DK_HGRN), F32)] * 2,
        scratch_shapes=[pltpu.VMEM((W_HGRN, W_HGRN), F32)] * 2
        + [pltpu.VMEM((2 * T_BLK // CHUNK, CHUNK, W_HGRN), F32)] * 4,
        compiler_params=_cparams(("arbitrary",)),
        name="hgrn",
    )(hg, hg, hg, hg, hg, hg, lb2, s0f, s0b, tri, tri4, hm, bd)


def _attn_kernel(q_ref, k_ref, vt_ref, o_ref, *, staged):
    lane = lax.broadcasted_iota(jnp.int32, (1, HEAD_PAD), 1)
    heads = [slice(h * HEAD_PAD, (h + 1) * HEAD_PAD) for h in range(q_ref.shape[1] // HEAD_PAD)]

    def scores(hs):
        return _dot_nt(q_ref[:, hs], k_ref[:, hs])

    def probs(sh):
        return jnp.exp2(sh - jnp.max(sh, axis=-1, keepdims=True)).astype(BF16)

    def values(ph, hs):
        return _dot_nt(vt_ref[hs, :], ph)
    if staged:
        s = [scores(hs) for hs in heads]
        p = [probs(sh) for sh in s]
        pv_t = [values(ph, hs) for ph, hs in zip(p, heads)]
    else:
        pv_t = []
        s_next = scores(heads[0])
        for i, hs in enumerate(heads):
            s_cur = s_next
            if i + 1 < len(heads):
                s_next = scores(heads[i + 1])
            pv_t.append(values(probs(s_cur), hs))
    o = [(x * (1.0 / x[D_V:D_V + 1, :])).T for x in pv_t]
    o = [jnp.where(lane < D_V, x, 0.0) for x in o]
    for hp in range(len(heads) // 2):
        o_ref[:, hp * 2 * D_V:(hp + 1) * 2 * D_V] = (o[2 * hp] + pltpu.roll(o[2 * hp + 1], D_V, axis=1)).astype(BF16)


def _attn_lat_kernel(q_ref, kp_ref, vp_ref, kn_ref, vn_ref, o_ref, k_scr, vt_scr):
    @pl.when(pl.program_id(2) == 0)
    def _():
        k_scr[0:PAST, :] = kp_ref[...]
        k_scr[PAST:, :] = kn_ref[...]
        vt_scr[:, 0:PAST] = vp_ref[...]
        vt_scr[:, PAST:] = vn_ref[...]
    _attn_kernel(q_ref, k_scr, vt_scr, o_ref, staged=True)


def _attention_ctx(q, k, v_t):
    blk = pl.BlockSpec((L_CTX, W_HEADS), lambda b: (b, 0))
    return pl.pallas_call(
        functools.partial(_attn_kernel, staged=True),
        grid=(N_CTX_SEQ,),
        in_specs=[blk, blk, pl.BlockSpec((W_HEADS, L_CTX), lambda b: (0, b))],
        out_specs=pl.BlockSpec((L_CTX, W_MLA), lambda b: (b, 0)),
        out_shape=jax.ShapeDtypeStruct((T_CTX, W_MLA), BF16),
        compiler_params=_cparams(("arbitrary",)),
        name="attn_ctx",
    )(q, k, v_t)


def _attention_lat(q, k, v_t, k_past, vt_past):
    grp = ATT_HEADS * HEAD_PAD
    nq = L_LAT // TQ
    q0 = T_CTX // TQ
    seq0 = T_CTX // L_LAT
    return pl.pallas_call(
        _attn_lat_kernel,
        grid=(N_LAT_SEQ, H_MLA // ATT_HEADS, nq),
        in_specs=[pl.BlockSpec((TQ, grp), lambda b, hp, qi: (q0 + b * nq + qi, hp)),
                  pl.BlockSpec((None, PAST, grp), lambda b, hp, qi: (b, 0, hp)),
                  pl.BlockSpec((None, grp, PAST), lambda b, hp, qi: (b, hp, 0)),
                  _resident((L_LAT, grp), lambda b, hp, qi: (seq0 + b, hp)),
                  _resident((grp, L_LAT), lambda b, hp, qi: (hp, seq0 + b))],
        out_specs=pl.BlockSpec((TQ, ATT_HEADS * D_V), lambda b, hp, qi: (b * nq + qi, hp)),
        out_shape=jax.ShapeDtypeStruct((T_LAT, W_MLA), BF16),
        scratch_shapes=[pltpu.VMEM((PAST + L_LAT, grp), BF16), pltpu.VMEM((grp, PAST + L_LAT), BF16)],
        compiler_params=_cparams(("arbitrary", "arbitrary", "arbitrary")),
        name="attn_lat",
    )(q, k_past, vt_past, k, v_t)


def _dft_tables(n):
    a = 2.0 * np.pi * np.outer(np.arange(n), np.arange(n)) / n
    return np.cos(a), np.sin(a)


def _bf16_operand(table):
    return jnp.asarray(table, F32).astype(BF16)


def _fnet_finish(re, im, cs_ref, w_ref, g_ref):
    spec = _dot(jnp.concatenate([re, im], axis=1).astype(BF16), cs_ref[...])
    return _rms_rows(_dot(spec.astype(BF16), w_ref[...]), g_ref[...])


def _fnet_channel_consts(w_bd, g_row):
    c, s = _dft_tables(C_FNET)
    eye = np.eye(G_FNET)
    return [_bf16_operand(np.concatenate([np.kron(eye, c), np.kron(eye, s)], axis=0)), w_bd, g_row]


def _fnet_channel_specs():
    zero = lambda *a: (0, 0)
    return [pl.BlockSpec((2 * W_FNET, W_FNET), zero), pl.BlockSpec((W_FNET, W_FNET), zero),
            pl.BlockSpec((1, W_FNET), zero)]


def _fnet_ctx_kernel(u_ref, m_ref, cs_ref, w_ref, g_ref, o_ref):
    p = _dot(m_ref[...], u_ref[...].astype(BF16))
    o_ref[...] = _fnet_finish(p[:L_CTX], p[L_CTX:], cs_ref, w_ref, g_ref)


def _fnet_ctx(uf, w_bd, g_row):
    c, s = _dft_tables(L_CTX)
    norm = 1.0 / np.sqrt(L_CTX * C_FNET)
    blk = pl.BlockSpec((L_CTX, W_FNET), lambda b: (b, 0))
    return pl.pallas_call(
        _fnet_ctx_kernel,
        grid=(N_CTX_SEQ,),
        in_specs=[blk, pl.BlockSpec((2 * L_CTX, L_CTX), lambda b: (0, 0))] + _fnet_channel_specs(),
        out_specs=blk,
        out_shape=jax.ShapeDtypeStruct((T_CTX, W_FNET), F32),
        compiler_params=_cparams(("arbitrary",)),
        name="fnet_ctx",
    )(uf, _bf16_operand(np.concatenate([c, -s], axis=0) * norm), *_fnet_channel_consts(w_bd, g_row))


FN_SUB = 8


def _fnet_rows_kernel(x_ref, m_ref, re_ref, im_ref):
    for s in range(FN_SUB):
        y = _dot(m_ref[s], x_ref[:, s, :].astype(BF16))
        re_ref[s] = y[:GRID_W]
        im_ref[s] = y[GRID_W:]


def _fnet_cols_kernel(re_ref, im_ref, m_ref, cs_ref, w_ref, g_ref, o_ref):
    z = [_dot(m_ref[...], jnp.concatenate([re_ref[:, s, :], im_ref[:, s, :]], axis=0).astype(BF16))
         for s in range(FN_SUB)]
    out = _fnet_finish(jnp.concatenate([zs[:GRID_W] for zs in z], axis=0),
                       jnp.concatenate([zs[GRID_W:] for zs in z], axis=0), cs_ref, w_ref, g_ref)
    for s in range(FN_SUB):
        o_ref[:, s, :] = out[s * GRID_W:(s + 1) * GRID_W]


def _fnet_lat(uf, w_bd, g_row):
    p1 = np.arange(GRID_W)[None, :, None]
    l = GRID_W * np.arange(GRID_W)[None, None, :] + np.arange(GRID_W)[:, None, None]
    ang = 2.0 * np.pi * p1 * l / L_LAT
    norm = 1.0 / np.sqrt(L_LAT * C_FNET)
    m_rows = _bf16_operand(np.concatenate([np.cos(ang), -np.sin(ang)], axis=1) * norm)
    c64, s64 = _dft_tables(GRID_W)
    m_cols = _bf16_operand(np.block([[c64, s64], [-s64, c64]]))
    x4 = uf.reshape(T_ALL // L_LAT, GRID_W, GRID_W, W_FNET)
    seq0 = T_CTX // L_LAT
    grid = (N_LAT_SEQ, GRID_W // FN_SUB)
    mid = pl.BlockSpec((None, FN_SUB, GRID_W, W_FNET), lambda b, j: (b, j, 0, 0))
    mid_shape = jax.ShapeDtypeStruct((N_LAT_SEQ, GRID_W, GRID_W, W_FNET), F32)
    re, im = pl.pallas_call(
        _fnet_rows_kernel,
        grid=grid,
        in_specs=[pl.BlockSpec((None, GRID_W, FN_SUB, W_FNET), lambda b, j: (seq0 + b, 0, j, 0)),
                  pl.BlockSpec((FN_SUB, 2 * GRID_W, GRID_W), lambda b, j: (j, 0, 0))],
        out_specs=[mid, mid],
        out_shape=[mid_shape, mid_shape],
        compiler_params=_cparams(("arbitrary", "arbitrary")),
        name="fnet_rows_lat",
    )(x4, m_rows)
    strided = pl.BlockSpec((None, GRID_W, FN_SUB, W_FNET), lambda b, j: (b, 0, j, 0))
    out = pl.pallas_call(
        _fnet_cols_kernel,
        grid=grid,
        in_specs=[strided, strided, pl.BlockSpec((2 * GRID_W, 2 * GRID_W), lambda b, j: (0, 0))]
        + _fnet_channel_specs(),
        out_specs=strided,
        out_shape=mid_shape,
        compiler_params=_cparams(("arbitrary", "arbitrary")),
        name="fnet_cols_lat",
    )(re, im, m_cols, *_fnet_channel_consts(w_bd, g_row))
    return out.reshape(T_LAT, W_FNET)


def _pad_heads(w, d_head):
    lead = w.shape[:-1]
    w = w.reshape(lead + (H_MLA, d_head))
    w = jnp.pad(w, [(0, 0)] * len(lead) + [(0, 0), (0, HEAD_PAD - d_head)])
    return w.reshape(lead + (W_HEADS,))


def _swap_rope(a):
    n = D_ROPE // 4
    parts = [jnp.zeros(a.shape[:-1] + (D_NOPE,), a.dtype)]
    for ax in range(2):
        base = D_NOPE + ax * 2 * n
        parts += [a[..., base + n:base + 2 * n], a[..., base:base + n]]
    parts.append(jnp.zeros(a.shape[:-1] + (HEAD_PAD - D_QK,), a.dtype))
    return jnp.concatenate(parts, axis=-1)


def _rope_tables(g_q, g_k):
    n_freq = D_ROPE // 4
    t = np.arange(L_LAT)
    pos = np.stack([t // GRID_W, t % GRID_W], axis=-1).astype(np.float32)
    freq = (np.float32(ROPE_THETA) ** (-np.arange(n_freq, dtype=np.float32) / n_freq)).astype(np.float32)
    ang = (pos[:, :, None] * freq).astype(np.float32)
    cos = np.ones((L_LAT, HEAD_PAD), np.float32)
    sin = np.zeros((L_LAT, HEAD_PAD), np.float32)
    for ax in range(2):
        base = D_NOPE + ax * 2 * n_freq
        c, s = np.cos(ang[:, ax, :]), np.sin(ang[:, ax, :])
        cos[:, base:base + n_freq] = c
        cos[:, base + n_freq:base + 2 * n_freq] = c
        sin[:, base:base + n_freq] = -s
        sin[:, base + n_freq:base + 2 * n_freq] = s
    cos = np.concatenate([np.ones((TM, HEAD_PAD), np.float32), cos]).reshape(1, 1 + LAT_TILES_PER_SEQ, TM, HEAD_PAD)
    sin = np.concatenate([np.zeros((TM, HEAD_PAD), np.float32), sin]).reshape(1, 1 + LAT_TILES_PER_SEQ, TM, HEAD_PAD)

    def pair(g, scale):
        g128 = jnp.pad(g, ((0, 0), (0, HEAD_PAD - D_QK))) * scale
        g_partner = _swap_rope(g128)
        return (cos * g128[:, None, None, :], sin * g_partner[:, None, None, :])
    return pair(g_q, D_QK ** -0.5 * LOG2_E) + pair(g_k, 1.0)


def kernel(x_prompt, x_sample, cache_ckv, cache_krope, state_hgrn, c, c_ctx, ada_w, ada_b, norm_g, ffn_w_gu, ffn_w_down, w_in, hgrn_lb, hgrn_norm_g, mla_q_norm_g, mla_w_q_up, mla_kv_norm_g, mla_w_kv_up, mla_qk_norm_g, mla_out_norm_g, fnet_w, fnet_norm_g, w_out):
    w_gu = ffn_w_gu.astype(BF16)
    w_down = ffn_w_down.astype(BF16)
    o = np.cumsum((0, 5 * W_HGRN, Q_RANK, KV_RANK, D_ROPE, W_FNET))
    w_kr = jnp.pad(w_in[:, :, o[3]:o[4]], ((0, 0), (0, 0), (D_NOPE, HEAD_PAD - D_NOPE - D_ROPE)))
    w_in_arr = jnp.concatenate([w_in[:, :, :o[3]], w_kr, _swap_rope(w_kr), w_in[:, :, o[4]:]], axis=-1).astype(BF16)
    wq = _pad_heads(mla_w_q_up, D_QK)
    wq_sw = _swap_rope(wq.reshape(DEPTH, Q_RANK, H_MLA, HEAD_PAD)).reshape(DEPTH, Q_RANK, W_HEADS)
    wq2 = jnp.concatenate([wq, wq_sw], axis=-1).astype(BF16)
    w_kv = mla_w_kv_up.reshape(DEPTH, KV_RANK, H_MLA, D_NOPE + D_V)
    wk_arr = _pad_heads(w_kv[..., :D_NOPE].reshape(DEPTH, KV_RANK, H_MLA * D_NOPE), D_NOPE).astype(BF16)
    wv_t = jnp.swapaxes(_pad_heads(w_kv[..., D_NOPE:].reshape(DEPTH, KV_RANK, H_MLA * D_V), D_V), 1, 2).astype(BF16)
    w_out_b = w_out.astype(BF16)
    eye_g = jnp.eye(G_FNET, dtype=F32)
    fnet_bd = jnp.einsum("lgcd,gh->lgchd", fnet_w, eye_g).reshape(DEPTH, W_FNET, W_FNET).astype(BF16)
    tabs = _rope_tables(mla_qk_norm_g[:, 0], mla_qk_norm_g[:, 1])
    lbs = jnp.cumsum(jax.nn.softmax(hgrn_lb.astype(F32), axis=0), axis=0)
    lbs = lbs - lbs[:1]

    cond8 = jnp.zeros((8, D), F32).at[0].set(c_ctx).at[1:1 + N_LAT_SEQ].set(c)
    mods = _mods(cond8, ada_w, ada_b).reshape(DEPTH, 8, N_MOD, D)

    x = (x_prompt.reshape(T_CTX, D), x_sample.reshape(T_LAT, D))
    ckv_out, kr_out, st_out = [], [], []
    zero_states = jnp.zeros((N_CTX_SEQ, H_HGRN, DV_HGRN, DK_HGRN), F32)
    for l in range(DEPTH):
        x = _ffn(x, mods, norm_g, w_gu, w_down, l, 0)
        hg, ckvn, kr128, uf, q, k, v_t = _inproj(x, mods, norm_g, w_in_arr, mla_q_norm_g, mla_kv_norm_g, wq2, wk_arr,
                                                 wv_t, tabs, l)

        s0 = [jnp.concatenate([zero_states, jnp.swapaxes(state_hgrn[:, l, d], -1, -2)], axis=0) for d in range(2)]
        h_fwd, h_bwd, s_f, s_b = _hgrn(hg, lbs[l], s0[0], s0[1])
        gn_row = jnp.tile(hgrn_norm_g[l], H_HGRN).reshape(1, W_HGRN)
        st_out.append(jnp.swapaxes(jnp.stack([s_f[:N_CTX_SEQ], s_b[:N_CTX_SEQ]], axis=1), -1, -2))

        cache_kr = jnp.pad(cache_krope[:, l], ((0, 0), (0, 0), (D_NOPE, HEAD_PAD - D_NOPE - D_ROPE)))
        k_past, vt_past = _kv_cache(cache_ckv[:, l], cache_kr, _swap_rope(cache_kr), wk_arr, wv_t, tabs, l)
        att_ctx = _attention_ctx(q, k, v_t)
        att_lat = _attention_lat(q, k, v_t, k_past, vt_past)

        fn_gain = fnet_norm_g[l].reshape(1, W_FNET)
        fn_ctx = _fnet_ctx(uf, fnet_bd[l], fn_gain)
        fn_lat = _fnet_lat(uf, fnet_bd[l], fn_gain)

        x = _outproj(x, mods, h_fwd, h_bwd, hg, gn_row, att_ctx, att_lat, fn_ctx, fn_lat, mla_out_norm_g, w_out_b, l)
        if l == DEPTH - 1:
            y_prompt = _ffn(x, mods, norm_g, w_gu, w_down, l, 1, n_tiles=N_CTX_TILES, out_rows=T_CTX)
            y_sample = _ffn(x, mods, norm_g, w_gu, w_down, l, 1, src_tile0=N_CTX_TILES, n_tiles=N_TILES - N_CTX_TILES,
                            out_rows=T_LAT)
        else:
            x = _ffn(x, mods, norm_g, w_gu, w_down, l, 1)

        ckv_out.append(ckvn[:T_CTX].reshape(N_CTX_SEQ, L_CTX, KV_RANK))
        kr_out.append(kr128[:T_CTX, D_NOPE:D_NOPE + D_ROPE].reshape(N_CTX_SEQ, L_CTX, D_ROPE))

    return (y_prompt.reshape(N_CTX_SEQ, L_CTX, D), y_sample.reshape(N_LAT_SEQ, L_LAT, D),
            jnp.stack(ckv_out, axis=1), jnp.stack(kr_out, axis=1), jnp.stack(st_out, axis=1))
```

```python
import functools

import numpy as np
import jax
import jax.numpy as jnp
from jax import lax
from jax.experimental import pallas as pl
from jax.experimental.pallas import tpu as pltpu

F32 = jnp.float32
BF16 = jnp.bfloat16
HIGHEST = lax.Precision.HIGHEST

D = 1024
N_CTX_SEQ, L_CTX = 32, 256
N_LAT_SEQ, L_LAT = 2, 4096
T_CTX = N_CTX_SEQ * L_CTX
T_LAT = N_LAT_SEQ * L_LAT
T_ALL = T_CTX + T_LAT
DEPTH = 2
PAST = 512
GRID_W = 64
N_MOD = 9
EPS = 1e-6

H_HGRN, DK_HGRN, DV_HGRN = 4, 64, 64
W_HGRN = H_HGRN * DV_HGRN
CHUNK = 32
HGRN_SAFE_LOG_DECAY = 60.0
H_MLA, Q_RANK, KV_RANK = 8, 384, 256
D_NOPE, D_ROPE, D_V = 64, 32, 64
D_QK = D_NOPE + D_ROPE
HEAD_PAD = 128
W_HEADS = H_MLA * HEAD_PAD
W_MLA = H_MLA * D_V
G_FNET, C_FNET = 4, 64
W_FNET = G_FNET * C_FNET
D_FF = 2816
FF_CHUNK = 256
ROPE_THETA = 10000.0
LOG2_E = 1.4426950408889634

IN_HG = 5 * W_HGRN
IN_ARR = IN_HG + Q_RANK + KV_RANK + 2 * HEAD_PAD + W_FNET

TM = 512
FFN_TM = 512
N_TILES = T_ALL // TM
N_CTX_TILES = T_CTX // TM
LAT_TILES_PER_SEQ = L_LAT // TM
T_BLK = 256
TQ = 256
ATT_HEADS = 2

VMEM_LIMIT = 56 * 1024 * 1024


def _cparams(sem):
    return pltpu.CompilerParams(dimension_semantics=sem, vmem_limit_bytes=VMEM_LIMIT)


def _tile_group(i, tm=TM):
    return (i >= T_CTX // tm).astype(jnp.int32) + (i >= (T_CTX + L_LAT) // tm).astype(jnp.int32)


def _silu(x):
    return x * (1.0 / (1.0 + jnp.exp(-x)))


def _rms_rows(x, g):
    return x * lax.rsqrt(jnp.mean(x * x, axis=-1, keepdims=True) + EPS) * g


def _dot(a, b):
    return jnp.dot(a, b, preferred_element_type=F32)


def _dot_nt(a, b):
    return lax.dot_general(a, b, (((1,), (1,)), ((), ())), preferred_element_type=F32)


def _dot_tn(a, b):
    return lax.dot_general(a, b, (((0,), (0,)), ((), ())), preferred_element_type=F32)


def _dot_exact(a, b):
    return jnp.dot(a, b, preferred_element_type=F32, precision=HIGHEST)


def _step_tile(i):
    return i


def _row_spec(width, col_block=0, tile=_step_tile, tm=TM):
    return pl.BlockSpec((tm, width), lambda i: (tile(i), col_block))


def _ctx_row_spec(width, tile=_step_tile, tm=TM):
    return pl.BlockSpec((tm, width), lambda i: (jnp.minimum(tile(i), T_CTX // tm - 1), 0))


def _lat_row_spec(width, tile=_step_tile, tm=TM):
    return pl.BlockSpec((tm, width), lambda i: (jnp.maximum(tile(i) - T_CTX // tm, 0), 0))


def _mod_spec(layer, tile=_step_tile, tm=TM):
    return pl.BlockSpec((None, None, N_MOD, D), lambda i: (layer, _tile_group(tile(i), tm), 0, 0))


def _resident(shape, index_map):
    return pl.BlockSpec(shape, index_map, pipeline_mode=pl.Buffered(1))


def _mods_kernel(c_ref, w_ref, b_ref, o_ref):
    a = _silu(c_ref[...]).astype(BF16)
    o_ref[...] = _dot(a, w_ref[...].astype(BF16)) + b_ref[...]


def _mods(cond8, ada_w, ada_b):
    tn = 1024
    return pl.pallas_call(
        _mods_kernel,
        grid=(DEPTH, N_MOD * D // tn),
        in_specs=[
            pl.BlockSpec((8, D), lambda l, j: (0, 0)),
            pl.BlockSpec((None, D, tn), lambda l, j: (l, 0, j)),
            pl.BlockSpec((None, 1, tn), lambda l, j: (l, 0, j)),
        ],
        out_specs=pl.BlockSpec((None, 8, tn), lambda l, j: (l, 0, j)),
        out_shape=jax.ShapeDtypeStruct((DEPTH, 8, N_MOD * D), F32),
        compiler_params=_cparams(("arbitrary", "arbitrary")),
        name="ada_mods",
    )(cond8, ada_w, ada_b.reshape(DEPTH, 1, N_MOD * D))


def _mixer_out(x, mod_ref, ctx, hf_ref, hb_ref, hgate_ref, gn_ref, ones_ref, ac_ref, al_ref, fc_ref, fl_ref, ga_ref,
               w_ref):
    gate = mod_ref[5:6, :]
    oh = hf_ref[...].astype(F32) + hb_ref[...].astype(F32)
    ms = _dot_exact(oh * oh, ones_ref[...])
    oh = oh * lax.rsqrt(ms + EPS) * gn_ref[...] * _silu(hgate_ref[...])
    oa = _rms_rows(jnp.where(ctx, ac_ref[...], al_ref[...]).astype(F32), ga_ref[...])
    of = jnp.where(ctx, fc_ref[...], fl_ref[...])
    o = (_dot(oh.astype(BF16), w_ref[0:W_HGRN, :])
         + _dot(oa.astype(BF16), w_ref[W_HGRN:W_HGRN + W_MLA, :])
         + _dot(of.astype(BF16), w_ref[W_HGRN + W_MLA:, :]))
    return x + gate * o


def _outproj_kernel(x_ref, mod_ref, *rest):
    o_ref = rest[-1]
    o_ref[...] = _mixer_out(x_ref[...], mod_ref, pl.program_id(0) < N_CTX_TILES, *rest[:-1])


def _outproj(x, mods, h_fwd, h_bwd, hg, gn_row, att_ctx, att_lat, fn_ctx, fn_lat, g_att, w_out, layer):
    lane = np.arange(W_HGRN)
    ones_bd = jnp.asarray((lane[:, None] // DV_HGRN == lane[None, :] // DV_HGRN).astype(np.float32) / DV_HGRN)
    return pl.pallas_call(
        _outproj_kernel,
        grid=(N_TILES,),
        in_specs=[_row_spec(D), _mod_spec(layer), _row_spec(W_HGRN), _row_spec(W_HGRN), _row_spec(W_HGRN, 4),
                  pl.BlockSpec((1, W_HGRN), lambda i: (0, 0)), pl.BlockSpec((W_HGRN, W_HGRN), lambda i: (0, 0)),
                  _ctx_row_spec(W_MLA), _lat_row_spec(W_MLA), _ctx_row_spec(W_FNET), _lat_row_spec(W_FNET),
                  pl.BlockSpec((None, 1, W_MLA), lambda i: (layer, 0, 0)),
                  _resident((None, D, D), lambda i: (layer, 0, 0))],
        out_specs=_row_spec(D),
        out_shape=jax.ShapeDtypeStruct((T_ALL, D), F32),
        compiler_params=_cparams(("arbitrary",)),
        name=f"outproj_l{layer}",
    )(x, mods, h_fwd, h_bwd, hg, gn_row, ones_bd, att_ctx, att_lat, fn_ctx, fn_lat,
      g_att.reshape(DEPTH, 1, W_MLA), w_out)


def _ffn_kernel(*refs, mi, two_sources):
    if two_sources:
        xc_ref, xl_ref, mod_ref, g_ref, wg_ref, wu_ref, wd_ref, o_ref = refs
        x = jnp.where(pl.program_id(0) < T_CTX // FFN_TM, xc_ref[...], xl_ref[...])
    else:
        x_ref, mod_ref, g_ref, wg_ref, wu_ref, wd_ref, o_ref = refs
        x = x_ref[...]
    shift = mod_ref[mi:mi + 1, :]
    scale = mod_ref[mi + 1:mi + 2, :]
    gate = mod_ref[mi + 2:mi + 3, :]
    hb = (_rms_rows(x, g_ref[...]) * (1.0 + scale) + shift).astype(BF16)
    acc = jnp.zeros(x.shape, F32)
    for j in range(D_FF // FF_CHUNK):
        cs = slice(j * FF_CHUNK, (j + 1) * FF_CHUNK)
        a = _silu(_dot(hb, wg_ref[:, cs])) * _dot(hb, wu_ref[:, cs])
        acc = acc + _dot(a.astype(BF16), wd_ref[cs, :])
    o_ref[...] = x + 0.5 * gate * acc


def _ffn(xs, mods, norm_g, w_gu, w_down, layer, which, *, src_row0=0, n_rows=T_ALL):
    mi = 0 if which == 0 else 6
    gi = 0 if which == 0 else 2
    tm = FFN_TM

    def tile(i):
        return i + src_row0 // tm
    two = isinstance(xs, tuple)
    if two:
        in_specs = [_ctx_row_spec(D, tm=tm), _lat_row_spec(D, tm=tm)]
        args = list(xs)
    else:
        in_specs = [_row_spec(D, tile=tile, tm=tm)]
        args = [xs]
    in_specs += [
        _mod_spec(layer, tile, tm),
        pl.BlockSpec((None, None, 1, D), lambda i: (layer, gi, 0, 0)),
        _resident((None, None, D, D_FF), lambda i: (layer, which, 0, 0)),
        _resident((None, None, D, D_FF), lambda i: (layer, which, 0, 1)),
        _resident((None, None, D_FF, D), lambda i: (layer, which, 0, 0)),
    ]
    args += [mods, norm_g.reshape(DEPTH, 3, 1, D), w_gu, w_gu, w_down]
    return pl.pallas_call(
        functools.partial(_ffn_kernel, mi=mi, two_sources=two),
        grid=(n_rows // tm,),
        in_specs=in_specs,
        out_specs=pl.BlockSpec((tm, D), lambda i: (i, 0)),
        out_shape=jax.ShapeDtypeStruct((n_rows, D), F32),
        compiler_params=_cparams(("arbitrary",)),
        name=f"ffn_l{layer}_{which}",
    )(*args)


def _mla_heads(qa, ka, va, kr, kr_sw, tabs, q_ref, k_ref, v_ref):
    cq, sq, ck, sk = tabs
    lane = lax.broadcasted_iota(jnp.int32, (1, W_HEADS), 1) % HEAD_PAD
    v_ref[...] = jnp.where(lane == D_V, 1.0, va).astype(BF16)
    k_rot = kr_sw * sk
    heads = [slice(h * HEAD_PAD, (h + 1) * HEAD_PAD) for h in range(H_MLA)]

    def inv_rms(xs):
        return [lax.rsqrt(jnp.sum(x * x, axis=-1, keepdims=True) * (1.0 / D_QK) + EPS) for x in xs]
    if qa is not None:
        q = [qa[:, hs] for hs in heads]
        q_sw = [qa[:, W_HEADS + hs.start:W_HEADS + hs.stop] for hs in heads]
        for hs, x, x_sw, rs in zip(heads, q, q_sw, inv_rms(q)):
            q_ref[:, hs] = (rs * (x * cq + x_sw * sq)).astype(BF16)
    k = [ka[:, hs] + kr for hs in heads]
    for hs, x, rs in zip(heads, k, inv_rms(k)):
        k_ref[:, hs] = (rs * (x * ck + k_rot)).astype(BF16)


def _inproj_kernel(x_ref, mod_ref, g_ref, w_ref, gq_ref, gkv_ref, wq_ref, wk_ref, wv_ref, cq_ref, sq_ref, ck_ref,
                   sk_ref, hg_ref, ckv_ref, kr_ref, uf_ref, q_ref, k_ref, v_ref):
    x = x_ref[...]
    shift = mod_ref[3:4, :]
    scale = mod_ref[4:5, :]
    hb = (_rms_rows(x, g_ref[...]) * (1.0 + scale) + shift).astype(BF16)
    u = _dot(hb, w_ref[...])
    o = 0
    hg_ref[...] = u[:, o:o + IN_HG]
    o += IN_HG
    cqn = _rms_rows(u[:, o:o + Q_RANK], gq_ref[...])
    o += Q_RANK
    ckvn = _rms_rows(u[:, o:o + KV_RANK], gkv_ref[...])
    ckv_ref[...] = ckvn
    o += KV_RANK
    kr = u[:, o:o + HEAD_PAD]
    kr_ref[...] = kr
    o += HEAD_PAD
    kr_sw = u[:, o:o + HEAD_PAD]
    o += HEAD_PAD
    uf_ref[...] = u[:, o:o + W_FNET]
    cb = ckvn.astype(BF16)
    _mla_heads(_dot(cqn.astype(BF16), wq_ref[...]), _dot(cb, wk_ref[...]), _dot(cb, wv_ref[...]), kr, kr_sw,
               (cq_ref[...], sq_ref[...], ck_ref[...], sk_ref[...]), q_ref, k_ref, v_ref)


def _rope_block(i):
    return jnp.where(i < N_CTX_TILES, 0, 1 + (i - N_CTX_TILES) % LAT_TILES_PER_SEQ)


def _inproj(x, mods, norm_g, w_in_arr, gq, gkv, wq2, wk_arr, wv_arr, tabs, layer):
    tab = pl.BlockSpec((None, None, TM, HEAD_PAD), lambda i: (layer, _rope_block(i), 0, 0))
    f32_widths = (IN_HG, KV_RANK, HEAD_PAD, W_FNET)
    return pl.pallas_call(
        _inproj_kernel,
        grid=(N_TILES,),
        in_specs=[
            _row_spec(D),
            _mod_spec(layer),
            pl.BlockSpec((None, None, 1, D), lambda i: (layer, 1, 0, 0)),
            _resident((None, D, IN_ARR), lambda i: (layer, 0, 0)),
            pl.BlockSpec((None, 1, Q_RANK), lambda i: (layer, 0, 0)),
            pl.BlockSpec((None, 1, KV_RANK), lambda i: (layer, 0, 0)),
            _resident((None, Q_RANK, 2 * W_HEADS), lambda i: (layer, 0, 0)),
            _resident((None, KV_RANK, W_HEADS), lambda i: (layer, 0, 0)),
            _resident((None, KV_RANK, W_HEADS), lambda i: (layer, 0, 0)),
            tab, tab, tab, tab,
        ],
        out_specs=[_row_spec(w) for w in f32_widths] + [_row_spec(W_HEADS)] * 3,
        out_shape=[jax.ShapeDtypeStruct((T_ALL, w), F32) for w in f32_widths]
        + [jax.ShapeDtypeStruct((T_ALL, W_HEADS), BF16)] * 3,
        compiler_params=_cparams(("arbitrary",)),
        name=f"inproj_l{layer}",
    )(x, mods, norm_g.reshape(DEPTH, 3, 1, D), w_in_arr, gq.reshape(DEPTH, 1, Q_RANK),
      gkv.reshape(DEPTH, 1, KV_RANK), wq2, wk_arr, wv_arr, *tabs)


def _kv_cache_kernel(ckv_ref, kr_ref, krsw_ref, wk_ref, wv_ref, ck_ref, sk_ref, k_ref, v_ref):
    cb = ckv_ref[...].astype(BF16)
    _mla_heads(None, _dot(cb, wk_ref[...]), _dot(cb, wv_ref[...]), kr_ref[...], krsw_ref[...],
               (None, None, ck_ref[...], sk_ref[...]), None, k_ref, v_ref)


def _kv_cache(cache_ckv_l, cache_kr, cache_kr_sw, wk_arr, wv_arr, tabs, layer):
    assert PAST == TM
    blk = lambda w: pl.BlockSpec((None, PAST, w), lambda b: (b, 0, 0))
    tab = pl.BlockSpec((None, None, TM, HEAD_PAD), lambda b: (layer, 0, 0, 0))
    return pl.pallas_call(
        _kv_cache_kernel,
        grid=(N_LAT_SEQ,),
        in_specs=[blk(KV_RANK), blk(HEAD_PAD), blk(HEAD_PAD),
                  _resident((None, KV_RANK, W_HEADS), lambda b: (layer, 0, 0)),
                  _resident((None, KV_RANK, W_HEADS), lambda b: (layer, 0, 0)), tab, tab],
        out_specs=[blk(W_HEADS)] * 2,
        out_shape=[jax.ShapeDtypeStruct((N_LAT_SEQ, PAST, W_HEADS), BF16)] * 2,
        compiler_params=_cparams(("arbitrary",)),
        name=f"mla_kv_cache_l{layer}",
    )(cache_ckv_l, cache_kr, cache_kr_sw, wk_arr, wv_arr, tabs[2], tabs[3])


N_SEQ = N_CTX_SEQ + N_LAT_SEQ
BLK_PER_LAT = L_LAT // T_BLK
N_HGRN_STEPS = N_CTX_SEQ + N_LAT_SEQ * BLK_PER_LAT


def _hgrn_seq(i):
    return jnp.where(i < N_CTX_SEQ, i, N_CTX_SEQ + (i - N_CTX_SEQ) // BLK_PER_LAT)


def _hgrn_blk(i, reverse):
    j = (i - N_CTX_SEQ) % BLK_PER_LAT
    if reverse:
        j = BLK_PER_LAT - 1 - j
    lat = N_CTX_SEQ + ((i - N_CTX_SEQ) // BLK_PER_LAT) * BLK_PER_LAT + j
    return jnp.where(i < N_CTX_SEQ, i, lat)


def _hgrn_kernel(qf_ref, ff_ref, vf_ref, qb_ref, fb_ref, vb_ref, lb_ref, s0f_ref, s0b_ref,
                 tri_ref, tri4_ref, hm_ref, bd_ref, of_ref, ob_ref, sf_ref, sb_ref, stf_scr, stb_scr,
                 ks_scr, bs_scr, vs_scr, oi_scr):
    i = pl.program_id(0)
    first = jnp.logical_or(i < N_CTX_SEQ, (i - N_CTX_SEQ) % BLK_PER_LAT == 0)
    heads = [slice(h * DK_HGRN, (h + 1) * DK_HGRN) for h in range(H_HGRN)]

    @pl.when(first)
    def _():
        for s0_ref, scr in ((s0f_ref, stf_scr), (s0b_ref, stb_scr)):
            scr[...] = jnp.zeros(scr.shape, F32)
            for h, hs in enumerate(heads):
                scr[hs, hs] = s0_ref[h]

    lb = lb_ref[...]
    loglb = jnp.log(lb)
    log1mlb = jnp.log(1.0 - lb)
    n_chunks = T_BLK // CHUNK
    dirs = ((qf_ref, ff_ref, vf_ref, of_ref, stf_scr), (qb_ref, fb_ref, vb_ref, ob_ref, stb_scr))
    units = [(d, c if d == 0 else n_chunks - 1 - c) for c in range(n_chunks) for d in (0, 1)]
    rows = [slice(cc * CHUNK, (cc + 1) * CHUNK) for _, cc in units]
    end_row = (CHUNK - 1, 0)
    mid_row = (CHUNK // 2 - 1, CHUNK // 2)
    hm, bd = hm_ref[...], bd_ref[...]

    q = [_silu(dirs[d][0][r, :]) for (d, _), r in zip(units, rows)]
    v = [dirs[d][2][r, :] for (d, _), r in zip(units, rows)]
    g = []
    for (d, _), r in zip(units, rows):
        x = dirs[d][1][r, :]
        y = log1mlb[d:d + 1] + (jnp.minimum(x, 0.0) - jnp.log(1.0 + jnp.exp(-jnp.abs(x))))
        g.append(jnp.maximum(loglb[d:d + 1], y) + jnp.log(1.0 + jnp.exp(-jnp.abs(loglb[d:d + 1] - y))))
    kk = [1.0 - jnp.exp(gu) for gu in g]
    b = [_dot_exact(tri_ref[d], gu) for (d, _), gu in zip(units, g)]
    b_end = [bu[end_row[d]:end_row[d] + 1, :] for (d, _), bu in zip(units, b)]

    def finish(o_intra):
        q_in = [(qu * jnp.exp(bu)).astype(BF16) for qu, bu in zip(q, b)]
        k_e = [(ku * jnp.exp(be - bu)).astype(BF16) for ku, bu, be in zip(kk, b, b_end)]
        decay = [jnp.exp(be) for be in b_end]
        kv = [_dot_tn(vu.astype(BF16), ku) * bd for vu, ku in zip(v, k_e)]
        st = [stf_scr[...], stb_scr[...]]
        for u, (d, _) in enumerate(units):
            dirs[d][3][rows[u], :] = (o_intra[u] + _dot_nt(q_in[u], st[d].astype(BF16))).astype(BF16)
            st[d] = st[d] * decay[u] + kv[u]
        stf_scr[...] = st[0]
        stb_scr[...] = st[1]
        for h, hs in enumerate(heads):
            sf_ref[h] = st[0][hs, hs]
            sb_ref[h] = st[1][hs, hs]

    safe = jnp.max(jnp.concatenate([jnp.abs(be) for be in b_end], axis=0)) <= HGRN_SAFE_LOG_DECAY

    @pl.when(safe)
    def _():
        b_mid = [bu[mid_row[d]:mid_row[d] + 1, :] for (d, _), bu in zip(units, b)]
        q_t = [(qu * jnp.exp(bu - bm)).astype(BF16) for qu, bu, bm in zip(q, b, b_mid)]
        k_t = [ku * jnp.exp(bm - bu) for ku, bu, bm in zip(kk, b, b_mid)]
        k_bd = [(jnp.concatenate([ku] * H_HGRN, axis=0) * hm).astype(BF16) for ku in k_t]
        v_bd = [(jnp.concatenate([vu] * H_HGRN, axis=0) * hm).astype(BF16) for vu in v]
        sc = [(_dot_nt(qu, ku) * tri4_ref[d]).astype(BF16) for (d, _), qu, ku in zip(units, q_t, k_bd)]
        finish([_dot(su, vu) for su, vu in zip(sc, v_bd)])

    @pl.when(jnp.logical_not(safe))
    def _():
        t = lax.broadcasted_iota(jnp.int32, (CHUNK, 1), 0)
        head_sum = bd.astype(BF16)
        for u in range(len(units)):
            ks_scr[u] = kk[u]
            bs_scr[u] = b[u]
            vs_scr[u] = v[u]
            oi_scr[u] = jnp.zeros((CHUNK, W_HGRN), F32)

        def source_row(s, carry):
            for u, (d, _) in enumerate(units):
                seen = (t >= s) if d == 0 else (t <= s)
                e = jnp.exp(jnp.where(seen, b[u] - bs_scr[u, pl.ds(s, 1), :], -jnp.inf))
                p = (q[u] * ks_scr[u, pl.ds(s, 1), :] * e).astype(BF16)
                oi_scr[u] += _dot(p, head_sum) * vs_scr[u, pl.ds(s, 1), :]
            return carry
        lax.fori_loop(0, CHUNK, source_row, 0)
        finish([oi_scr[u] for u in range(len(units))])


def _hgrn_consts():
    t = np.arange(CHUNK)
    tri = np.stack([t[:, None] >= t[None, :], t[:, None] <= t[None, :]]).astype(np.float32)
    tri4 = np.tile(tri, (1, 1, H_HGRN))
    r = np.arange(H_HGRN * CHUNK)
    lane = np.arange(W_HGRN)
    hm = (r[:, None] // CHUNK == lane[None, :] // DK_HGRN).astype(np.float32)
    bd = (lane[:, None] // DV_HGRN == lane[None, :] // DK_HGRN).astype(np.float32)
    return jnp.asarray(tri), jnp.asarray(tri4), jnp.asarray(hm), jnp.asarray(bd)


def _hgrn(hg, lb2, s0f, s0b):
    tri, tri4, hm, bd = _hgrn_consts()

    def col(cb, reverse):
        return pl.BlockSpec((T_BLK, W_HGRN), lambda i: (_hgrn_blk(i, reverse), cb))

    def const(shape):
        return pl.BlockSpec(shape, lambda i: (0,) * len(shape))
    state = pl.BlockSpec((None, H_HGRN, DV_HGRN, DK_HGRN), lambda i: (_hgrn_seq(i), 0, 0, 0))
    return pl.pallas_call(
        _hgrn_kernel,
        grid=(N_HGRN_STEPS,),
        in_specs=[col(0, False), col(1, False), col(3, False), col(0, True), col(2, True), col(3, True),
                  const((2, W_HGRN)), state, state,
                  const(tri.shape), const(tri4.shape), const(hm.shape), const(bd.shape)],
        out_specs=[col(0, False), col(0, True), state, state],
        out_shape=[jax.ShapeDtypeStruct((T_ALL, W_HGRN), BF16)] * 2
        + [jax.ShapeDtypeStruct((N_SEQ, H_HGRN, DV_HGRN, DK_HGRN), F32)] * 2,
        scratch_shapes=[pltpu.VMEM((W_HGRN, W_HGRN), F32)] * 2
        + [pltpu.VMEM((2 * T_BLK // CHUNK, CHUNK, W_HGRN), F32)] * 4,
        compiler_params=_cparams(("arbitrary",)),
        name="hgrn",
    )(hg, hg, hg, hg, hg, hg, lb2, s0f, s0b, tri, tri4, hm, bd)


def _attn_kernel(q_ref, k_ref, v_ref, o_ref):
    lane = lax.broadcasted_iota(jnp.int32, (1, HEAD_PAD), 1)
    heads = [slice(h * HEAD_PAD, (h + 1) * HEAD_PAD) for h in range(q_ref.shape[1] // HEAD_PAD)]
    s = [_dot_nt(q_ref[:, hs], k_ref[:, hs]) for hs in heads]
    p = [jnp.exp2(sh - jnp.max(sh, axis=-1, keepdims=True)).astype(BF16) for sh in s]
    pv = [_dot(ph, v_ref[:, hs]) for ph, hs in zip(p, heads)]
    o = [jnp.where(lane < D_V, x * (1.0 / x[:, D_V:D_V + 1]), 0.0) for x in pv]
    for hp in range(len(heads) // 2):
        o_ref[:, hp * 2 * D_V:(hp + 1) * 2 * D_V] = (o[2 * hp] + pltpu.roll(o[2 * hp + 1], D_V, axis=1)).astype(BF16)


def _attn_lat_kernel(q_ref, kp_ref, vp_ref, kn_ref, vn_ref, o_ref, k_scr, v_scr):
    @pl.when(pl.program_id(2) == 0)
    def _():
        k_scr[0:PAST, :] = kp_ref[...]
        k_scr[PAST:, :] = kn_ref[...]
        v_scr[0:PAST, :] = vp_ref[...]
        v_scr[PAST:, :] = vn_ref[...]
    _attn_kernel(q_ref, k_scr, v_scr, o_ref)


def _attention_ctx(q, k, v):
    blk = pl.BlockSpec((L_CTX, W_HEADS), lambda b: (b, 0))
    return pl.pallas_call(
        _attn_kernel,
        grid=(N_CTX_SEQ,),
        in_specs=[blk, blk, blk],
        out_specs=pl.BlockSpec((L_CTX, W_MLA), lambda b: (b, 0)),
        out_shape=jax.ShapeDtypeStruct((T_CTX, W_MLA), BF16),
        compiler_params=_cparams(("arbitrary",)),
        name="attn_ctx",
    )(q, k, v)


def _attention_lat(q, k, v, k_past, v_past):
    grp = ATT_HEADS * HEAD_PAD
    nq = L_LAT // TQ
    q0 = T_CTX // TQ
    seq0 = T_CTX // L_LAT
    new = pl.BlockSpec((L_LAT, grp), lambda b, hp, qi: (seq0 + b, hp))
    past = pl.BlockSpec((None, PAST, grp), lambda b, hp, qi: (b, 0, hp))
    return pl.pallas_call(
        _attn_lat_kernel,
        grid=(N_LAT_SEQ, H_MLA // ATT_HEADS, nq),
        in_specs=[pl.BlockSpec((TQ, grp), lambda b, hp, qi: (q0 + b * nq + qi, hp)), past, past, new, new],
        out_specs=pl.BlockSpec((TQ, ATT_HEADS * D_V), lambda b, hp, qi: (b * nq + qi, hp)),
        out_shape=jax.ShapeDtypeStruct((T_LAT, W_MLA), BF16),
        scratch_shapes=[pltpu.VMEM((PAST + L_LAT, grp), BF16)] * 2,
        compiler_params=_cparams(("arbitrary", "arbitrary", "arbitrary")),
        name="attn_lat",
    )(q, k_past, v_past, k, v)


def _dft_tables(n):
    a = 2.0 * np.pi * np.outer(np.arange(n), np.arange(n)) / n
    return np.cos(a), np.sin(a)


def _bf16_operand(table):
    return jnp.asarray(table, F32).astype(BF16)


def _fnet_finish(re, im, cs_ref, w_ref, g_ref):
    spec = _dot(jnp.concatenate([re, im], axis=1).astype(BF16), cs_ref[...])
    return _rms_rows(_dot(spec.astype(BF16), w_ref[...]), g_ref[...])


def _fnet_channel_consts(w_bd, g_row):
    c, s = _dft_tables(C_FNET)
    eye = np.eye(G_FNET)
    return [_bf16_operand(np.concatenate([np.kron(eye, c), np.kron(eye, s)], axis=0)), w_bd, g_row]


def _fnet_channel_specs():
    zero = lambda *a: (0, 0)
    return [pl.BlockSpec((2 * W_FNET, W_FNET), zero), pl.BlockSpec((W_FNET, W_FNET), zero),
            pl.BlockSpec((1, W_FNET), zero)]


def _fnet_ctx_kernel(u_ref, m_ref, cs_ref, w_ref, g_ref, o_ref):
    p = _dot(m_ref[...], u_ref[...].astype(BF16))
    o_ref[...] = _fnet_finish(p[:L_CTX], p[L_CTX:], cs_ref, w_ref, g_ref)


def _fnet_ctx(uf, w_bd, g_row):
    c, s = _dft_tables(L_CTX)
    norm = 1.0 / np.sqrt(L_CTX * C_FNET)
    blk = pl.BlockSpec((L_CTX, W_FNET), lambda b: (b, 0))
    return pl.pallas_call(
        _fnet_ctx_kernel,
        grid=(N_CTX_SEQ,),
        in_specs=[blk, pl.BlockSpec((2 * L_CTX, L_CTX), lambda b: (0, 0))] + _fnet_channel_specs(),
        out_specs=blk,
        out_shape=jax.ShapeDtypeStruct((T_CTX, W_FNET), F32),
        compiler_params=_cparams(("arbitrary",)),
        name="fnet_ctx",
    )(uf, _bf16_operand(np.concatenate([c, -s], axis=0) * norm), *_fnet_channel_consts(w_bd, g_row))


FN_SUB = 8


def _fnet_rows_kernel(x_ref, m_ref, re_ref, im_ref):
    for s in range(FN_SUB):
        y = _dot(m_ref[s], x_ref[:, s, :].astype(BF16))
        re_ref[s] = y[:GRID_W]
        im_ref[s] = y[GRID_W:]


def _fnet_cols_kernel(re_ref, im_ref, m_ref, cs_ref, w_ref, g_ref, o_ref):
    z = [_dot(m_ref[...], jnp.concatenate([re_ref[:, s, :], im_ref[:, s, :]], axis=0).astype(BF16))
         for s in range(FN_SUB)]
    out = _fnet_finish(jnp.concatenate([zs[:GRID_W] for zs in z], axis=0),
                       jnp.concatenate([zs[GRID_W:] for zs in z], axis=0), cs_ref, w_ref, g_ref)
    for s in range(FN_SUB):
        o_ref[:, s, :] = out[s * GRID_W:(s + 1) * GRID_W]


def _fnet_lat(uf, w_bd, g_row):
    p1 = np.arange(GRID_W)[None, :, None]
    l = GRID_W * np.arange(GRID_W)[None, None, :] + np.arange(GRID_W)[:, None, None]
    ang = 2.0 * np.pi * p1 * l / L_LAT
    norm = 1.0 / np.sqrt(L_LAT * C_FNET)
    m_rows = _bf16_operand(np.concatenate([np.cos(ang), -np.sin(ang)], axis=1) * norm)
    c64, s64 = _dft_tables(GRID_W)
    m_cols = _bf16_operand(np.block([[c64, s64], [-s64, c64]]))
    x4 = uf.reshape(T_ALL // L_LAT, GRID_W, GRID_W, W_FNET)
    seq0 = T_CTX // L_LAT
    grid = (N_LAT_SEQ, GRID_W // FN_SUB)
    mid = pl.BlockSpec((None, FN_SUB, GRID_W, W_FNET), lambda b, j: (b, j, 0, 0))
    mid_shape = jax.ShapeDtypeStruct((N_LAT_SEQ, GRID_W, GRID_W, W_FNET), F32)
    re, im = pl.pallas_call(
        _fnet_rows_kernel,
        grid=grid,
        in_specs=[pl.BlockSpec((None, GRID_W, FN_SUB, W_FNET), lambda b, j: (seq0 + b, 0, j, 0)),
                  pl.BlockSpec((FN_SUB, 2 * GRID_W, GRID_W), lambda b, j: (j, 0, 0))],
        out_specs=[mid, mid],
        out_shape=[mid_shape, mid_shape],
        compiler_params=_cparams(("arbitrary", "arbitrary")),
        name="fnet_rows_lat",
    )(x4, m_rows)
    strided = pl.BlockSpec((None, GRID_W, FN_SUB, W_FNET), lambda b, j: (b, 0, j, 0))
    out = pl.pallas_call(
        _fnet_cols_kernel,
        grid=grid,
        in_specs=[strided, strided, pl.BlockSpec((2 * GRID_W, 2 * GRID_W), lambda b, j: (0, 0))]
        + _fnet_channel_specs(),
        out_specs=strided,
        out_shape=mid_shape,
        compiler_params=_cparams(("arbitrary", "arbitrary")),
        name="fnet_cols_lat",
    )(re, im, m_cols, *_fnet_channel_consts(w_bd, g_row))
    return out.reshape(T_LAT, W_FNET)


def _pad_heads(w, d_head):
    lead = w.shape[:-1]
    w = w.reshape(lead + (H_MLA, d_head))
    w = jnp.pad(w, [(0, 0)] * len(lead) + [(0, 0), (0, HEAD_PAD - d_head)])
    return w.reshape(lead + (W_HEADS,))


def _swap_rope(a):
    n = D_ROPE // 4
    parts = [jnp.zeros(a.shape[:-1] + (D_NOPE,), a.dtype)]
    for ax in range(2):
        base = D_NOPE + ax * 2 * n
        parts += [a[..., base + n:base + 2 * n], a[..., base:base + n]]
    parts.append(jnp.zeros(a.shape[:-1] + (HEAD_PAD - D_QK,), a.dtype))
    return jnp.concatenate(parts, axis=-1)


def _rope_tables(g_q, g_k):
    n_freq = D_ROPE // 4
    t = np.arange(L_LAT)
    pos = np.stack([t // GRID_W, t % GRID_W], axis=-1).astype(np.float32)
    freq = (np.float32(ROPE_THETA) ** (-np.arange(n_freq, dtype=np.float32) / n_freq)).astype(np.float32)
    ang = (pos[:, :, None] * freq).astype(np.float32)
    cos = np.ones((L_LAT, HEAD_PAD), np.float32)
    sin = np.zeros((L_LAT, HEAD_PAD), np.float32)
    for ax in range(2):
        base = D_NOPE + ax * 2 * n_freq
        c, s = np.cos(ang[:, ax, :]), np.sin(ang[:, ax, :])
        cos[:, base:base + n_freq] = c
        cos[:, base + n_freq:base + 2 * n_freq] = c
        sin[:, base:base + n_freq] = -s
        sin[:, base + n_freq:base + 2 * n_freq] = s
    cos = np.concatenate([np.ones((TM, HEAD_PAD), np.float32), cos]).reshape(1, 1 + LAT_TILES_PER_SEQ, TM, HEAD_PAD)
    sin = np.concatenate([np.zeros((TM, HEAD_PAD), np.float32), sin]).reshape(1, 1 + LAT_TILES_PER_SEQ, TM, HEAD_PAD)

    def pair(g, scale):
        g128 = jnp.pad(g, ((0, 0), (0, HEAD_PAD - D_QK))) * scale
        g_partner = _swap_rope(g128)
        return (cos * g128[:, None, None, :], sin * g_partner[:, None, None, :])
    return pair(g_q, D_QK ** -0.5 * LOG2_E) + pair(g_k, 1.0)


def kernel(x_prompt, x_sample, cache_ckv, cache_krope, state_hgrn, c, c_ctx, ada_w, ada_b, norm_g, ffn_w_gu, ffn_w_down, w_in, hgrn_lb, hgrn_norm_g, mla_q_norm_g, mla_w_q_up, mla_kv_norm_g, mla_w_kv_up, mla_qk_norm_g, mla_out_norm_g, fnet_w, fnet_norm_g, w_out):
    w_gu = ffn_w_gu.astype(BF16)
    w_down = ffn_w_down.astype(BF16)
    o = np.cumsum((0, 5 * W_HGRN, Q_RANK, KV_RANK, D_ROPE, W_FNET))
    w_kr = jnp.pad(w_in[:, :, o[3]:o[4]], ((0, 0), (0, 0), (D_NOPE, HEAD_PAD - D_NOPE - D_ROPE)))
    w_in_arr = jnp.concatenate([w_in[:, :, :o[3]], w_kr, _swap_rope(w_kr), w_in[:, :, o[4]:]], axis=-1).astype(BF16)
    wq = _pad_heads(mla_w_q_up, D_QK)
    wq_sw = _swap_rope(wq.reshape(DEPTH, Q_RANK, H_MLA, HEAD_PAD)).reshape(DEPTH, Q_RANK, W_HEADS)
    wq2 = jnp.concatenate([wq, wq_sw], axis=-1).astype(BF16)
    w_kv = mla_w_kv_up.reshape(DEPTH, KV_RANK, H_MLA, D_NOPE + D_V)
    wk_arr = _pad_heads(w_kv[..., :D_NOPE].reshape(DEPTH, KV_RANK, H_MLA * D_NOPE), D_NOPE).astype(BF16)
    wv_arr = _pad_heads(w_kv[..., D_NOPE:].reshape(DEPTH, KV_RANK, H_MLA * D_V), D_V).astype(BF16)
    w_out_b = w_out.astype(BF16)
    eye_g = jnp.eye(G_FNET, dtype=F32)
    fnet_bd = jnp.einsum("lgcd,gh->lgchd", fnet_w, eye_g).reshape(DEPTH, W_FNET, W_FNET).astype(BF16)
    tabs = _rope_tables(mla_qk_norm_g[:, 0], mla_qk_norm_g[:, 1])
    lbs = jnp.cumsum(jax.nn.softmax(hgrn_lb.astype(F32), axis=0), axis=0)
    lbs = lbs - lbs[:1]

    cond8 = jnp.zeros((8, D), F32).at[0].set(c_ctx).at[1:1 + N_LAT_SEQ].set(c)
    mods = _mods(cond8, ada_w, ada_b).reshape(DEPTH, 8, N_MOD, D)

    x = (x_prompt.reshape(T_CTX, D), x_sample.reshape(T_LAT, D))
    ckv_out, kr_out, st_out = [], [], []
    zero_states = jnp.zeros((N_CTX_SEQ, H_HGRN, DV_HGRN, DK_HGRN), F32)
    for l in range(DEPTH):
        x = _ffn(x, mods, norm_g, w_gu, w_down, l, 0)
        hg, ckvn, kr128, uf, q, k, v = _inproj(x, mods, norm_g, w_in_arr, mla_q_norm_g, mla_kv_norm_g, wq2, wk_arr,
                                               wv_arr, tabs, l)

        s0 = [jnp.concatenate([zero_states, jnp.swapaxes(state_hgrn[:, l, d], -1, -2)], axis=0) for d in range(2)]
        h_fwd, h_bwd, s_f, s_b = _hgrn(hg, lbs[l], s0[0], s0[1])
        gn_row = jnp.tile(hgrn_norm_g[l], H_HGRN).reshape(1, W_HGRN)
        st_out.append(jnp.swapaxes(jnp.stack([s_f[:N_CTX_SEQ], s_b[:N_CTX_SEQ]], axis=1), -1, -2))

        cache_kr = jnp.pad(cache_krope[:, l], ((0, 0), (0, 0), (D_NOPE, HEAD_PAD - D_NOPE - D_ROPE)))
        k_past, v_past = _kv_cache(cache_ckv[:, l], cache_kr, _swap_rope(cache_kr), wk_arr, wv_arr, tabs, l)
        att_ctx = _attention_ctx(q, k, v)
        att_lat = _attention_lat(q, k, v, k_past, v_past)

        fn_gain = fnet_norm_g[l].reshape(1, W_FNET)
        fn_ctx = _fnet_ctx(uf, fnet_bd[l], fn_gain)
        fn_lat = _fnet_lat(uf, fnet_bd[l], fn_gain)

        x = _outproj(x, mods, h_fwd, h_bwd, hg, gn_row, att_ctx, att_lat, fn_ctx, fn_lat, mla_out_norm_g, w_out_b, l)
        if l == DEPTH - 1:
            y_prompt = _ffn(x, mods, norm_g, w_gu, w_down, l, 1, n_rows=T_CTX)
            y_sample = _ffn(x, mods, norm_g, w_gu, w_down, l, 1, src_row0=T_CTX, n_rows=T_LAT)
        else:
            x = _ffn(x, mods, norm_g, w_gu, w_down, l, 1)

        ckv_out.append(ckvn[:T_CTX].reshape(N_CTX_SEQ, L_CTX, KV_RANK))
        kr_out.append(kr128[:T_CTX, D_NOPE:D_NOPE + D_ROPE].reshape(N_CTX_SEQ, L_CTX, D_ROPE))

    return (y_prompt.reshape(N_CTX_SEQ, L_CTX, D), y_sample.reshape(N_LAT_SEQ, L_LAT, D),
            jnp.stack(ckv_out, axis=1), jnp.stack(kr_out, axis=1), jnp.stack(st_out, axis=1))
```

```python
import functools

import numpy as np
import jax
import jax.numpy as jnp
from jax import lax
from jax.experimental import pallas as pl
from jax.experimental.pallas import tpu as pltpu

F32 = jnp.float32
BF16 = jnp.bfloat16
HIGHEST = lax.Precision.HIGHEST

D = 1024
N_CTX_SEQ, L_CTX = 32, 256
N_LAT_SEQ, L_LAT = 2, 4096
T_CTX = N_CTX_SEQ * L_CTX
T_LAT = N_LAT_SEQ * L_LAT
T_ALL = T_CTX + T_LAT
DEPTH = 2
PAST = 512
GRID_W = 64
N_MOD = 9
EPS = 1e-6

H_HGRN, DK_HGRN, DV_HGRN = 4, 64, 64
W_HGRN = H_HGRN * DV_HGRN
CHUNK = 32
HGRN_SAFE_LOG_DECAY = 60.0
H_MLA, Q_RANK, KV_RANK = 8, 384, 256
D_NOPE, D_ROPE, D_V = 64, 32, 64
D_QK = D_NOPE + D_ROPE
HEAD_PAD = 128
W_HEADS = H_MLA * HEAD_PAD
W_MLA = H_MLA * D_V
G_FNET, C_FNET = 4, 64
W_FNET = G_FNET * C_FNET
D_FF = 2816
FF_CHUNK = 256
ROPE_THETA = 10000.0
LOG2_E = 1.4426950408889634

IN_HG = 5 * W_HGRN
IN_MAIN = IN_HG + Q_RANK + KV_RANK
IN_TAIL = 2 * HEAD_PAD + W_FNET

TM = 512
FFN_TM = 512
N_TILES = T_ALL // TM
N_CTX_TILES = T_CTX // TM
LAT_TILES_PER_SEQ = L_LAT // TM
T_BLK = 256
TQ = 512
ATT_HEADS = 2
CTX_SEQ_PER_STEP = 4

VMEM_LIMIT = 56 * 1024 * 1024


def _cparams(sem):
    return pltpu.CompilerParams(dimension_semantics=sem, vmem_limit_bytes=VMEM_LIMIT)


def _tile_group(i, tm=TM):
    return (i >= T_CTX // tm).astype(jnp.int32) + (i >= (T_CTX + L_LAT) // tm).astype(jnp.int32)


def _silu(x):
    return x * (1.0 / (1.0 + jnp.exp(-x)))


def _rms_rows(x, g):
    return x * lax.rsqrt(jnp.mean(x * x, axis=-1, keepdims=True) + EPS) * g


def _dot(a, b):
    return jnp.dot(a, b, preferred_element_type=F32)


def _dot_nt(a, b):
    return lax.dot_general(a, b, (((1,), (1,)), ((), ())), preferred_element_type=F32)


def _dot_tn(a, b):
    return lax.dot_general(a, b, (((0,), (0,)), ((), ())), preferred_element_type=F32)


def _dot_exact(a, b):
    return jnp.dot(a, b, preferred_element_type=F32, precision=HIGHEST)


def _step_tile(i):
    return i


def _row_spec(width, col_block=0, tile=_step_tile, tm=TM):
    return pl.BlockSpec((tm, width), lambda i: (tile(i), col_block))


def _ctx_row_spec(width, tile=_step_tile, tm=TM):
    return pl.BlockSpec((tm, width), lambda i: (jnp.minimum(tile(i), T_CTX // tm - 1), 0))


def _lat_row_spec(width, tile=_step_tile, tm=TM):
    return pl.BlockSpec((tm, width), lambda i: (jnp.maximum(tile(i) - T_CTX // tm, 0), 0))


def _mod_spec(layer, tile=_step_tile, tm=TM):
    return pl.BlockSpec((None, None, N_MOD, D), lambda i: (layer, _tile_group(tile(i), tm), 0, 0))


def _resident(shape, index_map):
    return pl.BlockSpec(shape, index_map, pipeline_mode=pl.Buffered(1))


def _mods_kernel(c_ref, w_ref, b_ref, o_ref):
    a = _silu(c_ref[...]).astype(BF16)
    o_ref[...] = _dot(a, w_ref[...].astype(BF16)) + b_ref[...]


def _mods(cond8, ada_w, ada_b):
    tn = 1024
    return pl.pallas_call(
        _mods_kernel,
        grid=(DEPTH, N_MOD * D // tn),
        in_specs=[
            pl.BlockSpec((8, D), lambda l, j: (0, 0)),
            pl.BlockSpec((None, D, tn), lambda l, j: (l, 0, j)),
            pl.BlockSpec((None, 1, tn), lambda l, j: (l, 0, j)),
        ],
        out_specs=pl.BlockSpec((None, 8, tn), lambda l, j: (l, 0, j)),
        out_shape=jax.ShapeDtypeStruct((DEPTH, 8, N_MOD * D), F32),
        compiler_params=_cparams(("arbitrary", "arbitrary")),
        name="ada_mods",
    )(cond8, ada_w, ada_b.reshape(DEPTH, 1, N_MOD * D))


def _mixer_out(x, mod_ref, ctx, hf_ref, hb_ref, hgate_ref, gn_ref, ones_ref, ac_ref, al_ref, fc_ref, fl_ref, ga_ref,
               w_ref):
    gate = mod_ref[5:6, :]
    oh = hf_ref[...].astype(F32) + hb_ref[...].astype(F32)
    ms = _dot_exact(oh * oh, ones_ref[...])
    oh = oh * lax.rsqrt(ms + EPS) * gn_ref[...] * _silu(hgate_ref[...])
    oa = _rms_rows(jnp.where(ctx, ac_ref[...], al_ref[...]).astype(F32), ga_ref[...])
    of = jnp.where(ctx, fc_ref[...], fl_ref[...])
    o = (_dot(oh.astype(BF16), w_ref[0:W_HGRN, :])
         + _dot(oa.astype(BF16), w_ref[W_HGRN:W_HGRN + W_MLA, :])
         + _dot(of.astype(BF16), w_ref[W_HGRN + W_MLA:, :]))
    return x + gate * o


def _outproj_kernel(x_ref, mod_ref, *rest):
    o_ref = rest[-1]
    o_ref[...] = _mixer_out(x_ref[...], mod_ref, pl.program_id(0) < N_CTX_TILES, *rest[:-1])


def _outproj(x, mods, h_fwd, h_bwd, hg, gn_row, att_ctx, att_lat, fn_ctx, fn_lat, g_att, w_out, layer):
    lane = np.arange(W_HGRN)
    ones_bd = jnp.asarray((lane[:, None] // DV_HGRN == lane[None, :] // DV_HGRN).astype(np.float32) / DV_HGRN)
    return pl.pallas_call(
        _outproj_kernel,
        grid=(N_TILES,),
        in_specs=[_row_spec(D), _mod_spec(layer), _row_spec(W_HGRN), _row_spec(W_HGRN), _row_spec(W_HGRN, 4),
                  pl.BlockSpec((1, W_HGRN), lambda i: (0, 0)), pl.BlockSpec((W_HGRN, W_HGRN), lambda i: (0, 0)),
                  _ctx_row_spec(W_MLA), _lat_row_spec(W_MLA), _ctx_row_spec(W_FNET), _lat_row_spec(W_FNET),
                  pl.BlockSpec((None, 1, W_MLA), lambda i: (layer, 0, 0)),
                  _resident((None, D, D), lambda i: (layer, 0, 0))],
        out_specs=_row_spec(D),
        out_shape=jax.ShapeDtypeStruct((T_ALL, D), F32),
        compiler_params=_cparams(("arbitrary",)),
        name=f"outproj_l{layer}",
    )(x, mods, h_fwd, h_bwd, hg, gn_row, ones_bd, att_ctx, att_lat, fn_ctx, fn_lat,
      g_att.reshape(DEPTH, 1, W_MLA), w_out)


def _ffn_kernel(*refs, mi, two_sources):
    if two_sources:
        xc_ref, xl_ref, mod_ref, g_ref, wg_ref, wu_ref, wd_ref, o_ref = refs
        x = jnp.where(pl.program_id(0) < T_CTX // FFN_TM, xc_ref[...], xl_ref[...])
    else:
        x_ref, mod_ref, g_ref, wg_ref, wu_ref, wd_ref, o_ref = refs
        x = x_ref[...]
    shift = mod_ref[mi:mi + 1, :]
    scale = mod_ref[mi + 1:mi + 2, :]
    gate = mod_ref[mi + 2:mi + 3, :]
    hb = (_rms_rows(x, g_ref[...]) * (1.0 + scale) + shift).astype(BF16)
    acc = jnp.zeros(x.shape, F32)
    for j in range(D_FF // FF_CHUNK):
        cs = slice(j * FF_CHUNK, (j + 1) * FF_CHUNK)
        a = _silu(_dot(hb, wg_ref[:, cs])) * _dot(hb, wu_ref[:, cs])
        acc = acc + _dot(a.astype(BF16), wd_ref[cs, :])
    o_ref[...] = x + 0.5 * gate * acc


def _ffn(xs, mods, norm_g, w_gu, w_down, layer, which, *, src_row0=0, n_rows=T_ALL):
    mi = 0 if which == 0 else 6
    gi = 0 if which == 0 else 2
    tm = FFN_TM

    def tile(i):
        return i + src_row0 // tm
    two = isinstance(xs, tuple)
    if two:
        in_specs = [_ctx_row_spec(D, tm=tm), _lat_row_spec(D, tm=tm)]
        args = list(xs)
    else:
        in_specs = [_row_spec(D, tile=tile, tm=tm)]
        args = [xs]
    in_specs += [
        _mod_spec(layer, tile, tm),
        pl.BlockSpec((None, None, 1, D), lambda i: (layer, gi, 0, 0)),
        _resident((None, None, D, D_FF), lambda i: (layer, which, 0, 0)),
        _resident((None, None, D, D_FF), lambda i: (layer, which, 0, 1)),
        _resident((None, None, D_FF, D), lambda i: (layer, which, 0, 0)),
    ]
    args += [mods, norm_g.reshape(DEPTH, 3, 1, D), w_gu, w_gu, w_down]
    return pl.pallas_call(
        functools.partial(_ffn_kernel, mi=mi, two_sources=two),
        grid=(n_rows // tm,),
        in_specs=in_specs,
        out_specs=pl.BlockSpec((tm, D), lambda i: (i, 0)),
        out_shape=jax.ShapeDtypeStruct((n_rows, D), F32),
        compiler_params=_cparams(("arbitrary",)),
        name=f"ffn_l{layer}_{which}",
    )(*args)


def _mla_heads(qa, ka, va, kr, kr_sw, tabs, q_ref, k_ref, v_ref):
    cq, sq, ck, sk = tabs
    lane = lax.broadcasted_iota(jnp.int32, (1, W_HEADS), 1) % HEAD_PAD
    v_ref[...] = jnp.where(lane == D_V, 1.0, va).astype(BF16)
    k_rot = kr_sw * sk
    heads = [slice(h * HEAD_PAD, (h + 1) * HEAD_PAD) for h in range(H_MLA)]

    def inv_rms(xs):
        return [lax.rsqrt(jnp.sum(x * x, axis=-1, keepdims=True) * (1.0 / D_QK) + EPS) for x in xs]
    if qa is not None:
        q = [qa[:, hs] for hs in heads]
        q_sw = [qa[:, W_HEADS + hs.start:W_HEADS + hs.stop] for hs in heads]
        for hs, x, x_sw, rs in zip(heads, q, q_sw, inv_rms(q)):
            q_ref[:, hs] = (rs * (x * cq + x_sw * sq)).astype(BF16)
    k = [ka[:, hs] + kr for hs in heads]
    for hs, x, rs in zip(heads, k, inv_rms(k)):
        k_ref[:, hs] = (rs * (x * ck + k_rot)).astype(BF16)


def _inproj_kernel(x_ref, mod_ref, g_ref, w_ref, wt_ref, gq_ref, gkv_ref, wq_ref, wk_ref, wv_ref, cq_ref, sq_ref,
                   ck_ref, sk_ref, hg_ref, ckv_ref, kr_ref, uf_ref, q_ref, k_ref, v_ref):
    x = x_ref[...]
    shift = mod_ref[3:4, :]
    scale = mod_ref[4:5, :]
    hb = (_rms_rows(x, g_ref[...]) * (1.0 + scale) + shift).astype(BF16)
    u = _dot(hb, w_ref[...])
    ut = _dot(hb, wt_ref[...])
    o = 0
    hg_ref[...] = u[:, o:o + IN_HG]
    o += IN_HG
    cqn = _rms_rows(u[:, o:o + Q_RANK], gq_ref[...])
    o += Q_RANK
    ckvn = _rms_rows(u[:, o:o + KV_RANK], gkv_ref[...])
    ckv_ref[...] = ckvn
    kr = ut[:, 0:HEAD_PAD]
    kr_ref[...] = kr
    kr_sw = ut[:, HEAD_PAD:2 * HEAD_PAD]
    uf_ref[...] = ut[:, 2 * HEAD_PAD:]
    cb = ckvn.astype(BF16)
    _mla_heads(_dot(cqn.astype(BF16), wq_ref[...]), _dot(cb, wk_ref[...]), _dot(cb, wv_ref[...]), kr, kr_sw,
               (cq_ref[...], sq_ref[...], ck_ref[...], sk_ref[...]), q_ref, k_ref, v_ref)


def _rope_block(i):
    return jnp.where(i < N_CTX_TILES, 0, 1 + (i - N_CTX_TILES) % LAT_TILES_PER_SEQ)


def _inproj(x, mods, norm_g, w_in_b, w_tail, gq, gkv, wq2, wk_arr, wv_arr, tabs, layer):
    tab = pl.BlockSpec((None, None, TM, HEAD_PAD), lambda i: (layer, _rope_block(i), 0, 0))
    f32_widths = (IN_HG, KV_RANK, HEAD_PAD, W_FNET)
    return pl.pallas_call(
        _inproj_kernel,
        grid=(N_TILES,),
        in_specs=[
            _row_spec(D),
            _mod_spec(layer),
            pl.BlockSpec((None, None, 1, D), lambda i: (layer, 1, 0, 0)),
            _resident((None, D, IN_MAIN), lambda i: (layer, 0, 0)),
            _resident((None, D, IN_TAIL), lambda i: (layer, 0, 0)),
            pl.BlockSpec((None, 1, Q_RANK), lambda i: (layer, 0, 0)),
            pl.BlockSpec((None, 1, KV_RANK), lambda i: (layer, 0, 0)),
            _resident((None, Q_RANK, 2 * W_HEADS), lambda i: (layer, 0, 0)),
            _resident((None, KV_RANK, W_HEADS), lambda i: (layer, 0, 0)),
            _resident((None, KV_RANK, W_HEADS), lambda i: (layer, 0, 0)),
            tab, tab, tab, tab,
        ],
        out_specs=[_row_spec(w) for w in f32_widths] + [_row_spec(W_HEADS)] * 3,
        out_shape=[jax.ShapeDtypeStruct((T_ALL, w), F32) for w in f32_widths]
        + [jax.ShapeDtypeStruct((T_ALL, W_HEADS), BF16)] * 3,
        compiler_params=_cparams(("arbitrary",)),
        name=f"inproj_l{layer}",
    )(x, mods, norm_g.reshape(DEPTH, 3, 1, D), w_in_b, w_tail, gq.reshape(DEPTH, 1, Q_RANK),
      gkv.reshape(DEPTH, 1, KV_RANK), wq2, wk_arr, wv_arr, *tabs)


def _kv_cache_kernel(ckv_ref, kr_ref, krsw_ref, wk_ref, wv_ref, ck_ref, sk_ref, k_ref, v_ref):
    cb = ckv_ref[...].astype(BF16)
    _mla_heads(None, _dot(cb, wk_ref[...]), _dot(cb, wv_ref[...]), kr_ref[...], krsw_ref[...],
               (None, None, ck_ref[...], sk_ref[...]), None, k_ref, v_ref)


def _kv_cache(cache_ckv_l, cache_kr, cache_kr_sw, wk_arr, wv_arr, tabs, layer):
    assert PAST == TM
    blk = lambda w: pl.BlockSpec((None, PAST, w), lambda b: (b, 0, 0))
    tab = pl.BlockSpec((None, None, TM, HEAD_PAD), lambda b: (layer, 0, 0, 0))
    return pl.pallas_call(
        _kv_cache_kernel,
        grid=(N_LAT_SEQ,),
        in_specs=[blk(KV_RANK), blk(HEAD_PAD), blk(HEAD_PAD),
                  _resident((None, KV_RANK, W_HEADS), lambda b: (layer, 0, 0)),
                  _resident((None, KV_RANK, W_HEADS), lambda b: (layer, 0, 0)), tab, tab],
        out_specs=[blk(W_HEADS)] * 2,
        out_shape=[jax.ShapeDtypeStruct((N_LAT_SEQ, PAST, W_HEADS), BF16)] * 2,
        compiler_params=_cparams(("arbitrary",)),
        name=f"mla_kv_cache_l{layer}",
    )(cache_ckv_l, cache_kr, cache_kr_sw, wk_arr, wv_arr, tabs[2], tabs[3])


N_SEQ = N_CTX_SEQ + N_LAT_SEQ
BLK_PER_LAT = L_LAT // T_BLK
N_HGRN_STEPS = N_CTX_SEQ + N_LAT_SEQ * BLK_PER_LAT


def _hgrn_seq(i):
    return jnp.where(i < N_CTX_SEQ, i, N_CTX_SEQ + (i - N_CTX_SEQ) // BLK_PER_LAT)


def _hgrn_blk(i, reverse):
    j = (i - N_CTX_SEQ) % BLK_PER_LAT
    if reverse:
        j = BLK_PER_LAT - 1 - j
    lat = N_CTX_SEQ + ((i - N_CTX_SEQ) // BLK_PER_LAT) * BLK_PER_LAT + j
    return jnp.where(i < N_CTX_SEQ, i, lat)


def _hgrn_kernel(qf_ref, ff_ref, vf_ref, qb_ref, fb_ref, vb_ref, lb_ref, s0f_ref, s0b_ref,
                 tri_ref, tri4_ref, hm_ref, bd_ref, of_ref, ob_ref, sf_ref, sb_ref, stf_scr, stb_scr,
                 ks_scr, bs_scr, vs_scr, oi_scr):
    i = pl.program_id(0)
    first = jnp.logical_or(i < N_CTX_SEQ, (i - N_CTX_SEQ) % BLK_PER_LAT == 0)
    heads = [slice(h * DK_HGRN, (h + 1) * DK_HGRN) for h in range(H_HGRN)]

    @pl.when(first)
    def _():
        for s0_ref, scr in ((s0f_ref, stf_scr), (s0b_ref, stb_scr)):
            scr[...] = jnp.zeros(scr.shape, F32)
            for h, hs in enumerate(heads):
                scr[hs, hs] = s0_ref[h]

    lb = lb_ref[...]
    loglb = jnp.log(lb)
    log1mlb = jnp.log(1.0 - lb)
    n_chunks = T_BLK // CHUNK
    dirs = ((qf_ref, ff_ref, vf_ref, of_ref, stf_scr), (qb_ref, fb_ref, vb_ref, ob_ref, stb_scr))
    units = [(d, c if d == 0 else n_chunks - 1 - c) for c in range(n_chunks) for d in (0, 1)]
    rows = [slice(cc * CHUNK, (cc + 1) * CHUNK) for _, cc in units]
    end_row = (CHUNK - 1, 0)
    mid_row = (CHUNK // 2 - 1, CHUNK // 2)
    hm, bd = hm_ref[...], bd_ref[...]

    q = [_silu(dirs[d][0][r, :]) for (d, _), r in zip(units, rows)]
    v = [dirs[d][2][r, :] for (d, _), r in zip(units, rows)]
    g = []
    for (d, _), r in zip(units, rows):
        x = dirs[d][1][r, :]
        y = log1mlb[d:d + 1] + (jnp.minimum(x, 0.0) - jnp.log(1.0 + jnp.exp(-jnp.abs(x))))
        g.append(jnp.maximum(loglb[d:d + 1], y) + jnp.log(1.0 + jnp.exp(-jnp.abs(loglb[d:d + 1] - y))))
    kk = [1.0 - jnp.exp(gu) for gu in g]
    b = [_dot_exact(tri_ref[d], gu) for (d, _), gu in zip(units, g)]
    b_end = [bu[end_row[d]:end_row[d] + 1, :] for (d, _), bu in zip(units, b)]

    def finish(o_intra):
        q_in = [(qu * jnp.exp(bu)).astype(BF16) for qu, bu in zip(q, b)]
        k_e = [(ku * jnp.exp(be - bu)).astype(BF16) for ku, bu, be in zip(kk, b, b_end)]
        decay = [jnp.exp(be) for be in b_end]
        kv = [_dot_tn(vu.astype(BF16), ku) * bd for vu, ku in zip(v, k_e)]
        st = [stf_scr[...], stb_scr[...]]
        for u, (d, _) in enumerate(units):
            dirs[d][3][rows[u], :] = (o_intra[u] + _dot_nt(q_in[u], st[d].astype(BF16))).astype(BF16)
            st[d] = st[d] * decay[u] + kv[u]
        stf_scr[...] = st[0]
        stb_scr[...] = st[1]
        for h, hs in enumerate(heads):
            sf_ref[h] = st[0][hs, hs]
            sb_ref[h] = st[1][hs, hs]

    safe = jnp.max(jnp.concatenate([jnp.abs(be) for be in b_end], axis=0)) <= HGRN_SAFE_LOG_DECAY

    @pl.when(safe)
    def _():
        b_mid = [bu[mid_row[d]:mid_row[d] + 1, :] for (d, _), bu in zip(units, b)]
        q_t = [(qu * jnp.exp(bu - bm)).astype(BF16) for qu, bu, bm in zip(q, b, b_mid)]
        k_t = [ku * jnp.exp(bm - bu) for ku, bu, bm in zip(kk, b, b_mid)]
        k_bd = [(jnp.concatenate([ku] * H_HGRN, axis=0) * hm).astype(BF16) for ku in k_t]
        v_bd = [(jnp.concatenate([vu] * H_HGRN, axis=0) * hm).astype(BF16) for vu in v]
        sc = [(_dot_nt(qu, ku) * tri4_ref[d]).astype(BF16) for (d, _), qu, ku in zip(units, q_t, k_bd)]
        finish([_dot(su, vu) for su, vu in zip(sc, v_bd)])

    @pl.when(jnp.logical_not(safe))
    def _():
        t = lax.broadcasted_iota(jnp.int32, (CHUNK, 1), 0)
        head_sum = bd.astype(BF16)
        for u in range(len(units)):
            ks_scr[u] = kk[u]
            bs_scr[u] = b[u]
            vs_scr[u] = v[u]
            oi_scr[u] = jnp.zeros((CHUNK, W_HGRN), F32)

        def source_row(s, carry):
            for u, (d, _) in enumerate(units):
                seen = (t >= s) if d == 0 else (t <= s)
                e = jnp.exp(jnp.where(seen, b[u] - bs_scr[u, pl.ds(s, 1), :], -jnp.inf))
                p = (q[u] * ks_scr[u, pl.ds(s, 1), :] * e).astype(BF16)
                oi_scr[u] += _dot(p, head_sum) * vs_scr[u, pl.ds(s, 1), :]
            return carry
        lax.fori_loop(0, CHUNK, source_row, 0)
        finish([oi_scr[u] for u in range(len(units))])


def _hgrn_consts():
    t = np.arange(CHUNK)
    tri = np.stack([t[:, None] >= t[None, :], t[:, None] <= t[None, :]]).astype(np.float32)
    tri4 = np.tile(tri, (1, 1, H_HGRN))
    r = np.arange(H_HGRN * CHUNK)
    lane = np.arange(W_HGRN)
    hm = (r[:, None] // CHUNK == lane[None, :] // DK_HGRN).astype(np.float32)
    bd = (lane[:, None] // DV_HGRN == lane[None, :] // DK_HGRN).astype(np.float32)
    return jnp.asarray(tri), jnp.asarray(tri4), jnp.asarray(hm), jnp.asarray(bd)


def _hgrn(hg, lb2, s0f, s0b):
    tri, tri4, hm, bd = _hgrn_consts()

    def col(cb, reverse):
        return pl.BlockSpec((T_BLK, W_HGRN), lambda i: (_hgrn_blk(i, reverse), cb))

    def const(shape):
        return pl.BlockSpec(shape, lambda i: (0,) * len(shape))
    state = pl.BlockSpec((None, H_HGRN, DV_HGRN, DK_HGRN), lambda i: (_hgrn_seq(i), 0, 0, 0))
    return pl.pallas_call(
        _hgrn_kernel,
        grid=(N_HGRN_STEPS,),
        in_specs=[col(0, False), col(1, False), col(3, False), col(0, True), col(2, True), col(3, True),
                  const((2, W_HGRN)), state, state,
                  const(tri.shape), const(tri4.shape), const(hm.shape), const(bd.shape)],
        out_specs=[col(0, False), col(0, True), state, state],
        out_shape=[jax.ShapeDtypeStruct((T_ALL, W_HGRN), BF16)] * 2
        + [jax.ShapeDtypeStruct((N_SEQ, H_HGRN, DV_HGRN, DK_HGRN), F32)] * 2,
        scratch_shapes=[pltpu.VMEM((W_HGRN, W_HGRN), F32)] * 2
        + [pltpu.VMEM((2 * T_BLK // CHUNK, CHUNK, W_HGRN), F32)] * 4,
        compiler_params=_cparams(("arbitrary",)),
        name="hgrn",
    )(hg, hg, hg, hg, hg, hg, lb2, s0f, s0b, tri, tri4, hm, bd)


def _attn_kernel(q_ref, k_ref, v_ref, o_ref, *, n_seq=1):
    lane = lax.broadcasted_iota(jnp.int32, (1, HEAD_PAD), 1)
    lq, lk = q_ref.shape[0] // n_seq, k_ref.shape[0] // n_seq
    n_heads = q_ref.shape[1] // HEAD_PAD
    units = [(slice(b * lq, (b + 1) * lq), slice(b * lk, (b + 1) * lk), slice(h * HEAD_PAD, (h + 1) * HEAD_PAD))
             for b in range(n_seq) for h in range(n_heads)]
    s = [_dot_nt(q_ref[qr, hs], k_ref[kr, hs]) for qr, kr, hs in units]
    p = [jnp.exp2(sh - jnp.max(sh, axis=-1, keepdims=True)).astype(BF16) for sh in s]
    pv = [_dot(ph, v_ref[kr, hs]) for ph, (_, kr, hs) in zip(p, units)]
    o = [jnp.where(lane < D_V, x * (1.0 / x[:, D_V:D_V + 1]), 0.0) for x in pv]
    for b in range(n_seq):
        for hp in range(n_heads // 2):
            u = b * n_heads + 2 * hp
            o_ref[units[u][0], hp * 2 * D_V:(hp + 1) * 2 * D_V] = (
                o[u] + pltpu.roll(o[u + 1], D_V, axis=1)).astype(BF16)


def _attn_lat_kernel(q_ref, kp_ref, vp_ref, kn_ref, vn_ref, o_ref, k_scr, v_scr):
    @pl.when(pl.program_id(2) == 0)
    def _():
        k_scr[0:PAST, :] = kp_ref[...]
        k_scr[PAST:, :] = kn_ref[...]
        v_scr[0:PAST, :] = vp_ref[...]
        v_scr[PAST:, :] = vn_ref[...]
    _attn_kernel(q_ref, k_scr, v_scr, o_ref)


def _attention_ctx(q, k, v):
    rows = CTX_SEQ_PER_STEP * L_CTX
    blk = pl.BlockSpec((rows, W_HEADS), lambda b: (b, 0))
    return pl.pallas_call(
        functools.partial(_attn_kernel, n_seq=CTX_SEQ_PER_STEP),
        grid=(N_CTX_SEQ // CTX_SEQ_PER_STEP,),
        in_specs=[blk, blk, blk],
        out_specs=pl.BlockSpec((rows, W_MLA), lambda b: (b, 0)),
        out_shape=jax.ShapeDtypeStruct((T_CTX, W_MLA), BF16),
        compiler_params=_cparams(("arbitrary",)),
        name="attn_ctx",
    )(q, k, v)


def _attention_lat(q, k, v, k_past, v_past):
    grp = ATT_HEADS * HEAD_PAD
    nq = L_LAT // TQ
    q0 = T_CTX // TQ
    seq0 = T_CTX // L_LAT
    new = pl.BlockSpec((L_LAT, grp), lambda b, hp, qi: (seq0 + b, hp))
    past = pl.BlockSpec((None, PAST, grp), lambda b, hp, qi: (b, 0, hp))
    return pl.pallas_call(
        _attn_lat_kernel,
        grid=(N_LAT_SEQ, H_MLA // ATT_HEADS, nq),
        in_specs=[pl.BlockSpec((TQ, grp), lambda b, hp, qi: (q0 + b * nq + qi, hp)), past, past, new, new],
        out_specs=pl.BlockSpec((TQ, ATT_HEADS * D_V), lambda b, hp, qi: (b * nq + qi, hp)),
        out_shape=jax.ShapeDtypeStruct((T_LAT, W_MLA), BF16),
        scratch_shapes=[pltpu.VMEM((PAST + L_LAT, grp), BF16)] * 2,
        compiler_params=_cparams(("arbitrary", "arbitrary", "arbitrary")),
        name="attn_lat",
    )(q, k_past, v_past, k, v)


def _dft_tables(n):
    a = 2.0 * np.pi * np.outer(np.arange(n), np.arange(n)) / n
    return np.cos(a), np.sin(a)


def _bf16_operand(table):
    return jnp.asarray(table, F32).astype(BF16)


def _fnet_finish(re, im, cs_ref, w_ref, g_ref):
    spec = _dot(jnp.concatenate([re, im], axis=1).astype(BF16), cs_ref[...])
    return _rms_rows(_dot(spec.astype(BF16), w_ref[...]), g_ref[...])


def _fnet_channel_consts(w_bd, g_row):
    c, s = _dft_tables(C_FNET)
    eye = np.eye(G_FNET)
    return [_bf16_operand(np.concatenate([np.kron(eye, c), np.kron(eye, s)], axis=0)), w_bd, g_row]


def _fnet_channel_specs():
    zero = lambda *a: (0, 0)
    return [pl.BlockSpec((2 * W_FNET, W_FNET), zero), pl.BlockSpec((W_FNET, W_FNET), zero),
            pl.BlockSpec((1, W_FNET), zero)]


def _fnet_ctx_kernel(u_ref, m_ref, cs_ref, w_ref, g_ref, o_ref):
    p = _dot(m_ref[...], u_ref[...].astype(BF16))
    o_ref[...] = _fnet_finish(p[:L_CTX], p[L_CTX:], cs_ref, w_ref, g_ref)


def _fnet_ctx(uf, w_bd, g_row):
    c, s = _dft_tables(L_CTX)
    norm = 1.0 / np.sqrt(L_CTX * C_FNET)
    blk = pl.BlockSpec((L_CTX, W_FNET), lambda b: (b, 0))
    return pl.pallas_call(
        _fnet_ctx_kernel,
        grid=(N_CTX_SEQ,),
        in_specs=[blk, pl.BlockSpec((2 * L_CTX, L_CTX), lambda b: (0, 0))] + _fnet_channel_specs(),
        out_specs=blk,
        out_shape=jax.ShapeDtypeStruct((T_CTX, W_FNET), F32),
        compiler_params=_cparams(("arbitrary",)),
        name="fnet_ctx",
    )(uf, _bf16_operand(np.concatenate([c, -s], axis=0) * norm), *_fnet_channel_consts(w_bd, g_row))


FN_SUB = 8


def _fnet_rows_kernel(x_ref, m_ref, re_ref, im_ref):
    for s in range(FN_SUB):
        y = _dot(m_ref[s], x_ref[:, s, :].astype(BF16))
        re_ref[s] = y[:GRID_W]
        im_ref[s] = y[GRID_W:]


def _fnet_cols_kernel(re_ref, im_ref, m_ref, cs_ref, w_ref, g_ref, o_ref):
    z = [_dot(m_ref[...], jnp.concatenate([re_ref[:, s, :], im_ref[:, s, :]], axis=0).astype(BF16))
         for s in range(FN_SUB)]
    out = _fnet_finish(jnp.concatenate([zs[:GRID_W] for zs in z], axis=0),
                       jnp.concatenate([zs[GRID_W:] for zs in z], axis=0), cs_ref, w_ref, g_ref)
    for s in range(FN_SUB):
        o_ref[:, s, :] = out[s * GRID_W:(s + 1) * GRID_W]


def _fnet_lat(uf, w_bd, g_row):
    p1 = np.arange(GRID_W)[None, :, None]
    l = GRID_W * np.arange(GRID_W)[None, None, :] + np.arange(GRID_W)[:, None, None]
    ang = 2.0 * np.pi * p1 * l / L_LAT
    norm = 1.0 / np.sqrt(L_LAT * C_FNET)
    m_rows = _bf16_operand(np.concatenate([np.cos(ang), -np.sin(ang)], axis=1) * norm)
    c64, s64 = _dft_tables(GRID_W)
    m_cols = _bf16_operand(np.block([[c64, s64], [-s64, c64]]))
    x4 = uf.reshape(T_ALL // L_LAT, GRID_W, GRID_W, W_FNET)
    seq0 = T_CTX // L_LAT
    grid = (N_LAT_SEQ, GRID_W // FN_SUB)
    mid = pl.BlockSpec((None, FN_SUB, GRID_W, W_FNET), lambda b, j: (b, j, 0, 0))
    mid_shape = jax.ShapeDtypeStruct((N_LAT_SEQ, GRID_W, GRID_W, W_FNET), F32)
    re, im = pl.pallas_call(
        _fnet_rows_kernel,
        grid=grid,
        in_specs=[pl.BlockSpec((None, GRID_W, FN_SUB, W_FNET), lambda b, j: (seq0 + b, 0, j, 0)),
                  pl.BlockSpec((FN_SUB, 2 * GRID_W, GRID_W), lambda b, j: (j, 0, 0))],
        out_specs=[mid, mid],
        out_shape=[mid_shape, mid_shape],
        compiler_params=_cparams(("arbitrary", "arbitrary")),
        name="fnet_rows_lat",
    )(x4, m_rows)
    strided = pl.BlockSpec((None, GRID_W, FN_SUB, W_FNET), lambda b, j: (b, 0, j, 0))
    out = pl.pallas_call(
        _fnet_cols_kernel,
        grid=grid,
        in_specs=[strided, strided, pl.BlockSpec((2 * GRID_W, 2 * GRID_W), lambda b, j: (0, 0))]
        + _fnet_channel_specs(),
        out_specs=strided,
        out_shape=mid_shape,
        compiler_params=_cparams(("arbitrary", "arbitrary")),
        name="fnet_cols_lat",
    )(re, im, m_cols, *_fnet_channel_consts(w_bd, g_row))
    return out.reshape(T_LAT, W_FNET)


def _pad_heads(w, d_head):
    lead = w.shape[:-1]
    w = w.reshape(lead + (H_MLA, d_head))
    w = jnp.pad(w, [(0, 0)] * len(lead) + [(0, 0), (0, HEAD_PAD - d_head)])
    return w.reshape(lead + (W_HEADS,))


def _swap_rope(a):
    n = D_ROPE // 4
    parts = [jnp.zeros(a.shape[:-1] + (D_NOPE,), a.dtype)]
    for ax in range(2):
        base = D_NOPE + ax * 2 * n
        parts += [a[..., base + n:base + 2 * n], a[..., base:base + n]]
    parts.append(jnp.zeros(a.shape[:-1] + (HEAD_PAD - D_QK,), a.dtype))
    return jnp.concatenate(parts, axis=-1)


def _rope_tables(g_q, g_k):
    n_freq = D_ROPE // 4
    t = np.arange(L_LAT)
    pos = np.stack([t // GRID_W, t % GRID_W], axis=-1).astype(np.float32)
    freq = (np.float32(ROPE_THETA) ** (-np.arange(n_freq, dtype=np.float32) / n_freq)).astype(np.float32)
    ang = (pos[:, :, None] * freq).astype(np.float32)
    cos = np.ones((L_LAT, HEAD_PAD), np.float32)
    sin = np.zeros((L_LAT, HEAD_PAD), np.float32)
    for ax in range(2):
        base = D_NOPE + ax * 2 * n_freq
        c, s = np.cos(ang[:, ax, :]), np.sin(ang[:, ax, :])
        cos[:, base:base + n_freq] = c
        cos[:, base + n_freq:base + 2 * n_freq] = c
        sin[:, base:base + n_freq] = -s
        sin[:, base + n_freq:base + 2 * n_freq] = s
    cos = np.concatenate([np.ones((TM, HEAD_PAD), np.float32), cos]).reshape(1, 1 + LAT_TILES_PER_SEQ, TM, HEAD_PAD)
    sin = np.concatenate([np.zeros((TM, HEAD_PAD), np.float32), sin]).reshape(1, 1 + LAT_TILES_PER_SEQ, TM, HEAD_PAD)

    def pair(g, scale):
        g128 = jnp.pad(g, ((0, 0), (0, HEAD_PAD - D_QK))) * scale
        g_partner = _swap_rope(g128)
        return (cos * g128[:, None, None, :], sin * g_partner[:, None, None, :])
    return pair(g_q, D_QK ** -0.5 * LOG2_E) + pair(g_k, 1.0)


def kernel(x_prompt, x_sample, cache_ckv, cache_krope, state_hgrn, c, c_ctx, ada_w, ada_b, norm_g, ffn_w_gu, ffn_w_down, w_in, hgrn_lb, hgrn_norm_g, mla_q_norm_g, mla_w_q_up, mla_kv_norm_g, mla_w_kv_up, mla_qk_norm_g, mla_out_norm_g, fnet_w, fnet_norm_g, w_out):
    w_gu = ffn_w_gu.astype(BF16)
    w_down = ffn_w_down.astype(BF16)
    o = np.cumsum((0, 5 * W_HGRN, Q_RANK, KV_RANK, D_ROPE, W_FNET))
    assert o[3] == IN_MAIN
    w_in_b = w_in.astype(BF16)
    w_kr = jnp.pad(w_in[:, :, o[3]:o[4]], ((0, 0), (0, 0), (D_NOPE, HEAD_PAD - D_NOPE - D_ROPE)))
    w_tail = jnp.concatenate([w_kr, _swap_rope(w_kr), w_in[:, :, o[4]:]], axis=-1).astype(BF16)
    wq = _pad_heads(mla_w_q_up, D_QK)
    wq_sw = _swap_rope(wq.reshape(DEPTH, Q_RANK, H_MLA, HEAD_PAD)).reshape(DEPTH, Q_RANK, W_HEADS)
    wq2 = jnp.concatenate([wq, wq_sw], axis=-1).astype(BF16)
    w_kv = mla_w_kv_up.reshape(DEPTH, KV_RANK, H_MLA, D_NOPE + D_V)
    wk_arr = _pad_heads(w_kv[..., :D_NOPE].reshape(DEPTH, KV_RANK, H_MLA * D_NOPE), D_NOPE).astype(BF16)
    wv_arr = _pad_heads(w_kv[..., D_NOPE:].reshape(DEPTH, KV_RANK, H_MLA * D_V), D_V).astype(BF16)
    w_out_b = w_out.astype(BF16)
    eye_g = jnp.eye(G_FNET, dtype=F32)
    fnet_bd = jnp.einsum("lgcd,gh->lgchd", fnet_w, eye_g).reshape(DEPTH, W_FNET, W_FNET).astype(BF16)
    tabs = _rope_tables(mla_qk_norm_g[:, 0], mla_qk_norm_g[:, 1])
    lbs = jnp.cumsum(jax.nn.softmax(hgrn_lb.astype(F32), axis=0), axis=0)
    lbs = lbs - lbs[:1]

    cond8 = jnp.zeros((8, D), F32).at[0].set(c_ctx).at[1:1 + N_LAT_SEQ].set(c)
    mods = _mods(cond8, ada_w, ada_b).reshape(DEPTH, 8, N_MOD, D)

    x = (x_prompt.reshape(T_CTX, D), x_sample.reshape(T_LAT, D))
    ckv_out, kr_out, st_out = [], [], []
    zero_states = jnp.zeros((N_CTX_SEQ, H_HGRN, DV_HGRN, DK_HGRN), F32)
    for l in range(DEPTH):
        x = _ffn(x, mods, norm_g, w_gu, w_down, l, 0)
        hg, ckvn, kr128, uf, q, k, v = _inproj(x, mods, norm_g, w_in_b, w_tail, mla_q_norm_g, mla_kv_norm_g, wq2, wk_arr,
                                               wv_arr, tabs, l)

        s0 = [jnp.concatenate([zero_states, jnp.swapaxes(state_hgrn[:, l, d], -1, -2)], axis=0) for d in range(2)]
        h_fwd, h_bwd, s_f, s_b = _hgrn(hg, lbs[l], s0[0], s0[1])
        gn_row = jnp.tile(hgrn_norm_g[l], H_HGRN).reshape(1, W_HGRN)
        st_out.append(jnp.swapaxes(jnp.stack([s_f[:N_CTX_SEQ], s_b[:N_CTX_SEQ]], axis=1), -1, -2))

        cache_kr = jnp.pad(cache_krope[:, l], ((0, 0), (0, 0), (D_NOPE, HEAD_PAD - D_NOPE - D_ROPE)))
        k_past, v_past = _kv_cache(cache_ckv[:, l], cache_kr, _swap_rope(cache_kr), wk_arr, wv_arr, tabs, l)
        att_ctx = _attention_ctx(q, k, v)
        att_lat = _attention_lat(q, k, v, k_past, v_past)

        fn_gain = fnet_norm_g[l].reshape(1, W_FNET)
        fn_ctx = _fnet_ctx(uf, fnet_bd[l], fn_gain)
        fn_lat = _fnet_lat(uf, fnet_bd[l], fn_gain)

        x = _outproj(x, mods, h_fwd, h_bwd, hg, gn_row, att_ctx, att_lat, fn_ctx, fn_lat, mla_out_norm_g, w_out_b, l)
        if l == DEPTH - 1:
            y_prompt = _ffn(x, mods, norm_g, w_gu, w_down, l, 1, n_rows=T_CTX)
            y_sample = _ffn(x, mods, norm_g, w_gu, w_down, l, 1, src_row0=T_CTX, n_rows=T_LAT)
        else:
            x = _ffn(x, mods, norm_g, w_gu, w_down, l, 1)

        ckv_out.append(ckvn[:T_CTX].reshape(N_CTX_SEQ, L_CTX, KV_RANK))
        kr_out.append(kr128[:T_CTX, D_NOPE:D_NOPE + D_ROPE].reshape(N_CTX_SEQ, L_CTX, D_ROPE))

    return (y_prompt.reshape(N_CTX_SEQ, L_CTX, D), y_sample.reshape(N_LAT_SEQ, L_LAT, D),
            jnp.stack(ckv_out, axis=1), jnp.stack(kr_out, axis=1), jnp.stack(st_out, axis=1))
```

```python
import functools

import numpy as np
import jax
import jax.numpy as jnp
from jax import lax
from jax.experimental import pallas as pl
from jax.experimental.pallas import tpu as pltpu

F32 = jnp.float32
BF16 = jnp.bfloat16

D = 1024
N_CTX_SEQ, L_CTX = 32, 256
N_LAT_SEQ, L_LAT = 2, 4096
T_CTX = N_CTX_SEQ * L_CTX
T_LAT = N_LAT_SEQ * L_LAT
T_ALL = T_CTX + T_LAT
DEPTH = 2
PAST = 512
GRID_W = 64
N_MOD = 9
EPS = 1e-6

H_HGRN, DK_HGRN, DV_HGRN = 4, 64, 64
W_HGRN = H_HGRN * DV_HGRN
CHUNK = 32
HGRN_SAFE_LOG_DECAY = 60.0
H_MLA, Q_RANK, KV_RANK = 8, 384, 256
D_NOPE, D_ROPE, D_V = 64, 32, 64
D_QK = D_NOPE + D_ROPE
HEAD_PAD = 128
W_HEADS = H_MLA * HEAD_PAD
W_MLA = H_MLA * D_V
G_FNET, C_FNET = 4, 64
W_FNET = G_FNET * C_FNET
D_FF = 2816
FF_CHUNK = 256
ROPE_THETA = 10000.0
LOG2_E = 1.4426950408889634

IN_HG = 5 * W_HGRN
IN_MAIN = IN_HG + Q_RANK + KV_RANK
IN_TAIL = 2 * HEAD_PAD + W_FNET

TM = 512
FFN_TM = 512
N_TILES = T_ALL // TM
N_CTX_TILES = T_CTX // TM
LAT_TILES_PER_SEQ = L_LAT // TM
T_BLK = 256
TQ = 512
ATT_HEADS = 2
CTX_SEQ_PER_STEP = 4

VMEM_LIMIT = 56 * 1024 * 1024


def _cparams(sem):
    return pltpu.CompilerParams(dimension_semantics=sem, vmem_limit_bytes=VMEM_LIMIT)


def _tile_group(i, tm=TM):
    return (i >= T_CTX // tm).astype(jnp.int32) + (i >= (T_CTX + L_LAT) // tm).astype(jnp.int32)


def _silu(x):
    return x * (1.0 / (1.0 + jnp.exp(-x)))


def _rms_rows(x, g):
    return x * lax.rsqrt(jnp.mean(x * x, axis=-1, keepdims=True) + EPS) * g


def _dot(a, b):
    return jnp.dot(a, b, preferred_element_type=F32)


def _dot_nt(a, b):
    return lax.dot_general(a, b, (((1,), (1,)), ((), ())), preferred_element_type=F32)


def _dot_tn(a, b):
    return lax.dot_general(a, b, (((0,), (0,)), ((), ())), preferred_element_type=F32)


def _step_tile(i):
    return i


def _row_spec(width, col_block=0, tile=_step_tile, tm=TM):
    return pl.BlockSpec((tm, width), lambda i: (tile(i), col_block))


def _ctx_row_spec(width, tile=_step_tile, tm=TM):
    return pl.BlockSpec((tm, width), lambda i: (jnp.minimum(tile(i), T_CTX // tm - 1), 0))


def _lat_row_spec(width, tile=_step_tile, tm=TM):
    return pl.BlockSpec((tm, width), lambda i: (jnp.maximum(tile(i) - T_CTX // tm, 0), 0))


def _mod_spec(layer, tile=_step_tile, tm=TM):
    return pl.BlockSpec((None, None, N_MOD, D), lambda i: (layer, _tile_group(tile(i), tm), 0, 0))


def _resident(shape, index_map):
    return pl.BlockSpec(shape, index_map, pipeline_mode=pl.Buffered(1))


def _mods_kernel(c_ref, w_ref, b_ref, o_ref):
    a = _silu(c_ref[...]).astype(BF16)
    o_ref[...] = _dot(a, w_ref[...].astype(BF16)) + b_ref[...]


def _mods(cond8, ada_w, ada_b):
    tn = 1024
    return pl.pallas_call(
        _mods_kernel,
        grid=(DEPTH, N_MOD * D // tn),
        in_specs=[
            pl.BlockSpec((8, D), lambda l, j: (0, 0)),
            pl.BlockSpec((None, D, tn), lambda l, j: (l, 0, j)),
            pl.BlockSpec((None, 1, tn), lambda l, j: (l, 0, j)),
        ],
        out_specs=pl.BlockSpec((None, 8, tn), lambda l, j: (l, 0, j)),
        out_shape=jax.ShapeDtypeStruct((DEPTH, 8, N_MOD * D), F32),
        compiler_params=_cparams(("arbitrary", "arbitrary")),
        name="ada_mods",
    )(cond8, ada_w, ada_b.reshape(DEPTH, 1, N_MOD * D))


def _mixer_out(x, mod_ref, ctx, hf_ref, hb_ref, hgate_ref, gn_ref, ones_ref, ac_ref, al_ref, fc_ref, fl_ref, ga_ref,
               w_ref):
    gate = mod_ref[5:6, :]
    oh = hf_ref[...].astype(F32) + hb_ref[...].astype(F32)
    ms = _dot((oh * oh).astype(BF16), ones_ref[...])
    oh = oh * lax.rsqrt(ms + EPS) * gn_ref[...] * _silu(hgate_ref[...])
    oa = _rms_rows(jnp.where(ctx, ac_ref[...], al_ref[...]).astype(F32), ga_ref[...])
    of = jnp.where(ctx, fc_ref[...], fl_ref[...])
    o = (_dot(oh.astype(BF16), w_ref[0:W_HGRN, :])
         + _dot(oa.astype(BF16), w_ref[W_HGRN:W_HGRN + W_MLA, :])
         + _dot(of.astype(BF16), w_ref[W_HGRN + W_MLA:, :]))
    return x + gate * o


def _outproj_kernel(x_ref, mod_ref, *rest):
    o_ref = rest[-1]
    o_ref[...] = _mixer_out(x_ref[...], mod_ref, pl.program_id(0) < N_CTX_TILES, *rest[:-1])


def _outproj(x, mods, h_fwd, h_bwd, hg, gn_row, att_ctx, att_lat, fn_ctx, fn_lat, g_att, w_out, layer):
    lane = np.arange(W_HGRN)
    ones_bd = jnp.asarray((lane[:, None] // DV_HGRN == lane[None, :] // DV_HGRN).astype(np.float32) / DV_HGRN, BF16)
    return pl.pallas_call(
        _outproj_kernel,
        grid=(N_TILES,),
        in_specs=[_row_spec(D), _mod_spec(layer), _row_spec(W_HGRN), _row_spec(W_HGRN), _row_spec(W_HGRN, 4),
                  pl.BlockSpec((1, W_HGRN), lambda i: (0, 0)), pl.BlockSpec((W_HGRN, W_HGRN), lambda i: (0, 0)),
                  _ctx_row_spec(W_MLA), _lat_row_spec(W_MLA), _ctx_row_spec(W_FNET), _lat_row_spec(W_FNET),
                  pl.BlockSpec((None, 1, W_MLA), lambda i: (layer, 0, 0)),
                  _resident((None, D, D), lambda i: (layer, 0, 0))],
        out_specs=_row_spec(D),
        out_shape=jax.ShapeDtypeStruct((T_ALL, D), F32),
        compiler_params=_cparams(("arbitrary",)),
        name=f"outproj_l{layer}",
    )(x, mods, h_fwd, h_bwd, hg, gn_row, ones_bd, att_ctx, att_lat, fn_ctx, fn_lat,
      g_att.reshape(DEPTH, 1, W_MLA), w_out)


def _ffn_kernel(*refs, mi, two_sources):
    if two_sources:
        xc_ref, xl_ref, mod_ref, g_ref, wg_ref, wu_ref, wd_ref, o_ref = refs
        x = jnp.where(pl.program_id(0) < T_CTX // FFN_TM, xc_ref[...], xl_ref[...])
    else:
        x_ref, mod_ref, g_ref, wg_ref, wu_ref, wd_ref, o_ref = refs
        x = x_ref[...]
    shift = mod_ref[mi:mi + 1, :]
    scale = mod_ref[mi + 1:mi + 2, :]
    gate = mod_ref[mi + 2:mi + 3, :]
    hb = (_rms_rows(x, g_ref[...]) * (1.0 + scale) + shift).astype(BF16)
    acc = jnp.zeros(x.shape, F32)
    for j in range(D_FF // FF_CHUNK):
        cs = slice(j * FF_CHUNK, (j + 1) * FF_CHUNK)
        a = _silu(_dot(hb, wg_ref[:, cs])) * _dot(hb, wu_ref[:, cs])
        acc = acc + _dot(a.astype(BF16), wd_ref[cs, :])
    o_ref[...] = x + 0.5 * gate * acc


def _ffn(xs, mods, norm_g, w_gu, w_down, layer, which, *, src_row0=0, n_rows=T_ALL):
    mi = 0 if which == 0 else 6
    gi = 0 if which == 0 else 2
    tm = FFN_TM

    def tile(i):
        return i + src_row0 // tm
    two = isinstance(xs, tuple)
    if two:
        in_specs = [_ctx_row_spec(D, tm=tm), _lat_row_spec(D, tm=tm)]
        args = list(xs)
    else:
        in_specs = [_row_spec(D, tile=tile, tm=tm)]
        args = [xs]
    in_specs += [
        _mod_spec(layer, tile, tm),
        pl.BlockSpec((None, None, 1, D), lambda i: (layer, gi, 0, 0)),
        _resident((None, None, D, D_FF), lambda i: (layer, which, 0, 0)),
        _resident((None, None, D, D_FF), lambda i: (layer, which, 0, 1)),
        _resident((None, None, D_FF, D), lambda i: (layer, which, 0, 0)),
    ]
    args += [mods, norm_g.reshape(DEPTH, 3, 1, D), w_gu, w_gu, w_down]
    return pl.pallas_call(
        functools.partial(_ffn_kernel, mi=mi, two_sources=two),
        grid=(n_rows // tm,),
        in_specs=in_specs,
        out_specs=pl.BlockSpec((tm, D), lambda i: (i, 0)),
        out_shape=jax.ShapeDtypeStruct((n_rows, D), F32),
        compiler_params=_cparams(("arbitrary",)),
        name=f"ffn_l{layer}_{which}",
    )(*args)


def _mla_heads(qa, ka, va, kr, kr_sw, tabs, q_ref, k_ref, v_ref):
    cq, sq, ck, sk = tabs
    lane = lax.broadcasted_iota(jnp.int32, (1, W_HEADS), 1) % HEAD_PAD
    v_ref[...] = jnp.where(lane == D_V, 1.0, va).astype(BF16)
    k_rot = kr_sw * sk
    heads = [slice(h * HEAD_PAD, (h + 1) * HEAD_PAD) for h in range(H_MLA)]

    def inv_rms(xs):
        return [lax.rsqrt(jnp.sum(x * x, axis=-1, keepdims=True) * (1.0 / D_QK) + EPS) for x in xs]
    if qa is not None:
        q = [qa[:, hs] for hs in heads]
        q_sw = [qa[:, W_HEADS + hs.start:W_HEADS + hs.stop] for hs in heads]
        for hs, x, x_sw, rs in zip(heads, q, q_sw, inv_rms(q)):
            q_ref[:, hs] = (rs * (x * cq + x_sw * sq)).astype(BF16)
    k = [ka[:, hs] + kr for hs in heads]
    for hs, x, rs in zip(heads, k, inv_rms(k)):
        k_ref[:, hs] = (rs * (x * ck + k_rot)).astype(BF16)


def _inproj_kernel(x_ref, mod_ref, g_ref, w_ref, wt_ref, gq_ref, gkv_ref, wq_ref, wk_ref, wv_ref, cq_ref, sq_ref,
                   ck_ref, sk_ref, hg_ref, ckv_ref, kr_ref, uf_ref, q_ref, k_ref, v_ref):
    x = x_ref[...]
    shift = mod_ref[3:4, :]
    scale = mod_ref[4:5, :]
    hb = (_rms_rows(x, g_ref[...]) * (1.0 + scale) + shift).astype(BF16)
    ul = _dot(hb, w_ref[:, IN_HG:])
    cqn = _rms_rows(ul[:, :Q_RANK], gq_ref[...])
    ckvn = _rms_rows(ul[:, Q_RANK:], gkv_ref[...])
    ckv_ref[...] = ckvn
    ut = _dot(hb, wt_ref[...])
    kr = ut[:, 0:HEAD_PAD]
    kr_ref[...] = kr
    kr_sw = ut[:, HEAD_PAD:2 * HEAD_PAD]
    uf_ref[...] = ut[:, 2 * HEAD_PAD:]
    hg_ref[...] = _dot(hb, w_ref[:, :IN_HG])
    cb = ckvn.astype(BF16)
    _mla_heads(_dot(cqn.astype(BF16), wq_ref[...]), _dot(cb, wk_ref[...]), _dot(cb, wv_ref[...]), kr, kr_sw,
               (cq_ref[...], sq_ref[...], ck_ref[...], sk_ref[...]), q_ref, k_ref, v_ref)


def _rope_block(i):
    return jnp.where(i < N_CTX_TILES, 0, 1 + (i - N_CTX_TILES) % LAT_TILES_PER_SEQ)


def _inproj(x, mods, norm_g, w_in_b, w_tail, gq, gkv, wq2, wk_arr, wv_arr, tabs, layer):
    tab = pl.BlockSpec((None, None, TM, HEAD_PAD), lambda i: (layer, _rope_block(i), 0, 0))
    f32_widths = (IN_HG, KV_RANK, HEAD_PAD, W_FNET)
    return pl.pallas_call(
        _inproj_kernel,
        grid=(N_TILES,),
        in_specs=[
            _row_spec(D),
            _mod_spec(layer),
            pl.BlockSpec((None, None, 1, D), lambda i: (layer, 1, 0, 0)),
            _resident((None, D, IN_MAIN), lambda i: (layer, 0, 0)),
            _resident((None, D, IN_TAIL), lambda i: (layer, 0, 0)),
            pl.BlockSpec((None, 1, Q_RANK), lambda i: (layer, 0, 0)),
            pl.BlockSpec((None, 1, KV_RANK), lambda i: (layer, 0, 0)),
            _resident((None, Q_RANK, 2 * W_HEADS), lambda i: (layer, 0, 0)),
            _resident((None, KV_RANK, W_HEADS), lambda i: (layer, 0, 0)),
            _resident((None, KV_RANK, W_HEADS), lambda i: (layer, 0, 0)),
            tab, tab, tab, tab,
        ],
        out_specs=[_row_spec(w) for w in f32_widths] + [_row_spec(W_HEADS)] * 3,
        out_shape=[jax.ShapeDtypeStruct((T_ALL, w), F32) for w in f32_widths]
        + [jax.ShapeDtypeStruct((T_ALL, W_HEADS), BF16)] * 3,
        compiler_params=_cparams(("arbitrary",)),
        name=f"inproj_l{layer}",
    )(x, mods, norm_g.reshape(DEPTH, 3, 1, D), w_in_b, w_tail, gq.reshape(DEPTH, 1, Q_RANK),
      gkv.reshape(DEPTH, 1, KV_RANK), wq2, wk_arr, wv_arr, *tabs)


def _kv_cache_kernel(ckv_ref, kr_ref, krsw_ref, wk_ref, wv_ref, ck_ref, sk_ref, k_ref, v_ref):
    cb = ckv_ref[...].astype(BF16)
    _mla_heads(None, _dot(cb, wk_ref[...]), _dot(cb, wv_ref[...]), kr_ref[...], krsw_ref[...],
               (None, None, ck_ref[...], sk_ref[...]), None, k_ref, v_ref)


def _kv_cache(cache_ckv_l, cache_kr, cache_kr_sw, wk_arr, wv_arr, tabs, layer):
    assert PAST == TM
    blk = lambda w: pl.BlockSpec((None, PAST, w), lambda b: (b, 0, 0))
    tab = pl.BlockSpec((None, None, TM, HEAD_PAD), lambda b: (layer, 0, 0, 0))
    return pl.pallas_call(
        _kv_cache_kernel,
        grid=(N_LAT_SEQ,),
        in_specs=[blk(KV_RANK), blk(HEAD_PAD), blk(HEAD_PAD),
                  _resident((None, KV_RANK, W_HEADS), lambda b: (layer, 0, 0)),
                  _resident((None, KV_RANK, W_HEADS), lambda b: (layer, 0, 0)), tab, tab],
        out_specs=[blk(W_HEADS)] * 2,
        out_shape=[jax.ShapeDtypeStruct((N_LAT_SEQ, PAST, W_HEADS), BF16)] * 2,
        compiler_params=_cparams(("arbitrary",)),
        name=f"mla_kv_cache_l{layer}",
    )(cache_ckv_l, cache_kr, cache_kr_sw, wk_arr, wv_arr, tabs[2], tabs[3])


N_SEQ = N_CTX_SEQ + N_LAT_SEQ
BLK_PER_LAT = L_LAT // T_BLK
N_HGRN_STEPS = N_CTX_SEQ + N_LAT_SEQ * BLK_PER_LAT


def _hgrn_seq(i):
    return jnp.where(i < N_CTX_SEQ, i, N_CTX_SEQ + (i - N_CTX_SEQ) // BLK_PER_LAT)


def _hgrn_blk(i, reverse):
    j = (i - N_CTX_SEQ) % BLK_PER_LAT
    if reverse:
        j = BLK_PER_LAT - 1 - j
    lat = N_CTX_SEQ + ((i - N_CTX_SEQ) // BLK_PER_LAT) * BLK_PER_LAT + j
    return jnp.where(i < N_CTX_SEQ, i, lat)


def _hgrn_kernel(qf_ref, ff_ref, vf_ref, qb_ref, fb_ref, vb_ref, lb_ref, s0f_ref, s0b_ref,
                 tri_ref, tri4_ref, hm_ref, bd_ref, of_ref, ob_ref, sf_ref, sb_ref, stf_scr, stb_scr,
                 ks_scr, bs_scr, vs_scr, oi_scr):
    i = pl.program_id(0)
    first = jnp.logical_or(i < N_CTX_SEQ, (i - N_CTX_SEQ) % BLK_PER_LAT == 0)
    heads = [slice(h * DK_HGRN, (h + 1) * DK_HGRN) for h in range(H_HGRN)]

    @pl.when(first)
    def _():
        for s0_ref, scr in ((s0f_ref, stf_scr), (s0b_ref, stb_scr)):
            scr[...] = jnp.zeros(scr.shape, F32)
            for h, hs in enumerate(heads):
                scr[hs, hs] = s0_ref[h]

    lb = lb_ref[...]
    loglb = jnp.log(lb)
    log1mlb = jnp.log(1.0 - lb)
    n_chunks = T_BLK // CHUNK
    dirs = ((qf_ref, ff_ref, vf_ref, of_ref, stf_scr), (qb_ref, fb_ref, vb_ref, ob_ref, stb_scr))
    units = [(d, c if d == 0 else n_chunks - 1 - c) for c in range(n_chunks) for d in (0, 1)]
    rows = [slice(cc * CHUNK, (cc + 1) * CHUNK) for _, cc in units]
    end_row = (CHUNK - 1, 0)
    mid_row = (CHUNK // 2 - 1, CHUNK // 2)
    hm, bd = hm_ref[...], bd_ref[...]

    q = [_silu(dirs[d][0][r, :]) for (d, _), r in zip(units, rows)]
    v = [dirs[d][2][r, :] for (d, _), r in zip(units, rows)]
    g = []
    for (d, _), r in zip(units, rows):
        x = dirs[d][1][r, :]
        y = log1mlb[d:d + 1] + (jnp.minimum(x, 0.0) - jnp.log(1.0 + jnp.exp(-jnp.abs(x))))
        g.append(jnp.maximum(loglb[d:d + 1], y) + jnp.log(1.0 + jnp.exp(-jnp.abs(loglb[d:d + 1] - y))))
    kk = [1.0 - jnp.exp(gu) for gu in g]
    g_hi = [gu.astype(BF16) for gu in g]
    g_lo = [(gu - gh.astype(F32)).astype(BF16) for gu, gh in zip(g, g_hi)]
    b = [_dot(tri_ref[d], gh) + _dot(tri_ref[d], gl) for (d, _), gh, gl in zip(units, g_hi, g_lo)]
    b_end = [bu[end_row[d]:end_row[d] + 1, :] for (d, _), bu in zip(units, b)]

    def finish(o_intra):
        q_in = [(qu * jnp.exp(bu)).astype(BF16) for qu, bu in zip(q, b)]
        k_e = [(ku * jnp.exp(be - bu)).astype(BF16) for ku, bu, be in zip(kk, b, b_end)]
        decay = [jnp.exp(be) for be in b_end]
        kv = [_dot_tn(vu.astype(BF16), ku) * bd for vu, ku in zip(v, k_e)]
        st = [stf_scr[...], stb_scr[...]]
        for u, (d, _) in enumerate(units):
            dirs[d][3][rows[u], :] = (o_intra[u] + _dot_nt(q_in[u], st[d].astype(BF16))).astype(BF16)
            st[d] = st[d] * decay[u] + kv[u]
        stf_scr[...] = st[0]
        stb_scr[...] = st[1]
        for h, hs in enumerate(heads):
            sf_ref[h] = st[0][hs, hs]
            sb_ref[h] = st[1][hs, hs]

    safe = jnp.max(jnp.concatenate([jnp.abs(be) for be in b_end], axis=0)) <= HGRN_SAFE_LOG_DECAY

    @pl.when(safe)
    def _():
        b_mid = [bu[mid_row[d]:mid_row[d] + 1, :] for (d, _), bu in zip(units, b)]
        q_t = [(qu * jnp.exp(bu - bm)).astype(BF16) for qu, bu, bm in zip(q, b, b_mid)]
        k_t = [ku * jnp.exp(bm - bu) for ku, bu, bm in zip(kk, b, b_mid)]
        k_bd = [jnp.concatenate([ku.astype(BF16)] * H_HGRN, axis=0) * hm for ku in k_t]
        v_bd = [jnp.concatenate([vu.astype(BF16)] * H_HGRN, axis=0) * hm for vu in v]
        sc = [(_dot_nt(qu, ku) * tri4_ref[d]).astype(BF16) for (d, _), qu, ku in zip(units, q_t, k_bd)]
        finish([_dot(su, vu) for su, vu in zip(sc, v_bd)])

    @pl.when(jnp.logical_not(safe))
    def _():
        t = lax.broadcasted_iota(jnp.int32, (CHUNK, 1), 0)
        head_sum = bd.astype(BF16)
        for u in range(len(units)):
            ks_scr[u] = kk[u]
            bs_scr[u] = b[u]
            vs_scr[u] = v[u]
            oi_scr[u] = jnp.zeros((CHUNK, W_HGRN), F32)

        def source_row(s, carry):
            for u, (d, _) in enumerate(units):
                seen = (t >= s) if d == 0 else (t <= s)
                e = jnp.exp(jnp.where(seen, b[u] - bs_scr[u, pl.ds(s, 1), :], -jnp.inf))
                p = (q[u] * ks_scr[u, pl.ds(s, 1), :] * e).astype(BF16)
                oi_scr[u] += _dot(p, head_sum) * vs_scr[u, pl.ds(s, 1), :]
            return carry
        lax.fori_loop(0, CHUNK, source_row, 0)
        finish([oi_scr[u] for u in range(len(units))])


def _hgrn_consts():
    t = np.arange(CHUNK)
    tri = np.stack([t[:, None] >= t[None, :], t[:, None] <= t[None, :]]).astype(np.float32)
    tri4 = np.tile(tri, (1, 1, H_HGRN))
    r = np.arange(H_HGRN * CHUNK)
    lane = np.arange(W_HGRN)
    hm = (r[:, None] // CHUNK == lane[None, :] // DK_HGRN).astype(np.float32)
    bd = (lane[:, None] // DV_HGRN == lane[None, :] // DK_HGRN).astype(np.float32)
    return jnp.asarray(tri, BF16), jnp.asarray(tri4), jnp.asarray(hm, BF16), jnp.asarray(bd)


def _hgrn(hg, lb2, s0f, s0b):
    tri, tri4, hm, bd = _hgrn_consts()

    def col(cb, reverse):
        return pl.BlockSpec((T_BLK, W_HGRN), lambda i: (_hgrn_blk(i, reverse), cb))

    def const(shape):
        return pl.BlockSpec(shape, lambda i: (0,) * len(shape))
    state = pl.BlockSpec((None, H_HGRN, DV_HGRN, DK_HGRN), lambda i: (_hgrn_seq(i), 0, 0, 0))
    return pl.pallas_call(
        _hgrn_kernel,
        grid=(N_HGRN_STEPS,),
        in_specs=[col(0, False), col(1, False), col(3, False), col(0, True), col(2, True), col(3, True),
                  const((2, W_HGRN)), state, state,
                  const(tri.shape), const(tri4.shape), const(hm.shape), const(bd.shape)],
        out_specs=[col(0, False), col(0, True), state, state],
        out_shape=[jax.ShapeDtypeStruct((T_ALL, W_HGRN), BF16)] * 2
        + [jax.ShapeDtypeStruct((N_SEQ, H_HGRN, DV_HGRN, DK_HGRN), F32)] * 2,
        scratch_shapes=[pltpu.VMEM((W_HGRN, W_HGRN), F32)] * 2
        + [pltpu.VMEM((2 * T_BLK // CHUNK, CHUNK, W_HGRN), F32)] * 4,
        compiler_params=_cparams(("arbitrary",)),
        name="hgrn",
    )(hg, hg, hg, hg, hg, hg, lb2, s0f, s0b, tri, tri4, hm, bd)


def _attn_kernel(q_ref, k_ref, v_ref, o_ref, *, n_seq=1):
    lane = lax.broadcasted_iota(jnp.int32, (1, HEAD_PAD), 1)
    lq, lk = q_ref.shape[0] // n_seq, k_ref.shape[0] // n_seq
    n_heads = q_ref.shape[1] // HEAD_PAD
    units = [(slice(b * lq, (b + 1) * lq), slice(b * lk, (b + 1) * lk), slice(h * HEAD_PAD, (h + 1) * HEAD_PAD))
             for b in range(n_seq) for h in range(n_heads)]
    s = [_dot_nt(q_ref[qr, hs], k_ref[kr, hs]) for qr, kr, hs in units]
    p = [jnp.exp2(sh - jnp.max(sh, axis=-1, keepdims=True)).astype(BF16) for sh in s]
    pv = [_dot(ph, v_ref[kr, hs]) for ph, (_, kr, hs) in zip(p, units)]
    o = [jnp.where(lane < D_V, x * (1.0 / x[:, D_V:D_V + 1]), 0.0) for x in pv]
    for b in range(n_seq):
        for hp in range(n_heads // 2):
            u = b * n_heads + 2 * hp
            o_ref[units[u][0], hp * 2 * D_V:(hp + 1) * 2 * D_V] = (
                o[u] + pltpu.roll(o[u + 1], D_V, axis=1)).astype(BF16)


def _attn_lat_kernel(q_ref, kp_ref, vp_ref, kn_ref, vn_ref, o_ref, k_scr, v_scr):
    @pl.when(pl.program_id(2) == 0)
    def _():
        k_scr[0:PAST, :] = kp_ref[...]
        k_scr[PAST:, :] = kn_ref[...]
        v_scr[0:PAST, :] = vp_ref[...]
        v_scr[PAST:, :] = vn_ref[...]
    _attn_kernel(q_ref, k_scr, v_scr, o_ref)


def _attention_ctx(q, k, v):
    rows = CTX_SEQ_PER_STEP * L_CTX
    blk = pl.BlockSpec((rows, W_HEADS), lambda b: (b, 0))
    return pl.pallas_call(
        functools.partial(_attn_kernel, n_seq=CTX_SEQ_PER_STEP),
        grid=(N_CTX_SEQ // CTX_SEQ_PER_STEP,),
        in_specs=[blk, blk, blk],
        out_specs=pl.BlockSpec((rows, W_MLA), lambda b: (b, 0)),
        out_shape=jax.ShapeDtypeStruct((T_CTX, W_MLA), BF16),
        compiler_params=_cparams(("arbitrary",)),
        name="attn_ctx",
    )(q, k, v)


def _attention_lat(q, k, v, k_past, v_past):
    grp = ATT_HEADS * HEAD_PAD
    nq = L_LAT // TQ
    q0 = T_CTX // TQ
    seq0 = T_CTX // L_LAT
    new = pl.BlockSpec((L_LAT, grp), lambda b, hp, qi: (seq0 + b, hp))
    past = pl.BlockSpec((None, PAST, grp), lambda b, hp, qi: (b, 0, hp))
    return pl.pallas_call(
        _attn_lat_kernel,
        grid=(N_LAT_SEQ, H_MLA // ATT_HEADS, nq),
        in_specs=[pl.BlockSpec((TQ, grp), lambda b, hp, qi: (q0 + b * nq + qi, hp)), past, past, new, new],
        out_specs=pl.BlockSpec((TQ, ATT_HEADS * D_V), lambda b, hp, qi: (b * nq + qi, hp)),
        out_shape=jax.ShapeDtypeStruct((T_LAT, W_MLA), BF16),
        scratch_shapes=[pltpu.VMEM((PAST + L_LAT, grp), BF16)] * 2,
        compiler_params=_cparams(("arbitrary", "arbitrary", "arbitrary")),
        name="attn_lat",
    )(q, k_past, v_past, k, v)


def _dft_tables(n):
    a = 2.0 * np.pi * np.outer(np.arange(n), np.arange(n)) / n
    return np.cos(a), np.sin(a)


def _bf16_operand(table):
    return jnp.asarray(table, F32).astype(BF16)


def _fnet_finish(re, im, cs_ref, w_ref, g_ref):
    spec = _dot(jnp.concatenate([re, im], axis=1).astype(BF16), cs_ref[...])
    return _rms_rows(_dot(spec.astype(BF16), w_ref[...]), g_ref[...])


def _fnet_channel_consts(w_bd, g_row):
    c, s = _dft_tables(C_FNET)
    eye = np.eye(G_FNET)
    return [_bf16_operand(np.concatenate([np.kron(eye, c), np.kron(eye, s)], axis=0)), w_bd, g_row]


def _fnet_channel_specs():
    zero = lambda *a: (0, 0)
    return [pl.BlockSpec((2 * W_FNET, W_FNET), zero), pl.BlockSpec((W_FNET, W_FNET), zero),
            pl.BlockSpec((1, W_FNET), zero)]


def _fnet_ctx_kernel(u_ref, m_ref, cs_ref, w_ref, g_ref, o_ref):
    p = [_dot(m_ref[...], u_ref[b * L_CTX:(b + 1) * L_CTX, :].astype(BF16)) for b in range(CTX_SEQ_PER_STEP)]
    o_ref[...] = _fnet_finish(jnp.concatenate([pb[:L_CTX] for pb in p], axis=0),
                              jnp.concatenate([pb[L_CTX:] for pb in p], axis=0), cs_ref, w_ref, g_ref)


def _fnet_ctx(uf, w_bd, g_row):
    c, s = _dft_tables(L_CTX)
    norm = 1.0 / np.sqrt(L_CTX * C_FNET)
    blk = pl.BlockSpec((CTX_SEQ_PER_STEP * L_CTX, W_FNET), lambda b: (b, 0))
    return pl.pallas_call(
        _fnet_ctx_kernel,
        grid=(N_CTX_SEQ // CTX_SEQ_PER_STEP,),
        in_specs=[blk, pl.BlockSpec((2 * L_CTX, L_CTX), lambda b: (0, 0))] + _fnet_channel_specs(),
        out_specs=blk,
        out_shape=jax.ShapeDtypeStruct((T_CTX, W_FNET), F32),
        compiler_params=_cparams(("arbitrary",)),
        name="fnet_ctx",
    )(uf, _bf16_operand(np.concatenate([c, -s], axis=0) * norm), *_fnet_channel_consts(w_bd, g_row))


FN_SUB = 8


def _fnet_rows_kernel(x_ref, m_ref, re_ref, im_ref):
    for s in range(FN_SUB):
        y = _dot(m_ref[s], x_ref[:, s, :].astype(BF16))
        re_ref[s] = y[:GRID_W]
        im_ref[s] = y[GRID_W:]


def _fnet_cols_kernel(re_ref, im_ref, m_ref, cs_ref, w_ref, g_ref, o_ref):
    z = [_dot(m_ref[...], jnp.concatenate([re_ref[:, s, :], im_ref[:, s, :]], axis=0).astype(BF16))
         for s in range(FN_SUB)]
    out = _fnet_finish(jnp.concatenate([zs[:GRID_W] for zs in z], axis=0),
                       jnp.concatenate([zs[GRID_W:] for zs in z], axis=0), cs_ref, w_ref, g_ref)
    for s in range(FN_SUB):
        o_ref[:, s, :] = out[s * GRID_W:(s + 1) * GRID_W]


def _fnet_lat(uf, w_bd, g_row):
    p1 = np.arange(GRID_W)[None, :, None]
    l = GRID_W * np.arange(GRID_W)[None, None, :] + np.arange(GRID_W)[:, None, None]
    ang = 2.0 * np.pi * p1 * l / L_LAT
    norm = 1.0 / np.sqrt(L_LAT * C_FNET)
    m_rows = _bf16_operand(np.concatenate([np.cos(ang), -np.sin(ang)], axis=1) * norm)
    c64, s64 = _dft_tables(GRID_W)
    m_cols = _bf16_operand(np.block([[c64, s64], [-s64, c64]]))
    x4 = uf.reshape(T_ALL // L_LAT, GRID_W, GRID_W, W_FNET)
    seq0 = T_CTX // L_LAT
    grid = (N_LAT_SEQ, GRID_W // FN_SUB)
    mid = pl.BlockSpec((None, FN_SUB, GRID_W, W_FNET), lambda b, j: (b, j, 0, 0))
    mid_shape = jax.ShapeDtypeStruct((N_LAT_SEQ, GRID_W, GRID_W, W_FNET), F32)
    re, im = pl.pallas_call(
        _fnet_rows_kernel,
        grid=grid,
        in_specs=[pl.BlockSpec((None, GRID_W, FN_SUB, W_FNET), lambda b, j: (seq0 + b, 0, j, 0)),
                  pl.BlockSpec((FN_SUB, 2 * GRID_W, GRID_W), lambda b, j: (j, 0, 0))],
        out_specs=[mid, mid],
        out_shape=[mid_shape, mid_shape],
        compiler_params=_cparams(("arbitrary", "arbitrary")),
        name="fnet_rows_lat",
    )(x4, m_rows)
    strided = pl.BlockSpec((None, GRID_W, FN_SUB, W_FNET), lambda b, j: (b, 0, j, 0))
    out = pl.pallas_call(
        _fnet_cols_kernel,
        grid=grid,
        in_specs=[strided, strided, pl.BlockSpec((2 * GRID_W, 2 * GRID_W), lambda b, j: (0, 0))]
        + _fnet_channel_specs(),
        out_specs=strided,
        out_shape=mid_shape,
        compiler_params=_cparams(("arbitrary", "arbitrary")),
        name="fnet_cols_lat",
    )(re, im, m_cols, *_fnet_channel_consts(w_bd, g_row))
    return out.reshape(T_LAT, W_FNET)


def _pad_heads(w, d_head):
    lead = w.shape[:-1]
    w = w.reshape(lead + (H_MLA, d_head))
    w = jnp.pad(w, [(0, 0)] * len(lead) + [(0, 0), (0, HEAD_PAD - d_head)])
    return w.reshape(lead + (W_HEADS,))


def _swap_rope(a):
    n = D_ROPE // 4
    parts = [jnp.zeros(a.shape[:-1] + (D_NOPE,), a.dtype)]
    for ax in range(2):
        base = D_NOPE + ax * 2 * n
        parts += [a[..., base + n:base + 2 * n], a[..., base:base + n]]
    parts.append(jnp.zeros(a.shape[:-1] + (HEAD_PAD - D_QK,), a.dtype))
    return jnp.concatenate(parts, axis=-1)


def _rope_tables(g_q, g_k):
    n_freq = D_ROPE // 4
    t = np.arange(L_LAT)
    pos = np.stack([t // GRID_W, t % GRID_W], axis=-1).astype(np.float32)
    freq = (np.float32(ROPE_THETA) ** (-np.arange(n_freq, dtype=np.float32) / n_freq)).astype(np.float32)
    ang = (pos[:, :, None] * freq).astype(np.float32)
    cos = np.ones((L_LAT, HEAD_PAD), np.float32)
    sin = np.zeros((L_LAT, HEAD_PAD), np.float32)
    for ax in range(2):
        base = D_NOPE + ax * 2 * n_freq
        c, s = np.cos(ang[:, ax, :]), np.sin(ang[:, ax, :])
        cos[:, base:base + n_freq] = c
        cos[:, base + n_freq:base + 2 * n_freq] = c
        sin[:, base:base + n_freq] = -s
        sin[:, base + n_freq:base + 2 * n_freq] = s
    cos = np.concatenate([np.ones((TM, HEAD_PAD), np.float32), cos]).reshape(1, 1 + LAT_TILES_PER_SEQ, TM, HEAD_PAD)
    sin = np.concatenate([np.zeros((TM, HEAD_PAD), np.float32), sin]).reshape(1, 1 + LAT_TILES_PER_SEQ, TM, HEAD_PAD)

    def pair(g, scale):
        g128 = jnp.pad(g, ((0, 0), (0, HEAD_PAD - D_QK))) * scale
        g_partner = _swap_rope(g128)
        return (cos * g128[:, None, None, :], sin * g_partner[:, None, None, :])
    return pair(g_q, D_QK ** -0.5 * LOG2_E) + pair(g_k, 1.0)


def kernel(x_prompt, x_sample, cache_ckv, cache_krope, state_hgrn, c, c_ctx, ada_w, ada_b, norm_g, ffn_w_gu, ffn_w_down, w_in, hgrn_lb, hgrn_norm_g, mla_q_norm_g, mla_w_q_up, mla_kv_norm_g, mla_w_kv_up, mla_qk_norm_g, mla_out_norm_g, fnet_w, fnet_norm_g, w_out):
    w_gu = ffn_w_gu.astype(BF16)
    w_down = ffn_w_down.astype(BF16)
    o = np.cumsum((0, 5 * W_HGRN, Q_RANK, KV_RANK, D_ROPE, W_FNET))
    assert o[3] == IN_MAIN
    w_in_b = w_in.astype(BF16)
    w_kr = jnp.pad(w_in[:, :, o[3]:o[4]], ((0, 0), (0, 0), (D_NOPE, HEAD_PAD - D_NOPE - D_ROPE)))
    w_tail = jnp.concatenate([w_kr, _swap_rope(w_kr), w_in[:, :, o[4]:]], axis=-1).astype(BF16)
    wq = _pad_heads(mla_w_q_up, D_QK)
    wq_sw = _swap_rope(wq.reshape(DEPTH, Q_RANK, H_MLA, HEAD_PAD)).reshape(DEPTH, Q_RANK, W_HEADS)
    wq2 = jnp.concatenate([wq, wq_sw], axis=-1).astype(BF16)
    w_kv = mla_w_kv_up.reshape(DEPTH, KV_RANK, H_MLA, D_NOPE + D_V)
    wk_arr = _pad_heads(w_kv[..., :D_NOPE].reshape(DEPTH, KV_RANK, H_MLA * D_NOPE), D_NOPE).astype(BF16)
    wv_arr = _pad_heads(w_kv[..., D_NOPE:].reshape(DEPTH, KV_RANK, H_MLA * D_V), D_V).astype(BF16)
    w_out_b = w_out.astype(BF16)
    eye_g = jnp.eye(G_FNET, dtype=F32)
    fnet_bd = jnp.einsum("lgcd,gh->lgchd", fnet_w, eye_g).reshape(DEPTH, W_FNET, W_FNET).astype(BF16)
    tabs = _rope_tables(mla_qk_norm_g[:, 0], mla_qk_norm_g[:, 1])
    lbs = jnp.cumsum(jax.nn.softmax(hgrn_lb.astype(F32), axis=0), axis=0)
    lbs = lbs - lbs[:1]

    cond8 = jnp.zeros((8, D), F32).at[0].set(c_ctx).at[1:1 + N_LAT_SEQ].set(c)
    mods = _mods(cond8, ada_w, ada_b).reshape(DEPTH, 8, N_MOD, D)

    x = (x_prompt.reshape(T_CTX, D), x_sample.reshape(T_LAT, D))
    ckv_out, kr_out, st_out = [], [], []
    zero_states = jnp.zeros((N_CTX_SEQ, H_HGRN, DV_HGRN, DK_HGRN), F32)
    for l in range(DEPTH):
        x = _ffn(x, mods, norm_g, w_gu, w_down, l, 0)
        hg, ckvn, kr128, uf, q, k, v = _inproj(x, mods, norm_g, w_in_b, w_tail, mla_q_norm_g, mla_kv_norm_g, wq2, wk_arr,
                                               wv_arr, tabs, l)

        s0 = [jnp.concatenate([zero_states, jnp.swapaxes(state_hgrn[:, l, d], -1, -2)], axis=0) for d in range(2)]
        h_fwd, h_bwd, s_f, s_b = _hgrn(hg, lbs[l], s0[0], s0[1])
        gn_row = jnp.tile(hgrn_norm_g[l], H_HGRN).reshape(1, W_HGRN)
        st_out.append(jnp.swapaxes(jnp.stack([s_f[:N_CTX_SEQ], s_b[:N_CTX_SEQ]], axis=1), -1, -2))

        cache_kr = jnp.pad(cache_krope[:, l], ((0, 0), (0, 0), (D_NOPE, HEAD_PAD - D_NOPE - D_ROPE)))
        k_past, v_past = _kv_cache(cache_ckv[:, l], cache_kr, _swap_rope(cache_kr), wk_arr, wv_arr, tabs, l)
        att_ctx = _attention_ctx(q, k, v)
        att_lat = _attention_lat(q, k, v, k_past, v_past)

        fn_gain = fnet_norm_g[l].reshape(1, W_FNET)
        fn_ctx = _fnet_ctx(uf, fnet_bd[l], fn_gain)
        fn_lat = _fnet_lat(uf, fnet_bd[l], fn_gain)

        x = _outproj(x, mods, h_fwd, h_bwd, hg, gn_row, att_ctx, att_lat, fn_ctx, fn_lat, mla_out_norm_g, w_out_b, l)
        if l == DEPTH - 1:
            y_prompt = _ffn(x, mods, norm_g, w_gu, w_down, l, 1, n_rows=T_CTX)
            y_sample = _ffn(x, mods, norm_g, w_gu, w_down, l, 1, src_row0=T_CTX, n_rows=T_LAT)
        else:
            x = _ffn(x, mods, norm_g, w_gu, w_down, l, 1)

        ckv_out.append(ckvn[:T_CTX].reshape(N_CTX_SEQ, L_CTX, KV_RANK))
        kr_out.append(kr128[:T_CTX, D_NOPE:D_NOPE + D_ROPE].reshape(N_CTX_SEQ, L_CTX, D_ROPE))

    return (y_prompt.reshape(N_CTX_SEQ, L_CTX, D), y_sample.reshape(N_LAT_SEQ, L_LAT, D),
            jnp.stack(ckv_out, axis=1), jnp.stack(kr_out, axis=1), jnp.stack(st_out, axis=1))
```

```python
import functools

import numpy as np
import jax
import jax.numpy as jnp
from jax import lax
from jax.experimental import pallas as pl
from jax.experimental.pallas import tpu as pltpu

F32 = jnp.float32
BF16 = jnp.bfloat16

D = 1024
N_CTX_SEQ, L_CTX = 32, 256
N_LAT_SEQ, L_LAT = 2, 4096
T_CTX = N_CTX_SEQ * L_CTX
T_LAT = N_LAT_SEQ * L_LAT
T_ALL = T_CTX + T_LAT
DEPTH = 2
PAST = 512
GRID_W = 64
N_MOD = 9
EPS = 1e-6

H_HGRN, DK_HGRN, DV_HGRN = 4, 64, 64
W_HGRN = H_HGRN * DV_HGRN
CHUNK = 32
HGRN_SAFE_LOG_DECAY = 60.0
H_MLA, Q_RANK, KV_RANK = 8, 384, 256
D_NOPE, D_ROPE, D_V = 64, 32, 64
D_QK = D_NOPE + D_ROPE
HEAD_PAD = 128
W_HEADS = H_MLA * HEAD_PAD
W_MLA = H_MLA * D_V
G_FNET, C_FNET = 4, 64
W_FNET = G_FNET * C_FNET
D_FF = 2816
FF_CHUNK = 256
ROPE_THETA = 10000.0
LOG2_E = 1.4426950408889634

IN_HG = 5 * W_HGRN
IN_MAIN = IN_HG + Q_RANK + KV_RANK
IN_TAIL = 2 * HEAD_PAD + W_FNET

TM = 512
FFN_TM = 512
N_TILES = T_ALL // TM
N_CTX_TILES = T_CTX // TM
LAT_TILES_PER_SEQ = L_LAT // TM
T_BLK = 256
TQ = 512
ATT_HEADS = 2
CTX_SEQ_PER_STEP = 4

VMEM_LIMIT = 56 * 1024 * 1024


def _cparams(sem):
    return pltpu.CompilerParams(dimension_semantics=sem, vmem_limit_bytes=VMEM_LIMIT)


def _tile_group(i, tm=TM):
    return (i >= T_CTX // tm).astype(jnp.int32) + (i >= (T_CTX + L_LAT) // tm).astype(jnp.int32)


def _silu(x):
    return x * (1.0 / (1.0 + jnp.exp(-x)))


def _rms_rows(x, g):
    return x * lax.rsqrt(jnp.mean(x * x, axis=-1, keepdims=True) + EPS) * g


def _dot(a, b):
    return jnp.dot(a, b, preferred_element_type=F32)


def _dot_nt(a, b):
    return lax.dot_general(a, b, (((1,), (1,)), ((), ())), preferred_element_type=F32)


def _dot_tn(a, b):
    return lax.dot_general(a, b, (((0,), (0,)), ((), ())), preferred_element_type=F32)


def _step_tile(i):
    return i


def _row_spec(width, col_block=0, tile=_step_tile, tm=TM):
    return pl.BlockSpec((tm, width), lambda i: (tile(i), col_block))


def _ctx_row_spec(width, tile=_step_tile, tm=TM):
    return pl.BlockSpec((tm, width), lambda i: (jnp.minimum(tile(i), T_CTX // tm - 1), 0))


def _lat_row_spec(width, tile=_step_tile, tm=TM):
    return pl.BlockSpec((tm, width), lambda i: (jnp.maximum(tile(i) - T_CTX // tm, 0), 0))


def _mod_spec(layer, tile=_step_tile, tm=TM):
    return pl.BlockSpec((None, None, N_MOD, D), lambda i: (layer, _tile_group(tile(i), tm), 0, 0))


def _resident(shape, index_map):
    return pl.BlockSpec(shape, index_map, pipeline_mode=pl.Buffered(1))


def _mods_kernel(c_ref, w_ref, b_ref, o_ref):
    a = _silu(c_ref[...]).astype(BF16)
    o_ref[...] = _dot(a, w_ref[...].astype(BF16)) + b_ref[...]


def _mods(cond8, ada_w, ada_b):
    tn = 1024
    return pl.pallas_call(
        _mods_kernel,
        grid=(DEPTH, N_MOD * D // tn),
        in_specs=[
            pl.BlockSpec((8, D), lambda l, j: (0, 0)),
            pl.BlockSpec((None, D, tn), lambda l, j: (l, 0, j)),
            pl.BlockSpec((None, 1, tn), lambda l, j: (l, 0, j)),
        ],
        out_specs=pl.BlockSpec((None, 8, tn), lambda l, j: (l, 0, j)),
        out_shape=jax.ShapeDtypeStruct((DEPTH, 8, N_MOD * D), F32),
        compiler_params=_cparams(("arbitrary", "arbitrary")),
        name="ada_mods",
    )(cond8, ada_w, ada_b.reshape(DEPTH, 1, N_MOD * D))


def _mixer_out(mod_ref, ctx, hf_ref, hb_ref, hgate_ref, gn_ref, ones_ref, ac_ref, al_ref, fc_ref, fl_ref, ga_ref,
               w_ref):
    gate = mod_ref[5:6, :]
    oh = hf_ref[...].astype(F32) + hb_ref[...].astype(F32)
    ms = _dot((oh * oh).astype(BF16), ones_ref[...])
    oh = oh * lax.rsqrt(ms + EPS) * gn_ref[...] * _silu(hgate_ref[...])
    oa = _rms_rows(jnp.where(ctx, ac_ref[...], al_ref[...]).astype(F32), ga_ref[...])
    of = jnp.where(ctx, fc_ref[...], fl_ref[...])
    o = (_dot(oh.astype(BF16), w_ref[0:W_HGRN, :])
         + _dot(oa.astype(BF16), w_ref[W_HGRN:W_HGRN + W_MLA, :])
         + _dot(of.astype(BF16), w_ref[W_HGRN + W_MLA:, :]))
    return gate * o


def _outproj_kernel(mod_ref, *rest):
    o_ref = rest[-1]
    o_ref[...] = _mixer_out(mod_ref, pl.program_id(0) < N_CTX_TILES, *rest[:-1]).astype(BF16)


def _outproj(mods, h_fwd, h_bwd, hg, gn_row, att_ctx, att_lat, fn_ctx, fn_lat, g_att, w_out, layer):
    lane = np.arange(W_HGRN)
    ones_bd = jnp.asarray((lane[:, None] // DV_HGRN == lane[None, :] // DV_HGRN).astype(np.float32) / DV_HGRN, BF16)
    return pl.pallas_call(
        _outproj_kernel,
        grid=(N_TILES,),
        in_specs=[_mod_spec(layer), _row_spec(W_HGRN), _row_spec(W_HGRN), _row_spec(W_HGRN, 4),
                  pl.BlockSpec((1, W_HGRN), lambda i: (0, 0)), pl.BlockSpec((W_HGRN, W_HGRN), lambda i: (0, 0)),
                  _ctx_row_spec(W_MLA), _lat_row_spec(W_MLA), _ctx_row_spec(W_FNET), _lat_row_spec(W_FNET),
                  pl.BlockSpec((None, 1, W_MLA), lambda i: (layer, 0, 0)),
                  _resident((None, D, D), lambda i: (layer, 0, 0))],
        out_specs=_row_spec(D),
        out_shape=jax.ShapeDtypeStruct((T_ALL, D), BF16),
        compiler_params=_cparams(("arbitrary",)),
        name=f"outproj_l{layer}",
    )(mods, h_fwd, h_bwd, hg, gn_row, ones_bd, att_ctx, att_lat, fn_ctx, fn_lat,
      g_att.reshape(DEPTH, 1, W_MLA), w_out)


def _ffn_kernel(*refs, mi, two_sources, with_delta):
    if two_sources:
        xc_ref, xl_ref, mod_ref, g_ref, wg_ref, wu_ref, wd_ref, o_ref = refs
        x = jnp.where(pl.program_id(0) < T_CTX // FFN_TM, xc_ref[...], xl_ref[...])
    elif with_delta:
        x_ref, dx_ref, mod_ref, g_ref, wg_ref, wu_ref, wd_ref, o_ref = refs
        x = x_ref[...] + dx_ref[...].astype(F32)
    else:
        x_ref, mod_ref, g_ref, wg_ref, wu_ref, wd_ref, o_ref = refs
        x = x_ref[...]
    shift = mod_ref[mi:mi + 1, :]
    scale = mod_ref[mi + 1:mi + 2, :]
    gate = mod_ref[mi + 2:mi + 3, :]
    hb = (_rms_rows(x, g_ref[...]) * (1.0 + scale) + shift).astype(BF16)
    acc = jnp.zeros(x.shape, F32)
    for j in range(D_FF // FF_CHUNK):
        cs = slice(j * FF_CHUNK, (j + 1) * FF_CHUNK)
        a = _silu(_dot(hb, wg_ref[:, cs])) * _dot(hb, wu_ref[:, cs])
        acc = acc + _dot(a.astype(BF16), wd_ref[cs, :])
    o_ref[...] = x + 0.5 * gate * acc


def _ffn(xs, mods, norm_g, w_gu, w_down, layer, which, *, delta=None, src_row0=0, n_rows=T_ALL):
    mi = 0 if which == 0 else 6
    gi = 0 if which == 0 else 2
    tm = FFN_TM

    def tile(i):
        return i + src_row0 // tm
    two = isinstance(xs, tuple)
    if two:
        in_specs = [_ctx_row_spec(D, tm=tm), _lat_row_spec(D, tm=tm)]
        args = list(xs)
    else:
        in_specs = [_row_spec(D, tile=tile, tm=tm)]
        args = [xs]
        if delta is not None:
            in_specs.append(_row_spec(D, tile=tile, tm=tm))
            args.append(delta)
    in_specs += [
        _mod_spec(layer, tile, tm),
        pl.BlockSpec((None, None, 1, D), lambda i: (layer, gi, 0, 0)),
        _resident((None, None, D, D_FF), lambda i: (layer, which, 0, 0)),
        _resident((None, None, D, D_FF), lambda i: (layer, which, 0, 1)),
        _resident((None, None, D_FF, D), lambda i: (layer, which, 0, 0)),
    ]
    args += [mods, norm_g.reshape(DEPTH, 3, 1, D), w_gu, w_gu, w_down]
    return pl.pallas_call(
        functools.partial(_ffn_kernel, mi=mi, two_sources=two, with_delta=delta is not None),
        grid=(n_rows // tm,),
        in_specs=in_specs,
        out_specs=pl.BlockSpec((tm, D), lambda i: (i, 0)),
        out_shape=jax.ShapeDtypeStruct((n_rows, D), F32),
        compiler_params=_cparams(("arbitrary",)),
        name=f"ffn_l{layer}_{which}",
    )(*args)


def _mla_heads(qa, ka, va, kr, kr_sw, tabs, q_ref, k_ref, v_ref):
    cq, sq, ck, sk = tabs
    lane = lax.broadcasted_iota(jnp.int32, (1, W_HEADS), 1) % HEAD_PAD
    v_ref[...] = jnp.where(lane == D_V, 1.0, va).astype(BF16)
    k_rot = kr_sw * sk
    heads = [slice(h * HEAD_PAD, (h + 1) * HEAD_PAD) for h in range(H_MLA)]

    def inv_rms(xs):
        return [lax.rsqrt(jnp.sum(x * x, axis=-1, keepdims=True) * (1.0 / D_QK) + EPS) for x in xs]
    if qa is not None:
        q = [qa[:, hs] for hs in heads]
        q_sw = [qa[:, W_HEADS + hs.start:W_HEADS + hs.stop] for hs in heads]
        for hs, x, x_sw, rs in zip(heads, q, q_sw, inv_rms(q)):
            q_ref[:, hs] = (rs * (x * cq + x_sw * sq)).astype(BF16)
    k = [ka[:, hs] + kr for hs in heads]
    for hs, x, rs in zip(heads, k, inv_rms(k)):
        k_ref[:, hs] = (rs * (x * ck + k_rot)).astype(BF16)


def _inproj_kernel(x_ref, mod_ref, g_ref, w_ref, wt_ref, gq_ref, gkv_ref, wq_ref, wk_ref, wv_ref, cq_ref, sq_ref,
                   ck_ref, sk_ref, hg_ref, ckv_ref, kr_ref, uf_ref, q_ref, k_ref, v_ref):
    x = x_ref[...]
    shift = mod_ref[3:4, :]
    scale = mod_ref[4:5, :]
    hb = (_rms_rows(x, g_ref[...]) * (1.0 + scale) + shift).astype(BF16)
    ul = _dot(hb, w_ref[:, IN_HG:])
    cqn = _rms_rows(ul[:, :Q_RANK], gq_ref[...])
    ckvn = _rms_rows(ul[:, Q_RANK:], gkv_ref[...])
    ckv_ref[...] = ckvn
    ut = _dot(hb, wt_ref[...])
    kr = ut[:, 0:HEAD_PAD]
    kr_ref[...] = kr
    kr_sw = ut[:, HEAD_PAD:2 * HEAD_PAD]
    uf_ref[...] = ut[:, 2 * HEAD_PAD:]
    hg_ref[...] = _dot(hb, w_ref[:, :IN_HG])
    cb = ckvn.astype(BF16)
    _mla_heads(_dot(cqn.astype(BF16), wq_ref[...]), _dot(cb, wk_ref[...]), _dot(cb, wv_ref[...]), kr, kr_sw,
               (cq_ref[...], sq_ref[...], ck_ref[...], sk_ref[...]), q_ref, k_ref, v_ref)


def _rope_block(i):
    return jnp.where(i < N_CTX_TILES, 0, 1 + (i - N_CTX_TILES) % LAT_TILES_PER_SEQ)


def _inproj(x, mods, norm_g, w_in_b, w_tail, gq, gkv, wq2, wk_arr, wv_arr, tabs, layer):
    tab = pl.BlockSpec((None, None, TM, HEAD_PAD), lambda i: (layer, _rope_block(i), 0, 0))
    f32_widths = (IN_HG, KV_RANK, HEAD_PAD, W_FNET)
    return pl.pallas_call(
        _inproj_kernel,
        grid=(N_TILES,),
        in_specs=[
            _row_spec(D),
            _mod_spec(layer),
            pl.BlockSpec((None, None, 1, D), lambda i: (layer, 1, 0, 0)),
            _resident((None, D, IN_MAIN), lambda i: (layer, 0, 0)),
            _resident((None, D, IN_TAIL), lambda i: (layer, 0, 0)),
            pl.BlockSpec((None, 1, Q_RANK), lambda i: (layer, 0, 0)),
            pl.BlockSpec((None, 1, KV_RANK), lambda i: (layer, 0, 0)),
            _resident((None, Q_RANK, 2 * W_HEADS), lambda i: (layer, 0, 0)),
            _resident((None, KV_RANK, W_HEADS), lambda i: (layer, 0, 0)),
            _resident((None, KV_RANK, W_HEADS), lambda i: (layer, 0, 0)),
            tab, tab, tab, tab,
        ],
        out_specs=[_row_spec(w) for w in f32_widths] + [_row_spec(W_HEADS)] * 3,
        out_shape=[jax.ShapeDtypeStruct((T_ALL, w), F32) for w in f32_widths]
        + [jax.ShapeDtypeStruct((T_ALL, W_HEADS), BF16)] * 3,
        compiler_params=_cparams(("arbitrary",)),
        name=f"inproj_l{layer}",
    )(x, mods, norm_g.reshape(DEPTH, 3, 1, D), w_in_b, w_tail, gq.reshape(DEPTH, 1, Q_RANK),
      gkv.reshape(DEPTH, 1, KV_RANK), wq2, wk_arr, wv_arr, *tabs)


def _kv_cache_kernel(ckv_ref, kr_ref, krsw_ref, wk_ref, wv_ref, ck_ref, sk_ref, k_ref, v_ref):
    cb = ckv_ref[...].astype(BF16)
    _mla_heads(None, _dot(cb, wk_ref[...]), _dot(cb, wv_ref[...]), kr_ref[...], krsw_ref[...],
               (None, None, ck_ref[...], sk_ref[...]), None, k_ref, v_ref)


def _kv_cache(cache_ckv_l, cache_kr, cache_kr_sw, wk_arr, wv_arr, tabs, layer):
    assert PAST == TM
    blk = lambda w: pl.BlockSpec((None, PAST, w), lambda b: (b, 0, 0))
    tab = pl.BlockSpec((None, None, TM, HEAD_PAD), lambda b: (layer, 0, 0, 0))
    return pl.pallas_call(
        _kv_cache_kernel,
        grid=(N_LAT_SEQ,),
        in_specs=[blk(KV_RANK), blk(HEAD_PAD), blk(HEAD_PAD),
                  _resident((None, KV_RANK, W_HEADS), lambda b: (layer, 0, 0)),
                  _resident((None, KV_RANK, W_HEADS), lambda b: (layer, 0, 0)), tab, tab],
        out_specs=[blk(W_HEADS)] * 2,
        out_shape=[jax.ShapeDtypeStruct((N_LAT_SEQ, PAST, W_HEADS), BF16)] * 2,
        compiler_params=_cparams(("arbitrary",)),
        name=f"mla_kv_cache_l{layer}",
    )(cache_ckv_l, cache_kr, cache_kr_sw, wk_arr, wv_arr, tabs[2], tabs[3])


N_SEQ = N_CTX_SEQ + N_LAT_SEQ
BLK_PER_LAT = L_LAT // T_BLK
N_HGRN_STEPS = N_CTX_SEQ + N_LAT_SEQ * BLK_PER_LAT


def _hgrn_seq(i):
    return jnp.where(i < N_CTX_SEQ, i, N_CTX_SEQ + (i - N_CTX_SEQ) // BLK_PER_LAT)


def _hgrn_blk(i, reverse):
    j = (i - N_CTX_SEQ) % BLK_PER_LAT
    if reverse:
        j = BLK_PER_LAT - 1 - j
    lat = N_CTX_SEQ + ((i - N_CTX_SEQ) // BLK_PER_LAT) * BLK_PER_LAT + j
    return jnp.where(i < N_CTX_SEQ, i, lat)


def _hgrn_kernel(qf_ref, ff_ref, vf_ref, qb_ref, fb_ref, vb_ref, lb_ref, s0f_ref, s0b_ref,
                 tri_ref, tri4_ref, hm_ref, bd_ref, of_ref, ob_ref, sf_ref, sb_ref, stf_scr, stb_scr,
                 ks_scr, bs_scr, vs_scr, oi_scr):
    i = pl.program_id(0)
    first = jnp.logical_or(i < N_CTX_SEQ, (i - N_CTX_SEQ) % BLK_PER_LAT == 0)
    heads = [slice(h * DK_HGRN, (h + 1) * DK_HGRN) for h in range(H_HGRN)]

    @pl.when(first)
    def _():
        for s0_ref, scr in ((s0f_ref, stf_scr), (s0b_ref, stb_scr)):
            scr[...] = jnp.zeros(scr.shape, F32)
            for h, hs in enumerate(heads):
                scr[hs, hs] = s0_ref[h]

    lb = lb_ref[...]
    loglb = jnp.log(lb)
    log1mlb = jnp.log(1.0 - lb)
    n_chunks = T_BLK // CHUNK
    dirs = ((qf_ref, ff_ref, vf_ref, of_ref, stf_scr), (qb_ref, fb_ref, vb_ref, ob_ref, stb_scr))
    units = [(d, c if d == 0 else n_chunks - 1 - c) for c in range(n_chunks) for d in (0, 1)]
    rows = [slice(cc * CHUNK, (cc + 1) * CHUNK) for _, cc in units]
    end_row = (CHUNK - 1, 0)
    mid_row = (CHUNK // 2 - 1, CHUNK // 2)
    hm, bd = hm_ref[...], bd_ref[...]

    q = [_silu(dirs[d][0][r, :]) for (d, _), r in zip(units, rows)]
    v = [dirs[d][2][r, :] for (d, _), r in zip(units, rows)]
    g = []
    for (d, _), r in zip(units, rows):
        x = dirs[d][1][r, :]
        y = log1mlb[d:d + 1] + (jnp.minimum(x, 0.0) - jnp.log(1.0 + jnp.exp(-jnp.abs(x))))
        g.append(jnp.maximum(loglb[d:d + 1], y) + jnp.log(1.0 + jnp.exp(-jnp.abs(loglb[d:d + 1] - y))))
    kk = [1.0 - jnp.exp(gu) for gu in g]
    g_hi = [gu.astype(BF16) for gu in g]
    g_lo = [(gu - gh.astype(F32)).astype(BF16) for gu, gh in zip(g, g_hi)]
    b = [_dot(tri_ref[d], gh) + _dot(tri_ref[d], gl) for (d, _), gh, gl in zip(units, g_hi, g_lo)]
    b_end = [bu[end_row[d]:end_row[d] + 1, :] for (d, _), bu in zip(units, b)]

    def finish(o_intra):
        q_in = [(qu * jnp.exp(bu)).astype(BF16) for qu, bu in zip(q, b)]
        k_e = [(ku * jnp.exp(be - bu)).astype(BF16) for ku, bu, be in zip(kk, b, b_end)]
        decay = [jnp.exp(be) for be in b_end]
        kv = [_dot_tn(vu.astype(BF16), ku) * bd for vu, ku in zip(v, k_e)]
        st = [stf_scr[...], stb_scr[...]]
        for u, (d, _) in enumerate(units):
            dirs[d][3][rows[u], :] = (o_intra[u] + _dot_nt(q_in[u], st[d].astype(BF16))).astype(BF16)
            st[d] = st[d] * decay[u] + kv[u]
        stf_scr[...] = st[0]
        stb_scr[...] = st[1]
        for h, hs in enumerate(heads):
            sf_ref[h] = st[0][hs, hs]
            sb_ref[h] = st[1][hs, hs]

    safe = jnp.max(jnp.concatenate([jnp.abs(be) for be in b_end], axis=0)) <= HGRN_SAFE_LOG_DECAY

    @pl.when(safe)
    def _():
        b_mid = [bu[mid_row[d]:mid_row[d] + 1, :] for (d, _), bu in zip(units, b)]
        q_t = [(qu * jnp.exp(bu - bm)).astype(BF16) for qu, bu, bm in zip(q, b, b_mid)]
        k_t = [ku * jnp.exp(bm - bu) for ku, bu, bm in zip(kk, b, b_mid)]
        k_bd = [jnp.concatenate([ku.astype(BF16)] * H_HGRN, axis=0) * hm for ku in k_t]
        v_bd = [jnp.concatenate([vu.astype(BF16)] * H_HGRN, axis=0) * hm for vu in v]
        sc = [(_dot_nt(qu, ku) * tri4_ref[d]).astype(BF16) for (d, _), qu, ku in zip(units, q_t, k_bd)]
        finish([_dot(su, vu) for su, vu in zip(sc, v_bd)])

    @pl.when(jnp.logical_not(safe))
    def _():
        t = lax.broadcasted_iota(jnp.int32, (CHUNK, 1), 0)
        head_sum = bd.astype(BF16)
        for u in range(len(units)):
            ks_scr[u] = kk[u]
            bs_scr[u] = b[u]
            vs_scr[u] = v[u]
            oi_scr[u] = jnp.zeros((CHUNK, W_HGRN), F32)

        def source_row(s, carry):
            for u, (d, _) in enumerate(units):
                seen = (t >= s) if d == 0 else (t <= s)
                e = jnp.exp(jnp.where(seen, b[u] - bs_scr[u, pl.ds(s, 1), :], -jnp.inf))
                p = (q[u] * ks_scr[u, pl.ds(s, 1), :] * e).astype(BF16)
                oi_scr[u] += _dot(p, head_sum) * vs_scr[u, pl.ds(s, 1), :]
            return carry
        lax.fori_loop(0, CHUNK, source_row, 0)
        finish([oi_scr[u] for u in range(len(units))])


def _hgrn_consts():
    t = np.arange(CHUNK)
    tri = np.stack([t[:, None] >= t[None, :], t[:, None] <= t[None, :]]).astype(np.float32)
    tri4 = np.tile(tri, (1, 1, H_HGRN))
    r = np.arange(H_HGRN * CHUNK)
    lane = np.arange(W_HGRN)
    hm = (r[:, None] // CHUNK == lane[None, :] // DK_HGRN).astype(np.float32)
    bd = (lane[:, None] // DV_HGRN == lane[None, :] // DK_HGRN).astype(np.float32)
    return jnp.asarray(tri, BF16), jnp.asarray(tri4), jnp.asarray(hm, BF16), jnp.asarray(bd)


def _hgrn(hg, lb2, s0f, s0b):
    tri, tri4, hm, bd = _hgrn_consts()

    def col(cb, reverse):
        return pl.BlockSpec((T_BLK, W_HGRN), lambda i: (_hgrn_blk(i, reverse), cb))

    def const(shape):
        return pl.BlockSpec(shape, lambda i: (0,) * len(shape))
    state = pl.BlockSpec((None, H_HGRN, DV_HGRN, DK_HGRN), lambda i: (_hgrn_seq(i), 0, 0, 0))
    return pl.pallas_call(
        _hgrn_kernel,
        grid=(N_HGRN_STEPS,),
        in_specs=[col(0, False), col(1, False), col(3, False), col(0, True), col(2, True), col(3, True),
                  const((2, W_HGRN)), state, state,
                  const(tri.shape), const(tri4.shape), const(hm.shape), const(bd.shape)],
        out_specs=[col(0, False), col(0, True), state, state],
        out_shape=[jax.ShapeDtypeStruct((T_ALL, W_HGRN), BF16)] * 2
        + [jax.ShapeDtypeStruct((N_SEQ, H_HGRN, DV_HGRN, DK_HGRN), F32)] * 2,
        scratch_shapes=[pltpu.VMEM((W_HGRN, W_HGRN), F32)] * 2
        + [pltpu.VMEM((2 * T_BLK // CHUNK, CHUNK, W_HGRN), F32)] * 4,
        compiler_params=_cparams(("arbitrary",)),
        name="hgrn",
    )(hg, hg, hg, hg, hg, hg, lb2, s0f, s0b, tri, tri4, hm, bd)


def _attn_kernel(q_ref, k_ref, v_ref, o_ref, *, n_seq=1):
    lane = lax.broadcasted_iota(jnp.int32, (1, HEAD_PAD), 1)
    lq, lk = q_ref.shape[0] // n_seq, k_ref.shape[0] // n_seq
    n_heads = q_ref.shape[1] // HEAD_PAD
    units = [(slice(b * lq, (b + 1) * lq), slice(b * lk, (b + 1) * lk), slice(h * HEAD_PAD, (h + 1) * HEAD_PAD))
             for b in range(n_seq) for h in range(n_heads)]
    s = [_dot_nt(q_ref[qr, hs], k_ref[kr, hs]) for qr, kr, hs in units]
    p = [jnp.exp2(sh - jnp.max(sh, axis=-1, keepdims=True)).astype(BF16) for sh in s]
    pv = [_dot(ph, v_ref[kr, hs]) for ph, (_, kr, hs) in zip(p, units)]
    o = [jnp.where(lane < D_V, x * (1.0 / x[:, D_V:D_V + 1]), 0.0) for x in pv]
    for b in range(n_seq):
        for hp in range(n_heads // 2):
            u = b * n_heads + 2 * hp
            o_ref[units[u][0], hp * 2 * D_V:(hp + 1) * 2 * D_V] = (
                o[u] + pltpu.roll(o[u + 1], D_V, axis=1)).astype(BF16)


def _attn_lat_kernel(q_ref, kp_ref, vp_ref, kn_ref, vn_ref, o_ref, k_scr, v_scr):
    @pl.when(pl.program_id(2) == 0)
    def _():
        k_scr[0:PAST, :] = kp_ref[...]
        k_scr[PAST:, :] = kn_ref[...]
        v_scr[0:PAST, :] = vp_ref[...]
        v_scr[PAST:, :] = vn_ref[...]
    _attn_kernel(q_ref, k_scr, v_scr, o_ref)


def _attention_ctx(q, k, v):
    rows = CTX_SEQ_PER_STEP * L_CTX
    blk = pl.BlockSpec((rows, W_HEADS), lambda b: (b, 0))
    return pl.pallas_call(
        functools.partial(_attn_kernel, n_seq=CTX_SEQ_PER_STEP),
        grid=(N_CTX_SEQ // CTX_SEQ_PER_STEP,),
        in_specs=[blk, blk, blk],
        out_specs=pl.BlockSpec((rows, W_MLA), lambda b: (b, 0)),
        out_shape=jax.ShapeDtypeStruct((T_CTX, W_MLA), BF16),
        compiler_params=_cparams(("arbitrary",)),
        name="attn_ctx",
    )(q, k, v)


def _attention_lat(q, k, v, k_past, v_past):
    grp = ATT_HEADS * HEAD_PAD
    nq = L_LAT // TQ
    q0 = T_CTX // TQ
    seq0 = T_CTX // L_LAT
    new = pl.BlockSpec((L_LAT, grp), lambda b, hp, qi: (seq0 + b, hp))
    past = pl.BlockSpec((None, PAST, grp), lambda b, hp, qi: (b, 0, hp))
    return pl.pallas_call(
        _attn_lat_kernel,
        grid=(N_LAT_SEQ, H_MLA // ATT_HEADS, nq),
        in_specs=[pl.BlockSpec((TQ, grp), lambda b, hp, qi: (q0 + b * nq + qi, hp)), past, past, new, new],
        out_specs=pl.BlockSpec((TQ, ATT_HEADS * D_V), lambda b, hp, qi: (b * nq + qi, hp)),
        out_shape=jax.ShapeDtypeStruct((T_LAT, W_MLA), BF16),
        scratch_shapes=[pltpu.VMEM((PAST + L_LAT, grp), BF16)] * 2,
        compiler_params=_cparams(("arbitrary", "arbitrary", "arbitrary")),
        name="attn_lat",
    )(q, k_past, v_past, k, v)


def _dft_tables(n):
    a = 2.0 * np.pi * np.outer(np.arange(n), np.arange(n)) / n
    return np.cos(a), np.sin(a)


def _bf16_operand(table):
    return jnp.asarray(table, F32).astype(BF16)


def _fnet_finish(re, im, cs_ref, w_ref, g_ref):
    spec = _dot(jnp.concatenate([re, im], axis=1).astype(BF16), cs_ref[...])
    return _rms_rows(_dot(spec.astype(BF16), w_ref[...]), g_ref[...])


def _fnet_channel_consts(w_bd, g_row):
    c, s = _dft_tables(C_FNET)
    eye = np.eye(G_FNET)
    return [_bf16_operand(np.concatenate([np.kron(eye, c), np.kron(eye, s)], axis=0)), w_bd, g_row]


def _fnet_channel_specs():
    zero = lambda *a: (0, 0)
    return [pl.BlockSpec((2 * W_FNET, W_FNET), zero), pl.BlockSpec((W_FNET, W_FNET), zero),
            pl.BlockSpec((1, W_FNET), zero)]


def _fnet_ctx_kernel(u_ref, m_ref, cs_ref, w_ref, g_ref, o_ref):
    p = [_dot(m_ref[...], u_ref[b * L_CTX:(b + 1) * L_CTX, :].astype(BF16)) for b in range(CTX_SEQ_PER_STEP)]
    o_ref[...] = _fnet_finish(jnp.concatenate([pb[:L_CTX] for pb in p], axis=0),
                              jnp.concatenate([pb[L_CTX:] for pb in p], axis=0), cs_ref, w_ref, g_ref)


def _fnet_ctx(uf, w_bd, g_row):
    c, s = _dft_tables(L_CTX)
    norm = 1.0 / np.sqrt(L_CTX * C_FNET)
    blk = pl.BlockSpec((CTX_SEQ_PER_STEP * L_CTX, W_FNET), lambda b: (b, 0))
    return pl.pallas_call(
        _fnet_ctx_kernel,
        grid=(N_CTX_SEQ // CTX_SEQ_PER_STEP,),
        in_specs=[blk, pl.BlockSpec((2 * L_CTX, L_CTX), lambda b: (0, 0))] + _fnet_channel_specs(),
        out_specs=blk,
        out_shape=jax.ShapeDtypeStruct((T_CTX, W_FNET), F32),
        compiler_params=_cparams(("arbitrary",)),
        name="fnet_ctx",
    )(uf, _bf16_operand(np.concatenate([c, -s], axis=0) * norm), *_fnet_channel_consts(w_bd, g_row))


FN_SUB = 8


def _fnet_rows_kernel(x_ref, m_ref, re_ref, im_ref):
    for s in range(FN_SUB):
        y = _dot(m_ref[s], x_ref[:, s, :].astype(BF16))
        re_ref[s] = y[:GRID_W]
        im_ref[s] = y[GRID_W:]


def _fnet_cols_kernel(re_ref, im_ref, m_ref, cs_ref, w_ref, g_ref, o_ref):
    z = [_dot(m_ref[...], jnp.concatenate([re_ref[:, s, :], im_ref[:, s, :]], axis=0).astype(BF16))
         for s in range(FN_SUB)]
    out = _fnet_finish(jnp.concatenate([zs[:GRID_W] for zs in z], axis=0),
                       jnp.concatenate([zs[GRID_W:] for zs in z], axis=0), cs_ref, w_ref, g_ref)
    for s in range(FN_SUB):
        o_ref[:, s, :] = out[s * GRID_W:(s + 1) * GRID_W]


def _fnet_lat(uf, w_bd, g_row):
    p1 = np.arange(GRID_W)[None, :, None]
    l = GRID_W * np.arange(GRID_W)[None, None, :] + np.arange(GRID_W)[:, None, None]
    ang = 2.0 * np.pi * p1 * l / L_LAT
    norm = 1.0 / np.sqrt(L_LAT * C_FNET)
    m_rows = _bf16_operand(np.concatenate([np.cos(ang), -np.sin(ang)], axis=1) * norm)
    c64, s64 = _dft_tables(GRID_W)
    m_cols = _bf16_operand(np.block([[c64, s64], [-s64, c64]]))
    x4 = uf.reshape(T_ALL // L_LAT, GRID_W, GRID_W, W_FNET)
    seq0 = T_CTX // L_LAT
    grid = (N_LAT_SEQ, GRID_W // FN_SUB)
    mid = pl.BlockSpec((None, FN_SUB, GRID_W, W_FNET), lambda b, j: (b, j, 0, 0))
    mid_shape = jax.ShapeDtypeStruct((N_LAT_SEQ, GRID_W, GRID_W, W_FNET), F32)
    re, im = pl.pallas_call(
        _fnet_rows_kernel,
        grid=grid,
        in_specs=[pl.BlockSpec((None, GRID_W, FN_SUB, W_FNET), lambda b, j: (seq0 + b, 0, j, 0)),
                  pl.BlockSpec((FN_SUB, 2 * GRID_W, GRID_W), lambda b, j: (j, 0, 0))],
        out_specs=[mid, mid],
        out_shape=[mid_shape, mid_shape],
        compiler_params=_cparams(("arbitrary", "arbitrary")),
        name="fnet_rows_lat",
    )(x4, m_rows)
    strided = pl.BlockSpec((None, GRID_W, FN_SUB, W_FNET), lambda b, j: (b, 0, j, 0))
    out = pl.pallas_call(
        _fnet_cols_kernel,
        grid=grid,
        in_specs=[strided, strided, pl.BlockSpec((2 * GRID_W, 2 * GRID_W), lambda b, j: (0, 0))]
        + _fnet_channel_specs(),
        out_specs=strided,
        out_shape=mid_shape,
        compiler_params=_cparams(("arbitrary", "arbitrary")),
        name="fnet_cols_lat",
    )(re, im, m_cols, *_fnet_channel_consts(w_bd, g_row))
    return out.reshape(T_LAT, W_FNET)


def _pad_heads(w, d_head):
    lead = w.shape[:-1]
    w = w.reshape(lead + (H_MLA, d_head))
    w = jnp.pad(w, [(0, 0)] * len(lead) + [(0, 0), (0, HEAD_PAD - d_head)])
    return w.reshape(lead + (W_HEADS,))


def _swap_rope(a):
    n = D_ROPE // 4
    parts = [jnp.zeros(a.shape[:-1] + (D_NOPE,), a.dtype)]
    for ax in range(2):
        base = D_NOPE + ax * 2 * n
        parts += [a[..., base + n:base + 2 * n], a[..., base:base + n]]
    parts.append(jnp.zeros(a.shape[:-1] + (HEAD_PAD - D_QK,), a.dtype))
    return jnp.concatenate(parts, axis=-1)


def _rope_tables(g_q, g_k):
    n_freq = D_ROPE // 4
    t = np.arange(L_LAT)
    pos = np.stack([t // GRID_W, t % GRID_W], axis=-1).astype(np.float32)
    freq = (np.float32(ROPE_THETA) ** (-np.arange(n_freq, dtype=np.float32) / n_freq)).astype(np.float32)
    ang = (pos[:, :, None] * freq).astype(np.float32)
    cos = np.ones((L_LAT, HEAD_PAD), np.float32)
    sin = np.zeros((L_LAT, HEAD_PAD), np.float32)
    for ax in range(2):
        base = D_NOPE + ax * 2 * n_freq
        c, s = np.cos(ang[:, ax, :]), np.sin(ang[:, ax, :])
        cos[:, base:base + n_freq] = c
        cos[:, base + n_freq:base + 2 * n_freq] = c
        sin[:, base:base + n_freq] = -s
        sin[:, base + n_freq:base + 2 * n_freq] = s
    cos = np.concatenate([np.ones((TM, HEAD_PAD), np.float32), cos]).reshape(1, 1 + LAT_TILES_PER_SEQ, TM, HEAD_PAD)
    sin = np.concatenate([np.zeros((TM, HEAD_PAD), np.float32), sin]).reshape(1, 1 + LAT_TILES_PER_SEQ, TM, HEAD_PAD)

    def pair(g, scale):
        g128 = jnp.pad(g, ((0, 0), (0, HEAD_PAD - D_QK))) * scale
        g_partner = _swap_rope(g128)
        return (cos * g128[:, None, None, :], sin * g_partner[:, None, None, :])
    return pair(g_q, D_QK ** -0.5 * LOG2_E) + pair(g_k, 1.0)


def kernel(x_prompt, x_sample, cache_ckv, cache_krope, state_hgrn, c, c_ctx, ada_w, ada_b, norm_g, ffn_w_gu, ffn_w_down, w_in, hgrn_lb, hgrn_norm_g, mla_q_norm_g, mla_w_q_up, mla_kv_norm_g, mla_w_kv_up, mla_qk_norm_g, mla_out_norm_g, fnet_w, fnet_norm_g, w_out):
    w_gu = ffn_w_gu.astype(BF16)
    w_down = ffn_w_down.astype(BF16)
    o = np.cumsum((0, 5 * W_HGRN, Q_RANK, KV_RANK, D_ROPE, W_FNET))
    assert o[3] == IN_MAIN
    w_in_b = w_in.astype(BF16)
    w_kr = jnp.pad(w_in[:, :, o[3]:o[4]], ((0, 0), (0, 0), (D_NOPE, HEAD_PAD - D_NOPE - D_ROPE)))
    w_tail = jnp.concatenate([w_kr, _swap_rope(w_kr), w_in[:, :, o[4]:]], axis=-1).astype(BF16)
    wq = _pad_heads(mla_w_q_up, D_QK)
    wq_sw = _swap_rope(wq.reshape(DEPTH, Q_RANK, H_MLA, HEAD_PAD)).reshape(DEPTH, Q_RANK, W_HEADS)
    wq2 = jnp.concatenate([wq, wq_sw], axis=-1).astype(BF16)
    w_kv = mla_w_kv_up.reshape(DEPTH, KV_RANK, H_MLA, D_NOPE + D_V)
    wk_arr = _pad_heads(w_kv[..., :D_NOPE].reshape(DEPTH, KV_RANK, H_MLA * D_NOPE), D_NOPE).astype(BF16)
    wv_arr = _pad_heads(w_kv[..., D_NOPE:].reshape(DEPTH, KV_RANK, H_MLA * D_V), D_V).astype(BF16)
    w_out_b = w_out.astype(BF16)
    eye_g = jnp.eye(G_FNET, dtype=F32)
    fnet_bd = jnp.einsum("lgcd,gh->lgchd", fnet_w, eye_g).reshape(DEPTH, W_FNET, W_FNET).astype(BF16)
    tabs = _rope_tables(mla_qk_norm_g[:, 0], mla_qk_norm_g[:, 1])
    lbs = jnp.cumsum(jax.nn.softmax(hgrn_lb.astype(F32), axis=0), axis=0)
    lbs = lbs - lbs[:1]

    cond8 = jnp.zeros((8, D), F32).at[0].set(c_ctx).at[1:1 + N_LAT_SEQ].set(c)
    mods = _mods(cond8, ada_w, ada_b).reshape(DEPTH, 8, N_MOD, D)

    x = (x_prompt.reshape(T_CTX, D), x_sample.reshape(T_LAT, D))
    ckv_out, kr_out, st_out = [], [], []
    zero_states = jnp.zeros((N_CTX_SEQ, H_HGRN, DV_HGRN, DK_HGRN), F32)
    for l in range(DEPTH):
        x = _ffn(x, mods, norm_g, w_gu, w_down, l, 0)
        hg, ckvn, kr128, uf, q, k, v = _inproj(x, mods, norm_g, w_in_b, w_tail, mla_q_norm_g, mla_kv_norm_g, wq2, wk_arr,
                                               wv_arr, tabs, l)

        s0 = [jnp.concatenate([zero_states, jnp.swapaxes(state_hgrn[:, l, d], -1, -2)], axis=0) for d in range(2)]
        h_fwd, h_bwd, s_f, s_b = _hgrn(hg, lbs[l], s0[0], s0[1])
        gn_row = jnp.tile(hgrn_norm_g[l], H_HGRN).reshape(1, W_HGRN)
        st_out.append(jnp.swapaxes(jnp.stack([s_f[:N_CTX_SEQ], s_b[:N_CTX_SEQ]], axis=1), -1, -2))

        cache_kr = jnp.pad(cache_krope[:, l], ((0, 0), (0, 0), (D_NOPE, HEAD_PAD - D_NOPE - D_ROPE)))
        k_past, v_past = _kv_cache(cache_ckv[:, l], cache_kr, _swap_rope(cache_kr), wk_arr, wv_arr, tabs, l)
        att_ctx = _attention_ctx(q, k, v)
        att_lat = _attention_lat(q, k, v, k_past, v_past)

        fn_gain = fnet_norm_g[l].reshape(1, W_FNET)
        fn_ctx = _fnet_ctx(uf, fnet_bd[l], fn_gain)
        fn_lat = _fnet_lat(uf, fnet_bd[l], fn_gain)

        dx = _outproj(mods, h_fwd, h_bwd, hg, gn_row, att_ctx, att_lat, fn_ctx, fn_lat, mla_out_norm_g, w_out_b, l)
        if l == DEPTH - 1:
            y_prompt = _ffn(x, mods, norm_g, w_gu, w_down, l, 1, delta=dx, n_rows=T_CTX)
            y_sample = _ffn(x, mods, norm_g, w_gu, w_down, l, 1, delta=dx, src_row0=T_CTX, n_rows=T_LAT)
        else:
            x = _ffn(x, mods, norm_g, w_gu, w_down, l, 1, delta=dx)

        ckv_out.append(ckvn[:T_CTX].reshape(N_CTX_SEQ, L_CTX, KV_RANK))
        kr_out.append(kr128[:T_CTX, D_NOPE:D_NOPE + D_ROPE].reshape(N_CTX_SEQ, L_CTX, D_ROPE))

    return (y_prompt.reshape(N_CTX_SEQ, L_CTX, D), y_sample.reshape(N_LAT_SEQ, L_LAT, D),
            jnp.stack(ckv_out, axis=1), jnp.stack(kr_out, axis=1), jnp.stack(st_out, axis=1))
```

```python
import functools

import numpy as np
import jax
import jax.numpy as jnp
from jax import lax
from jax.experimental import pallas as pl
from jax.experimental.pallas import tpu as pltpu

F32 = jnp.float32
BF16 = jnp.bfloat16

D = 1024
N_CTX_SEQ, L_CTX = 32, 256
N_LAT_SEQ, L_LAT = 2, 4096
T_CTX = N_CTX_SEQ * L_CTX
T_LAT = N_LAT_SEQ * L_LAT
T_ALL = T_CTX + T_LAT
DEPTH = 2
PAST = 512
GRID_W = 64
N_MOD = 9
EPS = 1e-6

H_HGRN, DK_HGRN, DV_HGRN = 4, 64, 64
W_HGRN = H_HGRN * DV_HGRN
CHUNK = 32
HGRN_SAFE_LOG_DECAY = 60.0
H_MLA, Q_RANK, KV_RANK = 8, 384, 256
D_NOPE, D_ROPE, D_V = 64, 32, 64
D_QK = D_NOPE + D_ROPE
HEAD_PAD = 128
W_HEADS = H_MLA * HEAD_PAD
W_MLA = H_MLA * D_V
G_FNET, C_FNET = 4, 64
W_FNET = G_FNET * C_FNET
D_FF = 2816
FF_CHUNK = 256
ROPE_THETA = 10000.0
LOG2_E = 1.4426950408889634

IN_HG = 5 * W_HGRN
IN_MAIN = IN_HG + Q_RANK + KV_RANK
IN_TAIL = 2 * HEAD_PAD + W_FNET

TM = 512
FFN_TM = 512
N_TILES = T_ALL // TM
N_CTX_TILES = T_CTX // TM
LAT_TILES_PER_SEQ = L_LAT // TM
T_BLK = 256
TQ = 2048
TQ_SUB = 512
ATT_HEADS = 2
CTX_SEQ_PER_STEP = 4

VMEM_LIMIT = 56 * 1024 * 1024


def _cparams(sem):
    return pltpu.CompilerParams(dimension_semantics=sem, vmem_limit_bytes=VMEM_LIMIT)


def _tile_group(i, tm=TM):
    return (i >= T_CTX // tm).astype(jnp.int32) + (i >= (T_CTX + L_LAT) // tm).astype(jnp.int32)


def _silu(x):
    return x * (1.0 / (1.0 + jnp.exp(-x)))


def _rms_rows(x, g):
    return x * lax.rsqrt(jnp.mean(x * x, axis=-1, keepdims=True) + EPS) * g


def _dot(a, b):
    return jnp.dot(a, b, preferred_element_type=F32)


def _dot_nt(a, b):
    return lax.dot_general(a, b, (((1,), (1,)), ((), ())), preferred_element_type=F32)


def _dot_tn(a, b):
    return lax.dot_general(a, b, (((0,), (0,)), ((), ())), preferred_element_type=F32)


def _step_tile(i):
    return i


def _row_spec(width, col_block=0, tile=_step_tile, tm=TM):
    return pl.BlockSpec((tm, width), lambda i: (tile(i), col_block))


def _ctx_row_spec(width, tile=_step_tile, tm=TM):
    return pl.BlockSpec((tm, width), lambda i: (jnp.minimum(tile(i), T_CTX // tm - 1), 0))


def _lat_row_spec(width, tile=_step_tile, tm=TM):
    return pl.BlockSpec((tm, width), lambda i: (jnp.maximum(tile(i) - T_CTX // tm, 0), 0))


def _mod_spec(layer, tile=_step_tile, tm=TM):
    return pl.BlockSpec((None, None, N_MOD, D), lambda i: (layer, _tile_group(tile(i), tm), 0, 0))


def _resident(shape, index_map):
    return pl.BlockSpec(shape, index_map, pipeline_mode=pl.Buffered(1))


def _mods_kernel(c_ref, w_ref, b_ref, o_ref):
    a = _silu(c_ref[...]).astype(BF16)
    o_ref[...] = _dot(a, w_ref[...].astype(BF16)) + b_ref[...]


def _mods(cond8, ada_w, ada_b):
    tn = 1024
    return pl.pallas_call(
        _mods_kernel,
        grid=(DEPTH, N_MOD * D // tn),
        in_specs=[
            pl.BlockSpec((8, D), lambda l, j: (0, 0)),
            pl.BlockSpec((None, D, tn), lambda l, j: (l, 0, j)),
            pl.BlockSpec((None, 1, tn), lambda l, j: (l, 0, j)),
        ],
        out_specs=pl.BlockSpec((None, 8, tn), lambda l, j: (l, 0, j)),
        out_shape=jax.ShapeDtypeStruct((DEPTH, 8, N_MOD * D), F32),
        compiler_params=_cparams(("arbitrary", "arbitrary")),
        name="ada_mods",
    )(cond8, ada_w, ada_b.reshape(DEPTH, 1, N_MOD * D))


def _mixer_out(mod_ref, ctx, hf_ref, hb_ref, hgate_ref, gn_ref, ones_ref, ac_ref, al_ref, fc_ref, fl_ref, ga_ref,
               w_ref):
    gate = mod_ref[5:6, :]
    oh = hf_ref[...].astype(F32) + hb_ref[...].astype(F32)
    ms = _dot((oh * oh).astype(BF16), ones_ref[...])
    oh = oh * lax.rsqrt(ms + EPS) * gn_ref[...] * _silu(hgate_ref[...])
    oa = _rms_rows(jnp.where(ctx, ac_ref[...], al_ref[...]).astype(F32), ga_ref[...])
    of = jnp.where(ctx, fc_ref[...], fl_ref[...])
    o = (_dot(oh.astype(BF16), w_ref[0:W_HGRN, :])
         + _dot(oa.astype(BF16), w_ref[W_HGRN:W_HGRN + W_MLA, :])
         + _dot(of.astype(BF16), w_ref[W_HGRN + W_MLA:, :]))
    return gate * o


def _outproj_kernel(mod_ref, *rest):
    o_ref = rest[-1]
    o_ref[...] = _mixer_out(mod_ref, pl.program_id(0) < N_CTX_TILES, *rest[:-1]).astype(BF16)


def _outproj(mods, h_fwd, h_bwd, hg, gn_row, att_ctx, att_lat, fn_ctx, fn_lat, g_att, w_out, layer):
    lane = np.arange(W_HGRN)
    ones_bd = jnp.asarray((lane[:, None] // DV_HGRN == lane[None, :] // DV_HGRN).astype(np.float32) / DV_HGRN, BF16)
    return pl.pallas_call(
        _outproj_kernel,
        grid=(N_TILES,),
        in_specs=[_mod_spec(layer), _row_spec(W_HGRN), _row_spec(W_HGRN), _row_spec(W_HGRN, 4),
                  pl.BlockSpec((1, W_HGRN), lambda i: (0, 0)), pl.BlockSpec((W_HGRN, W_HGRN), lambda i: (0, 0)),
                  _ctx_row_spec(W_MLA), _lat_row_spec(W_MLA), _ctx_row_spec(W_FNET), _lat_row_spec(W_FNET),
                  pl.BlockSpec((None, 1, W_MLA), lambda i: (layer, 0, 0)),
                  _resident((None, D, D), lambda i: (layer, 0, 0))],
        out_specs=_row_spec(D),
        out_shape=jax.ShapeDtypeStruct((T_ALL, D), BF16),
        compiler_params=_cparams(("arbitrary",)),
        name=f"outproj_l{layer}",
    )(mods, h_fwd, h_bwd, hg, gn_row, ones_bd, att_ctx, att_lat, fn_ctx, fn_lat,
      g_att.reshape(DEPTH, 1, W_MLA), w_out)


def _ffn_kernel(*refs, mi, two_sources, with_delta):
    if two_sources:
        xc_ref, xl_ref, mod_ref, g_ref, wg_ref, wu_ref, wd_ref, o_ref = refs
        x = jnp.where(pl.program_id(0) < T_CTX // FFN_TM, xc_ref[...], xl_ref[...])
    elif with_delta:
        x_ref, dx_ref, mod_ref, g_ref, wg_ref, wu_ref, wd_ref, o_ref = refs
        x = x_ref[...] + dx_ref[...].astype(F32)
    else:
        x_ref, mod_ref, g_ref, wg_ref, wu_ref, wd_ref, o_ref = refs
        x = x_ref[...]
    shift = mod_ref[mi:mi + 1, :]
    scale = mod_ref[mi + 1:mi + 2, :]
    gate = mod_ref[mi + 2:mi + 3, :]
    hb = (_rms_rows(x, g_ref[...]) * (1.0 + scale) + shift).astype(BF16)
    acc = jnp.zeros(x.shape, F32)
    for j in range(D_FF // FF_CHUNK):
        cs = slice(j * FF_CHUNK, (j + 1) * FF_CHUNK)
        a = _silu(_dot(hb, wg_ref[:, cs])) * _dot(hb, wu_ref[:, cs])
        acc = acc + _dot(a.astype(BF16), wd_ref[cs, :])
    o_ref[...] = x + 0.5 * gate * acc


def _ffn(xs, mods, norm_g, w_gu, w_down, layer, which, *, delta=None, src_row0=0, n_rows=T_ALL):
    mi = 0 if which == 0 else 6
    gi = 0 if which == 0 else 2
    tm = FFN_TM

    def tile(i):
        return i + src_row0 // tm
    two = isinstance(xs, tuple)
    if two:
        in_specs = [_ctx_row_spec(D, tm=tm), _lat_row_spec(D, tm=tm)]
        args = list(xs)
    else:
        in_specs = [_row_spec(D, tile=tile, tm=tm)]
        args = [xs]
        if delta is not None:
            in_specs.append(_row_spec(D, tile=tile, tm=tm))
            args.append(delta)
    in_specs += [
        _mod_spec(layer, tile, tm),
        pl.BlockSpec((None, None, 1, D), lambda i: (layer, gi, 0, 0)),
        _resident((None, None, D, D_FF), lambda i: (layer, which, 0, 0)),
        _resident((None, None, D, D_FF), lambda i: (layer, which, 0, 1)),
        _resident((None, None, D_FF, D), lambda i: (layer, which, 0, 0)),
    ]
    args += [mods, norm_g.reshape(DEPTH, 3, 1, D), w_gu, w_gu, w_down]
    return pl.pallas_call(
        functools.partial(_ffn_kernel, mi=mi, two_sources=two, with_delta=delta is not None),
        grid=(n_rows // tm,),
        in_specs=in_specs,
        out_specs=pl.BlockSpec((tm, D), lambda i: (i, 0)),
        out_shape=jax.ShapeDtypeStruct((n_rows, D), F32),
        compiler_params=_cparams(("arbitrary",)),
        name=f"ffn_l{layer}_{which}",
    )(*args)


def _mla_heads(qa, ka, va, kr, kr_sw, tabs, q_ref, k_ref, v_ref):
    cq, sq, ck, sk = tabs
    lane = lax.broadcasted_iota(jnp.int32, (1, W_HEADS), 1) % HEAD_PAD
    v_ref[...] = jnp.where(lane == D_V, 1.0, va).astype(BF16)
    k_rot = kr_sw * sk
    heads = [slice(h * HEAD_PAD, (h + 1) * HEAD_PAD) for h in range(H_MLA)]

    def inv_rms(xs):
        return [lax.rsqrt(jnp.sum(x * x, axis=-1, keepdims=True) * (1.0 / D_QK) + EPS) for x in xs]
    if qa is not None:
        q = [qa[:, hs] for hs in heads]
        q_sw = [qa[:, W_HEADS + hs.start:W_HEADS + hs.stop] for hs in heads]
        for hs, x, x_sw, rs in zip(heads, q, q_sw, inv_rms(q)):
            q_ref[:, hs] = (rs * (x * cq + x_sw * sq)).astype(BF16)
    k = [ka[:, hs] + kr for hs in heads]
    for hs, x, rs in zip(heads, k, inv_rms(k)):
        k_ref[:, hs] = (rs * (x * ck + k_rot)).astype(BF16)


def _inproj_kernel(x_ref, mod_ref, g_ref, w_ref, wt_ref, gq_ref, gkv_ref, wq_ref, wk_ref, wv_ref, cq_ref, sq_ref,
                   ck_ref, sk_ref, hg_ref, ckv_ref, kr_ref, uf_ref, q_ref, k_ref, v_ref):
    x = x_ref[...]
    shift = mod_ref[3:4, :]
    scale = mod_ref[4:5, :]
    hb = (_rms_rows(x, g_ref[...]) * (1.0 + scale) + shift).astype(BF16)
    ul = _dot(hb, w_ref[:, IN_HG:])
    cqn = _rms_rows(ul[:, :Q_RANK], gq_ref[...])
    ckvn = _rms_rows(ul[:, Q_RANK:], gkv_ref[...])
    ckv_ref[...] = ckvn
    ut = _dot(hb, wt_ref[...])
    kr = ut[:, 0:HEAD_PAD]
    kr_ref[...] = kr
    kr_sw = ut[:, HEAD_PAD:2 * HEAD_PAD]
    uf_ref[...] = ut[:, 2 * HEAD_PAD:]
    hg_ref[...] = _dot(hb, w_ref[:, :IN_HG])
    cb = ckvn.astype(BF16)
    _mla_heads(_dot(cqn.astype(BF16), wq_ref[...]), _dot(cb, wk_ref[...]), _dot(cb, wv_ref[...]), kr, kr_sw,
               (cq_ref[...], sq_ref[...], ck_ref[...], sk_ref[...]), q_ref, k_ref, v_ref)


def _rope_block(i):
    return jnp.where(i < N_CTX_TILES, 0, 1 + (i - N_CTX_TILES) % LAT_TILES_PER_SEQ)


def _inproj(x, mods, norm_g, w_in_b, w_tail, gq, gkv, wq2, wk_arr, wv_arr, tabs, layer):
    tab = pl.BlockSpec((None, None, TM, HEAD_PAD), lambda i: (layer, _rope_block(i), 0, 0))
    f32_widths = (IN_HG, KV_RANK, HEAD_PAD, W_FNET)
    return pl.pallas_call(
        _inproj_kernel,
        grid=(N_TILES,),
        in_specs=[
            _row_spec(D),
            _mod_spec(layer),
            pl.BlockSpec((None, None, 1, D), lambda i: (layer, 1, 0, 0)),
            _resident((None, D, IN_MAIN), lambda i: (layer, 0, 0)),
            _resident((None, D, IN_TAIL), lambda i: (layer, 0, 0)),
            pl.BlockSpec((None, 1, Q_RANK), lambda i: (layer, 0, 0)),
            pl.BlockSpec((None, 1, KV_RANK), lambda i: (layer, 0, 0)),
            _resident((None, Q_RANK, 2 * W_HEADS), lambda i: (layer, 0, 0)),
            _resident((None, KV_RANK, W_HEADS), lambda i: (layer, 0, 0)),
            _resident((None, KV_RANK, W_HEADS), lambda i: (layer, 0, 0)),
            tab, tab, tab, tab,
        ],
        out_specs=[_row_spec(w) for w in f32_widths] + [_row_spec(W_HEADS)] * 3,
        out_shape=[jax.ShapeDtypeStruct((T_ALL, w), F32) for w in f32_widths]
        + [jax.ShapeDtypeStruct((T_ALL, W_HEADS), BF16)] * 3,
        compiler_params=_cparams(("arbitrary",)),
        name=f"inproj_l{layer}",
    )(x, mods, norm_g.reshape(DEPTH, 3, 1, D), w_in_b, w_tail, gq.reshape(DEPTH, 1, Q_RANK),
      gkv.reshape(DEPTH, 1, KV_RANK), wq2, wk_arr, wv_arr, *tabs)


def _kv_cache_kernel(ckv_ref, kr_ref, krsw_ref, wk_ref, wv_ref, ck_ref, sk_ref, k_ref, v_ref):
    cb = ckv_ref[...].astype(BF16)
    _mla_heads(None, _dot(cb, wk_ref[...]), _dot(cb, wv_ref[...]), kr_ref[...], krsw_ref[...],
               (None, None, ck_ref[...], sk_ref[...]), None, k_ref, v_ref)


def _kv_cache(cache_ckv_l, cache_kr, cache_kr_sw, wk_arr, wv_arr, tabs, layer):
    assert PAST == TM
    blk = lambda w: pl.BlockSpec((None, PAST, w), lambda b: (b, 0, 0))
    tab = pl.BlockSpec((None, None, TM, HEAD_PAD), lambda b: (layer, 0, 0, 0))
    return pl.pallas_call(
        _kv_cache_kernel,
        grid=(N_LAT_SEQ,),
        in_specs=[blk(KV_RANK), blk(HEAD_PAD), blk(HEAD_PAD),
                  _resident((None, KV_RANK, W_HEADS), lambda b: (layer, 0, 0)),
                  _resident((None, KV_RANK, W_HEADS), lambda b: (layer, 0, 0)), tab, tab],
        out_specs=[blk(W_HEADS)] * 2,
        out_shape=[jax.ShapeDtypeStruct((N_LAT_SEQ, PAST, W_HEADS), BF16)] * 2,
        compiler_params=_cparams(("arbitrary",)),
        name=f"mla_kv_cache_l{layer}",
    )(cache_ckv_l, cache_kr, cache_kr_sw, wk_arr, wv_arr, tabs[2], tabs[3])


N_SEQ = N_CTX_SEQ + N_LAT_SEQ
BLK_PER_LAT = L_LAT // T_BLK
N_HGRN_STEPS = N_CTX_SEQ + N_LAT_SEQ * BLK_PER_LAT


def _hgrn_seq(i):
    return jnp.where(i < N_CTX_SEQ, i, N_CTX_SEQ + (i - N_CTX_SEQ) // BLK_PER_LAT)


def _hgrn_blk(i, reverse):
    j = (i - N_CTX_SEQ) % BLK_PER_LAT
    if reverse:
        j = BLK_PER_LAT - 1 - j
    lat = N_CTX_SEQ + ((i - N_CTX_SEQ) // BLK_PER_LAT) * BLK_PER_LAT + j
    return jnp.where(i < N_CTX_SEQ, i, lat)


def _hgrn_kernel(qf_ref, ff_ref, vf_ref, qb_ref, fb_ref, vb_ref, lb_ref, s0f_ref, s0b_ref,
                 tri_ref, tri4_ref, hm_ref, bd_ref, of_ref, ob_ref, sf_ref, sb_ref, stf_scr, stb_scr,
                 ks_scr, bs_scr, vs_scr, oi_scr):
    i = pl.program_id(0)
    first = jnp.logical_or(i < N_CTX_SEQ, (i - N_CTX_SEQ) % BLK_PER_LAT == 0)
    heads = [slice(h * DK_HGRN, (h + 1) * DK_HGRN) for h in range(H_HGRN)]

    @pl.when(first)
    def _():
        for s0_ref, scr in ((s0f_ref, stf_scr), (s0b_ref, stb_scr)):
            scr[...] = jnp.zeros(scr.shape, F32)
            for h, hs in enumerate(heads):
                scr[hs, hs] = s0_ref[h]

    lb = lb_ref[...]
    loglb = jnp.log(lb)
    log1mlb = jnp.log(1.0 - lb)
    n_chunks = T_BLK // CHUNK
    dirs = ((qf_ref, ff_ref, vf_ref, of_ref, stf_scr), (qb_ref, fb_ref, vb_ref, ob_ref, stb_scr))
    units = [(d, c if d == 0 else n_chunks - 1 - c) for c in range(n_chunks) for d in (0, 1)]
    rows = [slice(cc * CHUNK, (cc + 1) * CHUNK) for _, cc in units]
    end_row = (CHUNK - 1, 0)
    mid_row = (CHUNK // 2 - 1, CHUNK // 2)
    hm, bd = hm_ref[...], bd_ref[...]

    q = [_silu(dirs[d][0][r, :]) for (d, _), r in zip(units, rows)]
    v = [dirs[d][2][r, :] for (d, _), r in zip(units, rows)]
    g = []
    for (d, _), r in zip(units, rows):
        x = dirs[d][1][r, :]
        y = log1mlb[d:d + 1] + (jnp.minimum(x, 0.0) - jnp.log(1.0 + jnp.exp(-jnp.abs(x))))
        g.append(jnp.maximum(loglb[d:d + 1], y) + jnp.log(1.0 + jnp.exp(-jnp.abs(loglb[d:d + 1] - y))))
    kk = [1.0 - jnp.exp(gu) for gu in g]
    g_hi = [gu.astype(BF16) for gu in g]
    g_lo = [(gu - gh.astype(F32)).astype(BF16) for gu, gh in zip(g, g_hi)]
    b = [_dot(tri_ref[d], gh) + _dot(tri_ref[d], gl) for (d, _), gh, gl in zip(units, g_hi, g_lo)]
    b_end = [bu[end_row[d]:end_row[d] + 1, :] for (d, _), bu in zip(units, b)]

    def finish(o_intra):
        q_in = [(qu * jnp.exp(bu)).astype(BF16) for qu, bu in zip(q, b)]
        k_e = [(ku * jnp.exp(be - bu)).astype(BF16) for ku, bu, be in zip(kk, b, b_end)]
        decay = [jnp.exp(be) for be in b_end]
        kv = [_dot_tn(vu.astype(BF16), ku) * bd for vu, ku in zip(v, k_e)]
        st = [stf_scr[...], stb_scr[...]]
        for u, (d, _) in enumerate(units):
            dirs[d][3][rows[u], :] = (o_intra[u] + _dot_nt(q_in[u], st[d].astype(BF16))).astype(BF16)
            st[d] = st[d] * decay[u] + kv[u]
        stf_scr[...] = st[0]
        stb_scr[...] = st[1]
        for h, hs in enumerate(heads):
            sf_ref[h] = st[0][hs, hs]
            sb_ref[h] = st[1][hs, hs]

    safe = jnp.max(jnp.concatenate([jnp.abs(be) for be in b_end], axis=0)) <= HGRN_SAFE_LOG_DECAY

    @pl.when(safe)
    def _():
        b_mid = [bu[mid_row[d]:mid_row[d] + 1, :] for (d, _), bu in zip(units, b)]
        q_t = [(qu * jnp.exp(bu - bm)).astype(BF16) for qu, bu, bm in zip(q, b, b_mid)]
        k_t = [ku * jnp.exp(bm - bu) for ku, bu, bm in zip(kk, b, b_mid)]
        k_bd = [jnp.concatenate([ku.astype(BF16)] * H_HGRN, axis=0) * hm for ku in k_t]
        v_bd = [jnp.concatenate([vu.astype(BF16)] * H_HGRN, axis=0) * hm for vu in v]
        sc = [(_dot_nt(qu, ku) * tri4_ref[d]).astype(BF16) for (d, _), qu, ku in zip(units, q_t, k_bd)]
        finish([_dot(su, vu) for su, vu in zip(sc, v_bd)])

    @pl.when(jnp.logical_not(safe))
    def _():
        t = lax.broadcasted_iota(jnp.int32, (CHUNK, 1), 0)
        head_sum = bd.astype(BF16)
        for u in range(len(units)):
            ks_scr[u] = kk[u]
            bs_scr[u] = b[u]
            vs_scr[u] = v[u]
            oi_scr[u] = jnp.zeros((CHUNK, W_HGRN), F32)

        def source_row(s, carry):
            for u, (d, _) in enumerate(units):
                seen = (t >= s) if d == 0 else (t <= s)
                e = jnp.exp(jnp.where(seen, b[u] - bs_scr[u, pl.ds(s, 1), :], -jnp.inf))
                p = (q[u] * ks_scr[u, pl.ds(s, 1), :] * e).astype(BF16)
                oi_scr[u] += _dot(p, head_sum) * vs_scr[u, pl.ds(s, 1), :]
            return carry
        lax.fori_loop(0, CHUNK, source_row, 0)
        finish([oi_scr[u] for u in range(len(units))])


def _hgrn_consts():
    t = np.arange(CHUNK)
    tri = np.stack([t[:, None] >= t[None, :], t[:, None] <= t[None, :]]).astype(np.float32)
    tri4 = np.tile(tri, (1, 1, H_HGRN))
    r = np.arange(H_HGRN * CHUNK)
    lane = np.arange(W_HGRN)
    hm = (r[:, None] // CHUNK == lane[None, :] // DK_HGRN).astype(np.float32)
    bd = (lane[:, None] // DV_HGRN == lane[None, :] // DK_HGRN).astype(np.float32)
    return jnp.asarray(tri, BF16), jnp.asarray(tri4), jnp.asarray(hm, BF16), jnp.asarray(bd)


def _hgrn(hg, lb2, s0f, s0b):
    tri, tri4, hm, bd = _hgrn_consts()

    def col(cb, reverse):
        return pl.BlockSpec((T_BLK, W_HGRN), lambda i: (_hgrn_blk(i, reverse), cb))

    def const(shape):
        return pl.BlockSpec(shape, lambda i: (0,) * len(shape))
    state = pl.BlockSpec((None, H_HGRN, DV_HGRN, DK_HGRN), lambda i: (_hgrn_seq(i), 0, 0, 0))
    return pl.pallas_call(
        _hgrn_kernel,
        grid=(N_HGRN_STEPS,),
        in_specs=[col(0, False), col(1, False), col(3, False), col(0, True), col(2, True), col(3, True),
                  const((2, W_HGRN)), state, state,
                  const(tri.shape), const(tri4.shape), const(hm.shape), const(bd.shape)],
        out_specs=[col(0, False), col(0, True), state, state],
        out_shape=[jax.ShapeDtypeStruct((T_ALL, W_HGRN), BF16)] * 2
        + [jax.ShapeDtypeStruct((N_SEQ, H_HGRN, DV_HGRN, DK_HGRN), F32)] * 2,
        scratch_shapes=[pltpu.VMEM((W_HGRN, W_HGRN), F32)] * 2
        + [pltpu.VMEM((2 * T_BLK // CHUNK, CHUNK, W_HGRN), F32)] * 4,
        compiler_params=_cparams(("arbitrary",)),
        name="hgrn",
    )(hg, hg, hg, hg, hg, hg, lb2, s0f, s0b, tri, tri4, hm, bd)


def _attn_kernel(q_ref, k_ref, v_ref, o_ref, *, n_seq=1):
    lane = lax.broadcasted_iota(jnp.int32, (1, HEAD_PAD), 1)
    lq, lk = q_ref.shape[0] // n_seq, k_ref.shape[0] // n_seq
    n_heads = q_ref.shape[1] // HEAD_PAD
    units = [(slice(b * lq, (b + 1) * lq), slice(b * lk, (b + 1) * lk), slice(h * HEAD_PAD, (h + 1) * HEAD_PAD))
             for b in range(n_seq) for h in range(n_heads)]
    s = [_dot_nt(q_ref[qr, hs], k_ref[kr, hs]) for qr, kr, hs in units]
    p = [jnp.exp2(sh - jnp.max(sh, axis=-1, keepdims=True)).astype(BF16) for sh in s]
    pv = [_dot(ph, v_ref[kr, hs]) for ph, (_, kr, hs) in zip(p, units)]
    o = [jnp.where(lane < D_V, x * (1.0 / x[:, D_V:D_V + 1]), 0.0) for x in pv]
    for b in range(n_seq):
        for hp in range(n_heads // 2):
            u = b * n_heads + 2 * hp
            o_ref[units[u][0], hp * 2 * D_V:(hp + 1) * 2 * D_V] = (
                o[u] + pltpu.roll(o[u + 1], D_V, axis=1)).astype(BF16)


def _attn_lat_kernel(q_ref, kp_ref, vp_ref, kn_ref, vn_ref, o_ref, k_scr, v_scr):
    @pl.when(pl.program_id(2) == 0)
    def _():
        k_scr[0:PAST, :] = kp_ref[...]
        k_scr[PAST:, :] = kn_ref[...]
        v_scr[0:PAST, :] = vp_ref[...]
        v_scr[PAST:, :] = vn_ref[...]
    for sub in range(TQ // TQ_SUB):
        rows = pl.ds(sub * TQ_SUB, TQ_SUB)
        _attn_kernel(q_ref.at[rows, :], k_scr, v_scr, o_ref.at[rows, :])


def _attention_ctx(q, k, v):
    rows = CTX_SEQ_PER_STEP * L_CTX
    blk = pl.BlockSpec((rows, W_HEADS), lambda b: (b, 0))
    return pl.pallas_call(
        functools.partial(_attn_kernel, n_seq=CTX_SEQ_PER_STEP),
        grid=(N_CTX_SEQ // CTX_SEQ_PER_STEP,),
        in_specs=[blk, blk, blk],
        out_specs=pl.BlockSpec((rows, W_MLA), lambda b: (b, 0)),
        out_shape=jax.ShapeDtypeStruct((T_CTX, W_MLA), BF16),
        compiler_params=_cparams(("arbitrary",)),
        name="attn_ctx",
    )(q, k, v)


def _attention_lat(q, k, v, k_past, v_past):
    grp = ATT_HEADS * HEAD_PAD
    nq = L_LAT // TQ
    q0 = T_CTX // TQ
    seq0 = T_CTX // L_LAT
    new = pl.BlockSpec((L_LAT, grp), lambda b, hp, qi: (seq0 + b, hp))
    past = pl.BlockSpec((None, PAST, grp), lambda b, hp, qi: (b, 0, hp))
    return pl.pallas_call(
        _attn_lat_kernel,
        grid=(N_LAT_SEQ, H_MLA // ATT_HEADS, nq),
        in_specs=[pl.BlockSpec((TQ, grp), lambda b, hp, qi: (q0 + b * nq + qi, hp)), past, past, new, new],
        out_specs=pl.BlockSpec((TQ, ATT_HEADS * D_V), lambda b, hp, qi: (b * nq + qi, hp)),
        out_shape=jax.ShapeDtypeStruct((T_LAT, W_MLA), BF16),
        scratch_shapes=[pltpu.VMEM((PAST + L_LAT, grp), BF16)] * 2,
        compiler_params=_cparams(("arbitrary", "arbitrary", "arbitrary")),
        name="attn_lat",
    )(q, k_past, v_past, k, v)


def _dft_tables(n):
    a = 2.0 * np.pi * np.outer(np.arange(n), np.arange(n)) / n
    return np.cos(a), np.sin(a)


def _bf16_operand(table):
    return jnp.asarray(table, F32).astype(BF16)


def _fnet_finish(re, im, cs_ref, w_ref, g_ref):
    spec = _dot(jnp.concatenate([re, im], axis=1).astype(BF16), cs_ref[...])
    return _rms_rows(_dot(spec.astype(BF16), w_ref[...]), g_ref[...])


def _fnet_channel_consts(w_bd, g_row):
    c, s = _dft_tables(C_FNET)
    eye = np.eye(G_FNET)
    return [_bf16_operand(np.concatenate([np.kron(eye, c), np.kron(eye, s)], axis=0)), w_bd, g_row]


def _fnet_channel_specs():
    zero = lambda *a: (0, 0)
    return [pl.BlockSpec((2 * W_FNET, W_FNET), zero), pl.BlockSpec((W_FNET, W_FNET), zero),
            pl.BlockSpec((1, W_FNET), zero)]


def _fnet_ctx_kernel(u_ref, m_ref, cs_ref, w_ref, g_ref, o_ref):
    p = [_dot(m_ref[...], u_ref[b * L_CTX:(b + 1) * L_CTX, :].astype(BF16)) for b in range(CTX_SEQ_PER_STEP)]
    o_ref[...] = _fnet_finish(jnp.concatenate([pb[:L_CTX] for pb in p], axis=0),
                              jnp.concatenate([pb[L_CTX:] for pb in p], axis=0), cs_ref, w_ref, g_ref)


def _fnet_ctx(uf, w_bd, g_row):
    c, s = _dft_tables(L_CTX)
    norm = 1.0 / np.sqrt(L_CTX * C_FNET)
    blk = pl.BlockSpec((CTX_SEQ_PER_STEP * L_CTX, W_FNET), lambda b: (b, 0))
    return pl.pallas_call(
        _fnet_ctx_kernel,
        grid=(N_CTX_SEQ // CTX_SEQ_PER_STEP,),
        in_specs=[blk, pl.BlockSpec((2 * L_CTX, L_CTX), lambda b: (0, 0))] + _fnet_channel_specs(),
        out_specs=blk,
        out_shape=jax.ShapeDtypeStruct((T_CTX, W_FNET), F32),
        compiler_params=_cparams(("arbitrary",)),
        name="fnet_ctx",
    )(uf, _bf16_operand(np.concatenate([c, -s], axis=0) * norm), *_fnet_channel_consts(w_bd, g_row))


FN_SUB = 16


def _fnet_rows_kernel(x_ref, m_ref, re_ref, im_ref):
    for s in range(FN_SUB):
        y = _dot(m_ref[s], x_ref[:, s, :].astype(BF16))
        re_ref[s] = y[:GRID_W]
        im_ref[s] = y[GRID_W:]


def _fnet_cols_kernel(re_ref, im_ref, m_ref, cs_ref, w_ref, g_ref, o_ref):
    z = [_dot(m_ref[...], jnp.concatenate([re_ref[:, s, :], im_ref[:, s, :]], axis=0).astype(BF16))
         for s in range(FN_SUB)]
    out = _fnet_finish(jnp.concatenate([zs[:GRID_W] for zs in z], axis=0),
                       jnp.concatenate([zs[GRID_W:] for zs in z], axis=0), cs_ref, w_ref, g_ref)
    for s in range(FN_SUB):
        o_ref[:, s, :] = out[s * GRID_W:(s + 1) * GRID_W]


def _fnet_lat(uf, w_bd, g_row):
    p1 = np.arange(GRID_W)[None, :, None]
    l = GRID_W * np.arange(GRID_W)[None, None, :] + np.arange(GRID_W)[:, None, None]
    ang = 2.0 * np.pi * p1 * l / L_LAT
    norm = 1.0 / np.sqrt(L_LAT * C_FNET)
    m_rows = _bf16_operand(np.concatenate([np.cos(ang), -np.sin(ang)], axis=1) * norm)
    c64, s64 = _dft_tables(GRID_W)
    m_cols = _bf16_operand(np.block([[c64, s64], [-s64, c64]]))
    x4 = uf.reshape(T_ALL // L_LAT, GRID_W, GRID_W, W_FNET)
    seq0 = T_CTX // L_LAT
    grid = (N_LAT_SEQ, GRID_W // FN_SUB)
    mid = pl.BlockSpec((None, FN_SUB, GRID_W, W_FNET), lambda b, j: (b, j, 0, 0))
    mid_shape = jax.ShapeDtypeStruct((N_LAT_SEQ, GRID_W, GRID_W, W_FNET), F32)
    re, im = pl.pallas_call(
        _fnet_rows_kernel,
        grid=grid,
        in_specs=[pl.BlockSpec((None, GRID_W, FN_SUB, W_FNET), lambda b, j: (seq0 + b, 0, j, 0)),
                  pl.BlockSpec((FN_SUB, 2 * GRID_W, GRID_W), lambda b, j: (j, 0, 0))],
        out_specs=[mid, mid],
        out_shape=[mid_shape, mid_shape],
        compiler_params=_cparams(("arbitrary", "arbitrary")),
        name="fnet_rows_lat",
    )(x4, m_rows)
    strided = pl.BlockSpec((None, GRID_W, FN_SUB, W_FNET), lambda b, j: (b, 0, j, 0))
    out = pl.pallas_call(
        _fnet_cols_kernel,
        grid=grid,
        in_specs=[strided, strided, pl.BlockSpec((2 * GRID_W, 2 * GRID_W), lambda b, j: (0, 0))]
        + _fnet_channel_specs(),
        out_specs=strided,
        out_shape=mid_shape,
        compiler_params=_cparams(("arbitrary", "arbitrary")),
        name="fnet_cols_lat",
    )(re, im, m_cols, *_fnet_channel_consts(w_bd, g_row))
    return out.reshape(T_LAT, W_FNET)


def _pad_heads(w, d_head):
    lead = w.shape[:-1]
    w = w.reshape(lead + (H_MLA, d_head))
    w = jnp.pad(w, [(0, 0)] * len(lead) + [(0, 0), (0, HEAD_PAD - d_head)])
    return w.reshape(lead + (W_HEADS,))


def _swap_rope(a):
    n = D_ROPE // 4
    parts = [jnp.zeros(a.shape[:-1] + (D_NOPE,), a.dtype)]
    for ax in range(2):
        base = D_NOPE + ax * 2 * n
        parts += [a[..., base + n:base + 2 * n], a[..., base:base + n]]
    parts.append(jnp.zeros(a.shape[:-1] + (HEAD_PAD - D_QK,), a.dtype))
    return jnp.concatenate(parts, axis=-1)


def _rope_tables(g_q, g_k):
    n_freq = D_ROPE // 4
    t = np.arange(L_LAT)
    pos = np.stack([t // GRID_W, t % GRID_W], axis=-1).astype(np.float32)
    freq = (np.float32(ROPE_THETA) ** (-np.arange(n_freq, dtype=np.float32) / n_freq)).astype(np.float32)
    ang = (pos[:, :, None] * freq).astype(np.float32)
    cos = np.ones((L_LAT, HEAD_PAD), np.float32)
    sin = np.zeros((L_LAT, HEAD_PAD), np.float32)
    for ax in range(2):
        base = D_NOPE + ax * 2 * n_freq
        c, s = np.cos(ang[:, ax, :]), np.sin(ang[:, ax, :])
        cos[:, base:base + n_freq] = c
        cos[:, base + n_freq:base + 2 * n_freq] = c
        sin[:, base:base + n_freq] = -s
        sin[:, base + n_freq:base + 2 * n_freq] = s
    cos = np.concatenate([np.ones((TM, HEAD_PAD), np.float32), cos]).reshape(1, 1 + LAT_TILES_PER_SEQ, TM, HEAD_PAD)
    sin = np.concatenate([np.zeros((TM, HEAD_PAD), np.float32), sin]).reshape(1, 1 + LAT_TILES_PER_SEQ, TM, HEAD_PAD)

    def pair(g, scale):
        g128 = jnp.pad(g, ((0, 0), (0, HEAD_PAD - D_QK))) * scale
        g_partner = _swap_rope(g128)
        return (cos * g128[:, None, None, :], sin * g_partner[:, None, None, :])
    return pair(g_q, D_QK ** -0.5 * LOG2_E) + pair(g_k, 1.0)


def kernel(x_prompt, x_sample, cache_ckv, cache_krope, state_hgrn, c, c_ctx, ada_w, ada_b, norm_g, ffn_w_gu, ffn_w_down, w_in, hgrn_lb, hgrn_norm_g, mla_q_norm_g, mla_w_q_up, mla_kv_norm_g, mla_w_kv_up, mla_qk_norm_g, mla_out_norm_g, fnet_w, fnet_norm_g, w_out):
    w_gu = ffn_w_gu.astype(BF16)
    w_down = ffn_w_down.astype(BF16)
    o = np.cumsum((0, 5 * W_HGRN, Q_RANK, KV_RANK, D_ROPE, W_FNET))
    assert o[3] == IN_MAIN
    w_in_b = w_in.astype(BF16)
    w_kr = jnp.pad(w_in[:, :, o[3]:o[4]], ((0, 0), (0, 0), (D_NOPE, HEAD_PAD - D_NOPE - D_ROPE)))
    w_tail = jnp.concatenate([w_kr, _swap_rope(w_kr), w_in[:, :, o[4]:]], axis=-1).astype(BF16)
    wq = _pad_heads(mla_w_q_up, D_QK)
    wq_sw = _swap_rope(wq.reshape(DEPTH, Q_RANK, H_MLA, HEAD_PAD)).reshape(DEPTH, Q_RANK, W_HEADS)
    wq2 = jnp.concatenate([wq, wq_sw], axis=-1).astype(BF16)
    w_kv = mla_w_kv_up.reshape(DEPTH, KV_RANK, H_MLA, D_NOPE + D_V)
    wk_arr = _pad_heads(w_kv[..., :D_NOPE].reshape(DEPTH, KV_RANK, H_MLA * D_NOPE), D_NOPE).astype(BF16)
    wv_arr = _pad_heads(w_kv[..., D_NOPE:].reshape(DEPTH, KV_RANK, H_MLA * D_V), D_V).astype(BF16)
    w_out_b = w_out.astype(BF16)
    eye_g = jnp.eye(G_FNET, dtype=F32)
    fnet_bd = jnp.einsum("lgcd,gh->lgchd", fnet_w, eye_g).reshape(DEPTH, W_FNET, W_FNET).astype(BF16)
    tabs = _rope_tables(mla_qk_norm_g[:, 0], mla_qk_norm_g[:, 1])
    lbs = jnp.cumsum(jax.nn.softmax(hgrn_lb.astype(F32), axis=0), axis=0)
    lbs = lbs - lbs[:1]

    cond8 = jnp.zeros((8, D), F32).at[0].set(c_ctx).at[1:1 + N_LAT_SEQ].set(c)
    mods = _mods(cond8, ada_w, ada_b).reshape(DEPTH, 8, N_MOD, D)

    x = (x_prompt.reshape(T_CTX, D), x_sample.reshape(T_LAT, D))
    ckv_out, kr_out, st_out = [], [], []
    zero_states = jnp.zeros((N_CTX_SEQ, H_HGRN, DV_HGRN, DK_HGRN), F32)
    for l in range(DEPTH):
        x = _ffn(x, mods, norm_g, w_gu, w_down, l, 0)
        hg, ckvn, kr128, uf, q, k, v = _inproj(x, mods, norm_g, w_in_b, w_tail, mla_q_norm_g, mla_kv_norm_g, wq2, wk_arr,
                                               wv_arr, tabs, l)

        s0 = [jnp.concatenate([zero_states, jnp.swapaxes(state_hgrn[:, l, d], -1, -2)], axis=0) for d in range(2)]
        h_fwd, h_bwd, s_f, s_b = _hgrn(hg, lbs[l], s0[0], s0[1])
        gn_row = jnp.tile(hgrn_norm_g[l], H_HGRN).reshape(1, W_HGRN)
        st_out.append(jnp.swapaxes(jnp.stack([s_f[:N_CTX_SEQ], s_b[:N_CTX_SEQ]], axis=1), -1, -2))

        cache_kr = jnp.pad(cache_krope[:, l], ((0, 0), (0, 0), (D_NOPE, HEAD_PAD - D_NOPE - D_ROPE)))
        k_past, v_past = _kv_cache(cache_ckv[:, l], cache_kr, _swap_rope(cache_kr), wk_arr, wv_arr, tabs, l)
        att_ctx = _attention_ctx(q, k, v)
        att_lat = _attention_lat(q, k, v, k_past, v_past)

        fn_gain = fnet_norm_g[l].reshape(1, W_FNET)
        fn_ctx = _fnet_ctx(uf, fnet_bd[l], fn_gain)
        fn_lat = _fnet_lat(uf, fnet_bd[l], fn_gain)

        dx = _outproj(mods, h_fwd, h_bwd, hg, gn_row, att_ctx, att_lat, fn_ctx, fn_lat, mla_out_norm_g, w_out_b, l)
        if l == DEPTH - 1:
            y_prompt = _ffn(x, mods, norm_g, w_gu, w_down, l, 1, delta=dx, n_rows=T_CTX)
            y_sample = _ffn(x, mods, norm_g, w_gu, w_down, l, 1, delta=dx, src_row0=T_CTX, n_rows=T_LAT)
        else:
            x = _ffn(x, mods, norm_g, w_gu, w_down, l, 1, delta=dx)

        ckv_out.append(ckvn[:T_CTX].reshape(N_CTX_SEQ, L_CTX, KV_RANK))
        kr_out.append(kr128[:T_CTX, D_NOPE:D_NOPE + D_ROPE].reshape(N_CTX_SEQ, L_CTX, D_ROPE))

    return (y_prompt.reshape(N_CTX_SEQ, L_CTX, D), y_sample.reshape(N_LAT_SEQ, L_LAT, D),
            jnp.stack(ckv_out, axis=1), jnp.stack(kr_out, axis=1), jnp.stack(st_out, axis=1))
```

```python
import functools

import numpy as np
import jax
import jax.numpy as jnp
from jax import lax
from jax.experimental import pallas as pl
from jax.experimental.pallas import tpu as pltpu

F32 = jnp.float32
BF16 = jnp.bfloat16

D = 1024
N_CTX_SEQ, L_CTX = 32, 256
N_LAT_SEQ, L_LAT = 2, 4096
T_CTX = N_CTX_SEQ * L_CTX
T_LAT = N_LAT_SEQ * L_LAT
T_ALL = T_CTX + T_LAT
DEPTH = 2
PAST = 512
GRID_W = 64
N_MOD = 9
EPS = 1e-6

H_HGRN, DK_HGRN, DV_HGRN = 4, 64, 64
W_HGRN = H_HGRN * DV_HGRN
CHUNK = 32
HGRN_SAFE_LOG_DECAY = 60.0
H_MLA, Q_RANK, KV_RANK = 8, 384, 256
D_NOPE, D_ROPE, D_V = 64, 32, 64
D_QK = D_NOPE + D_ROPE
HEAD_PAD = 128
W_HEADS = H_MLA * HEAD_PAD
W_MLA = H_MLA * D_V
G_FNET, C_FNET = 4, 64
W_FNET = G_FNET * C_FNET
D_FF = 2816
FF_CHUNK = 256
ROPE_THETA = 10000.0
LOG2_E = 1.4426950408889634

IN_HG = 5 * W_HGRN
IN_MAIN = IN_HG + Q_RANK + KV_RANK
IN_TAIL = 2 * HEAD_PAD + W_FNET

TM = 512
FFN_TM = 512
OUT_TM = 1024
N_TILES = T_ALL // TM
N_CTX_TILES = T_CTX // TM
LAT_TILES_PER_SEQ = L_LAT // TM
T_BLK = 256
TQ = 2048
TQ_SUB = 512
ATT_HEADS = 2
CTX_SEQ_PER_STEP = 8

VMEM_LIMIT = 56 * 1024 * 1024


def _cparams(sem):
    return pltpu.CompilerParams(dimension_semantics=sem, vmem_limit_bytes=VMEM_LIMIT)


def _tile_group(i, tm=TM):
    return (i >= T_CTX // tm).astype(jnp.int32) + (i >= (T_CTX + L_LAT) // tm).astype(jnp.int32)


def _silu(x):
    return x * (1.0 / (1.0 + jnp.exp(-x)))


def _rms_rows(x, g):
    return x * lax.rsqrt(jnp.mean(x * x, axis=-1, keepdims=True) + EPS) * g


def _dot(a, b):
    return jnp.dot(a, b, preferred_element_type=F32)


def _dot_nt(a, b):
    return lax.dot_general(a, b, (((1,), (1,)), ((), ())), preferred_element_type=F32)


def _dot_tn(a, b):
    return lax.dot_general(a, b, (((0,), (0,)), ((), ())), preferred_element_type=F32)


def _step_tile(i):
    return i


def _row_spec(width, col_block=0, tile=_step_tile, tm=TM):
    return pl.BlockSpec((tm, width), lambda i: (tile(i), col_block))


def _ctx_row_spec(width, tile=_step_tile, tm=TM):
    return pl.BlockSpec((tm, width), lambda i: (jnp.minimum(tile(i), T_CTX // tm - 1), 0))


def _lat_row_spec(width, tile=_step_tile, tm=TM):
    return pl.BlockSpec((tm, width), lambda i: (jnp.maximum(tile(i) - T_CTX // tm, 0), 0))


def _mod_spec(layer, tile=_step_tile, tm=TM):
    return pl.BlockSpec((None, None, N_MOD, D), lambda i: (layer, _tile_group(tile(i), tm), 0, 0))


def _resident(shape, index_map):
    return pl.BlockSpec(shape, index_map, pipeline_mode=pl.Buffered(1))


def _mods_kernel(c_ref, w_ref, b_ref, o_ref):
    a = _silu(c_ref[...]).astype(BF16)
    o_ref[...] = _dot(a, w_ref[...].astype(BF16)) + b_ref[...]


def _mods(cond8, ada_w, ada_b):
    tn = 1024
    return pl.pallas_call(
        _mods_kernel,
        grid=(DEPTH, N_MOD * D // tn),
        in_specs=[
            pl.BlockSpec((8, D), lambda l, j: (0, 0)),
            pl.BlockSpec((None, D, tn), lambda l, j: (l, 0, j)),
            pl.BlockSpec((None, 1, tn), lambda l, j: (l, 0, j)),
        ],
        out_specs=pl.BlockSpec((None, 8, tn), lambda l, j: (l, 0, j)),
        out_shape=jax.ShapeDtypeStruct((DEPTH, 8, N_MOD * D), F32),
        compiler_params=_cparams(("arbitrary", "arbitrary")),
        name="ada_mods",
    )(cond8, ada_w, ada_b.reshape(DEPTH, 1, N_MOD * D))


def _mixer_out(mod_ref, ctx, hf_ref, hb_ref, hgate_ref, gn_ref, ones_ref, ac_ref, al_ref, fc_ref, fl_ref, ga_ref,
               w_ref):
    gate = mod_ref[5:6, :]
    oh = hf_ref[...].astype(F32) + hb_ref[...].astype(F32)
    ms = _dot((oh * oh).astype(BF16), ones_ref[...])
    oh = oh * lax.rsqrt(ms + EPS) * gn_ref[...] * _silu(hgate_ref[...])
    oa = _rms_rows(jnp.where(ctx, ac_ref[...], al_ref[...]).astype(F32), ga_ref[...])
    of = jnp.where(ctx, fc_ref[...], fl_ref[...])
    o = (_dot(oh.astype(BF16), w_ref[0:W_HGRN, :])
         + _dot(oa.astype(BF16), w_ref[W_HGRN:W_HGRN + W_MLA, :])
         + _dot(of.astype(BF16), w_ref[W_HGRN + W_MLA:, :]))
    return gate * o


def _outproj_kernel(mod_ref, *rest):
    o_ref = rest[-1]
    o_ref[...] = _mixer_out(mod_ref, pl.program_id(0) < T_CTX // OUT_TM, *rest[:-1]).astype(BF16)


def _outproj(mods, h_fwd, h_bwd, hg, gn_row, att_ctx, att_lat, fn_ctx, fn_lat, g_att, w_out, layer):
    lane = np.arange(W_HGRN)
    ones_bd = jnp.asarray((lane[:, None] // DV_HGRN == lane[None, :] // DV_HGRN).astype(np.float32) / DV_HGRN, BF16)
    tm = OUT_TM
    return pl.pallas_call(
        _outproj_kernel,
        grid=(T_ALL // tm,),
        in_specs=[_mod_spec(layer, tm=tm), _row_spec(W_HGRN, tm=tm), _row_spec(W_HGRN, tm=tm),
                  _row_spec(W_HGRN, 4, tm=tm),
                  pl.BlockSpec((1, W_HGRN), lambda i: (0, 0)), pl.BlockSpec((W_HGRN, W_HGRN), lambda i: (0, 0)),
                  _ctx_row_spec(W_MLA, tm=tm), _lat_row_spec(W_MLA, tm=tm), _ctx_row_spec(W_FNET, tm=tm),
                  _lat_row_spec(W_FNET, tm=tm),
                  pl.BlockSpec((None, 1, W_MLA), lambda i: (layer, 0, 0)),
                  _resident((None, D, D), lambda i: (layer, 0, 0))],
        out_specs=_row_spec(D, tm=tm),
        out_shape=jax.ShapeDtypeStruct((T_ALL, D), BF16),
        compiler_params=_cparams(("arbitrary",)),
        name=f"outproj_l{layer}",
    )(mods, h_fwd, h_bwd, hg, gn_row, ones_bd, att_ctx, att_lat, fn_ctx, fn_lat,
      g_att.reshape(DEPTH, 1, W_MLA), w_out)


def _ffn_kernel(*refs, mi, two_sources, with_delta):
    if two_sources:
        xc_ref, xl_ref, mod_ref, g_ref, wg_ref, wu_ref, wd_ref, o_ref = refs
        x = jnp.where(pl.program_id(0) < T_CTX // FFN_TM, xc_ref[...], xl_ref[...])
    elif with_delta:
        x_ref, dx_ref, mod_ref, g_ref, wg_ref, wu_ref, wd_ref, o_ref = refs
        x = x_ref[...] + dx_ref[...].astype(F32)
    else:
        x_ref, mod_ref, g_ref, wg_ref, wu_ref, wd_ref, o_ref = refs
        x = x_ref[...]
    shift = mod_ref[mi:mi + 1, :]
    scale = mod_ref[mi + 1:mi + 2, :]
    gate = mod_ref[mi + 2:mi + 3, :]
    hb = (_rms_rows(x, g_ref[...]) * (1.0 + scale) + shift).astype(BF16)
    acc = jnp.zeros(x.shape, F32)
    for j in range(D_FF // FF_CHUNK):
        cs = slice(j * FF_CHUNK, (j + 1) * FF_CHUNK)
        a = _silu(_dot(hb, wg_ref[:, cs])) * _dot(hb, wu_ref[:, cs])
        acc = acc + _dot(a.astype(BF16), wd_ref[cs, :])
    o_ref[...] = x + 0.5 * gate * acc


def _ffn(xs, mods, norm_g, w_gu, w_down, layer, which, *, delta=None, src_row0=0, n_rows=T_ALL):
    mi = 0 if which == 0 else 6
    gi = 0 if which == 0 else 2
    tm = FFN_TM

    def tile(i):
        return i + src_row0 // tm
    two = isinstance(xs, tuple)
    if two:
        in_specs = [_ctx_row_spec(D, tm=tm), _lat_row_spec(D, tm=tm)]
        args = list(xs)
    else:
        in_specs = [_row_spec(D, tile=tile, tm=tm)]
        args = [xs]
        if delta is not None:
            in_specs.append(_row_spec(D, tile=tile, tm=tm))
            args.append(delta)
    in_specs += [
        _mod_spec(layer, tile, tm),
        pl.BlockSpec((None, None, 1, D), lambda i: (layer, gi, 0, 0)),
        _resident((None, None, D, D_FF), lambda i: (layer, which, 0, 0)),
        _resident((None, None, D, D_FF), lambda i: (layer, which, 0, 1)),
        _resident((None, None, D_FF, D), lambda i: (layer, which, 0, 0)),
    ]
    args += [mods, norm_g.reshape(DEPTH, 3, 1, D), w_gu, w_gu, w_down]
    return pl.pallas_call(
        functools.partial(_ffn_kernel, mi=mi, two_sources=two, with_delta=delta is not None),
        grid=(n_rows // tm,),
        in_specs=in_specs,
        out_specs=pl.BlockSpec((tm, D), lambda i: (i, 0)),
        out_shape=jax.ShapeDtypeStruct((n_rows, D), F32),
        compiler_params=_cparams(("arbitrary",)),
        name=f"ffn_l{layer}_{which}",
    )(*args)


def _mla_heads(qa, ka, va, kr, kr_sw, tabs, q_ref, k_ref, v_ref):
    cq, sq, ck, sk = tabs
    lane = lax.broadcasted_iota(jnp.int32, (1, W_HEADS), 1) % HEAD_PAD
    v_ref[...] = jnp.where(lane == D_V, 1.0, va).astype(BF16)
    k_rot = kr_sw * sk
    heads = [slice(h * HEAD_PAD, (h + 1) * HEAD_PAD) for h in range(H_MLA)]

    def inv_rms(xs):
        return [lax.rsqrt(jnp.sum(x * x, axis=-1, keepdims=True) * (1.0 / D_QK) + EPS) for x in xs]
    if qa is not None:
        q = [qa[:, hs] for hs in heads]
        q_sw = [qa[:, W_HEADS + hs.start:W_HEADS + hs.stop] for hs in heads]
        for hs, x, x_sw, rs in zip(heads, q, q_sw, inv_rms(q)):
            q_ref[:, hs] = (rs * (x * cq + x_sw * sq)).astype(BF16)
    k = [ka[:, hs] + kr for hs in heads]
    for hs, x, rs in zip(heads, k, inv_rms(k)):
        k_ref[:, hs] = (rs * (x * ck + k_rot)).astype(BF16)


def _inproj_kernel(x_ref, mod_ref, g_ref, w_ref, wt_ref, gq_ref, gkv_ref, wq_ref, wk_ref, wv_ref, cq_ref, sq_ref,
                   ck_ref, sk_ref, hg_ref, ckv_ref, kr_ref, uf_ref, q_ref, k_ref, v_ref):
    x = x_ref[...]
    shift = mod_ref[3:4, :]
    scale = mod_ref[4:5, :]
    hb = (_rms_rows(x, g_ref[...]) * (1.0 + scale) + shift).astype(BF16)
    ul = _dot(hb, w_ref[:, IN_HG:])
    cqn = _rms_rows(ul[:, :Q_RANK], gq_ref[...])
    ckvn = _rms_rows(ul[:, Q_RANK:], gkv_ref[...])
    ckv_ref[...] = ckvn
    ut = _dot(hb, wt_ref[...])
    kr = ut[:, 0:HEAD_PAD]
    kr_ref[...] = kr
    kr_sw = ut[:, HEAD_PAD:2 * HEAD_PAD]
    uf_ref[...] = ut[:, 2 * HEAD_PAD:]
    hg_ref[...] = _dot(hb, w_ref[:, :IN_HG])
    cb = ckvn.astype(BF16)
    _mla_heads(_dot(cqn.astype(BF16), wq_ref[...]), _dot(cb, wk_ref[...]), _dot(cb, wv_ref[...]), kr, kr_sw,
               (cq_ref[...], sq_ref[...], ck_ref[...], sk_ref[...]), q_ref, k_ref, v_ref)


def _rope_block(i):
    return jnp.where(i < N_CTX_TILES, 0, 1 + (i - N_CTX_TILES) % LAT_TILES_PER_SEQ)


def _inproj(x, mods, norm_g, w_in_b, w_tail, gq, gkv, wq2, wk_arr, wv_arr, tabs, layer):
    tab = pl.BlockSpec((None, None, TM, HEAD_PAD), lambda i: (layer, _rope_block(i), 0, 0))
    f32_widths = (IN_HG, KV_RANK, HEAD_PAD, W_FNET)
    return pl.pallas_call(
        _inproj_kernel,
        grid=(N_TILES,),
        in_specs=[
            _row_spec(D),
            _mod_spec(layer),
            pl.BlockSpec((None, None, 1, D), lambda i: (layer, 1, 0, 0)),
            _resident((None, D, IN_MAIN), lambda i: (layer, 0, 0)),
            _resident((None, D, IN_TAIL), lambda i: (layer, 0, 0)),
            pl.BlockSpec((None, 1, Q_RANK), lambda i: (layer, 0, 0)),
            pl.BlockSpec((None, 1, KV_RANK), lambda i: (layer, 0, 0)),
            _resident((None, Q_RANK, 2 * W_HEADS), lambda i: (layer, 0, 0)),
            _resident((None, KV_RANK, W_HEADS), lambda i: (layer, 0, 0)),
            _resident((None, KV_RANK, W_HEADS), lambda i: (layer, 0, 0)),
            tab, tab, tab, tab,
        ],
        out_specs=[_row_spec(w) for w in f32_widths] + [_row_spec(W_HEADS)] * 3,
        out_shape=[jax.ShapeDtypeStruct((T_ALL, w), F32) for w in f32_widths]
        + [jax.ShapeDtypeStruct((T_ALL, W_HEADS), BF16)] * 3,
        compiler_params=_cparams(("arbitrary",)),
        name=f"inproj_l{layer}",
    )(x, mods, norm_g.reshape(DEPTH, 3, 1, D), w_in_b, w_tail, gq.reshape(DEPTH, 1, Q_RANK),
      gkv.reshape(DEPTH, 1, KV_RANK), wq2, wk_arr, wv_arr, *tabs)


def _kv_cache_kernel(ckv_ref, kr_ref, krsw_ref, wk_ref, wv_ref, ck_ref, sk_ref, k_ref, v_ref):
    cb = ckv_ref[...].astype(BF16)
    _mla_heads(None, _dot(cb, wk_ref[...]), _dot(cb, wv_ref[...]), kr_ref[...], krsw_ref[...],
               (None, None, ck_ref[...], sk_ref[...]), None, k_ref, v_ref)


def _kv_cache(cache_ckv_l, cache_kr, cache_kr_sw, wk_arr, wv_arr, tabs, layer):
    assert PAST == TM
    blk = lambda w: pl.BlockSpec((None, PAST, w), lambda b: (b, 0, 0))
    tab = pl.BlockSpec((None, None, TM, HEAD_PAD), lambda b: (layer, 0, 0, 0))
    return pl.pallas_call(
        _kv_cache_kernel,
        grid=(N_LAT_SEQ,),
        in_specs=[blk(KV_RANK), blk(HEAD_PAD), blk(HEAD_PAD),
                  _resident((None, KV_RANK, W_HEADS), lambda b: (layer, 0, 0)),
                  _resident((None, KV_RANK, W_HEADS), lambda b: (layer, 0, 0)), tab, tab],
        out_specs=[blk(W_HEADS)] * 2,
        out_shape=[jax.ShapeDtypeStruct((N_LAT_SEQ, PAST, W_HEADS), BF16)] * 2,
        compiler_params=_cparams(("arbitrary",)),
        name=f"mla_kv_cache_l{layer}",
    )(cache_ckv_l, cache_kr, cache_kr_sw, wk_arr, wv_arr, tabs[2], tabs[3])


N_SEQ = N_CTX_SEQ + N_LAT_SEQ
BLK_PER_LAT = L_LAT // T_BLK
N_HGRN_STEPS = N_CTX_SEQ + N_LAT_SEQ * BLK_PER_LAT


def _hgrn_seq(i):
    return jnp.where(i < N_CTX_SEQ, i, N_CTX_SEQ + (i - N_CTX_SEQ) // BLK_PER_LAT)


def _hgrn_blk(i, reverse):
    j = (i - N_CTX_SEQ) % BLK_PER_LAT
    if reverse:
        j = BLK_PER_LAT - 1 - j
    lat = N_CTX_SEQ + ((i - N_CTX_SEQ) // BLK_PER_LAT) * BLK_PER_LAT + j
    return jnp.where(i < N_CTX_SEQ, i, lat)


def _hgrn_kernel(qf_ref, ff_ref, vf_ref, qb_ref, fb_ref, vb_ref, lb_ref, s0f_ref, s0b_ref,
                 tri_ref, tri4_ref, hm_ref, bd_ref, of_ref, ob_ref, sf_ref, sb_ref, stf_scr, stb_scr,
                 ks_scr, bs_scr, vs_scr, oi_scr):
    i = pl.program_id(0)
    first = jnp.logical_or(i < N_CTX_SEQ, (i - N_CTX_SEQ) % BLK_PER_LAT == 0)
    heads = [slice(h * DK_HGRN, (h + 1) * DK_HGRN) for h in range(H_HGRN)]

    @pl.when(first)
    def _():
        for s0_ref, scr in ((s0f_ref, stf_scr), (s0b_ref, stb_scr)):
            scr[...] = jnp.zeros(scr.shape, F32)
            for h, hs in enumerate(heads):
                scr[hs, hs] = s0_ref[h]

    lb = lb_ref[...]
    loglb = jnp.log(lb)
    log1mlb = jnp.log(1.0 - lb)
    n_chunks = T_BLK // CHUNK
    dirs = ((qf_ref, ff_ref, vf_ref, of_ref, stf_scr), (qb_ref, fb_ref, vb_ref, ob_ref, stb_scr))
    units = [(d, c if d == 0 else n_chunks - 1 - c) for c in range(n_chunks) for d in (0, 1)]
    rows = [slice(cc * CHUNK, (cc + 1) * CHUNK) for _, cc in units]
    end_row = (CHUNK - 1, 0)
    mid_row = (CHUNK // 2 - 1, CHUNK // 2)
    hm, bd = hm_ref[...], bd_ref[...]

    q = [_silu(dirs[d][0][r, :]) for (d, _), r in zip(units, rows)]
    v = [dirs[d][2][r, :] for (d, _), r in zip(units, rows)]
    g = []
    for (d, _), r in zip(units, rows):
        x = dirs[d][1][r, :]
        y = log1mlb[d:d + 1] + (jnp.minimum(x, 0.0) - jnp.log(1.0 + jnp.exp(-jnp.abs(x))))
        g.append(jnp.maximum(loglb[d:d + 1], y) + jnp.log(1.0 + jnp.exp(-jnp.abs(loglb[d:d + 1] - y))))
    kk = [1.0 - jnp.exp(gu) for gu in g]
    g_hi = [gu.astype(BF16) for gu in g]
    g_lo = [(gu - gh.astype(F32)).astype(BF16) for gu, gh in zip(g, g_hi)]
    b = [_dot(tri_ref[d], gh) + _dot(tri_ref[d], gl) for (d, _), gh, gl in zip(units, g_hi, g_lo)]
    b_end = [bu[end_row[d]:end_row[d] + 1, :] for (d, _), bu in zip(units, b)]

    def finish(o_intra):
        q_in = [(qu * jnp.exp(bu)).astype(BF16) for qu, bu in zip(q, b)]
        k_e = [(ku * jnp.exp(be - bu)).astype(BF16) for ku, bu, be in zip(kk, b, b_end)]
        decay = [jnp.exp(be) for be in b_end]
        kv = [_dot_tn(vu.astype(BF16), ku) * bd for vu, ku in zip(v, k_e)]
        st = [stf_scr[...], stb_scr[...]]
        for u, (d, _) in enumerate(units):
            dirs[d][3][rows[u], :] = (o_intra[u] + _dot_nt(q_in[u], st[d].astype(BF16))).astype(BF16)
            st[d] = st[d] * decay[u] + kv[u]
        stf_scr[...] = st[0]
        stb_scr[...] = st[1]
        for h, hs in enumerate(heads):
            sf_ref[h] = st[0][hs, hs]
            sb_ref[h] = st[1][hs, hs]

    safe = jnp.max(jnp.concatenate([jnp.abs(be) for be in b_end], axis=0)) <= HGRN_SAFE_LOG_DECAY

    @pl.when(safe)
    def _():
        b_mid = [bu[mid_row[d]:mid_row[d] + 1, :] for (d, _), bu in zip(units, b)]
        q_t = [(qu * jnp.exp(bu - bm)).astype(BF16) for qu, bu, bm in zip(q, b, b_mid)]
        k_t = [ku * jnp.exp(bm - bu) for ku, bu, bm in zip(kk, b, b_mid)]
        k_bd = [jnp.concatenate([ku.astype(BF16)] * H_HGRN, axis=0) * hm for ku in k_t]
        v_bd = [jnp.concatenate([vu.astype(BF16)] * H_HGRN, axis=0) * hm for vu in v]
        sc = [(_dot_nt(qu, ku) * tri4_ref[d]).astype(BF16) for (d, _), qu, ku in zip(units, q_t, k_bd)]
        finish([_dot(su, vu) for su, vu in zip(sc, v_bd)])

    @pl.when(jnp.logical_not(safe))
    def _():
        t = lax.broadcasted_iota(jnp.int32, (CHUNK, 1), 0)
        head_sum = bd.astype(BF16)
        for u in range(len(units)):
            ks_scr[u] = kk[u]
            bs_scr[u] = b[u]
            vs_scr[u] = v[u]
            oi_scr[u] = jnp.zeros((CHUNK, W_HGRN), F32)

        def source_row(s, carry):
            for u, (d, _) in enumerate(units):
                seen = (t >= s) if d == 0 else (t <= s)
                e = jnp.exp(jnp.where(seen, b[u] - bs_scr[u, pl.ds(s, 1), :], -jnp.inf))
                p = (q[u] * ks_scr[u, pl.ds(s, 1), :] * e).astype(BF16)
                oi_scr[u] += _dot(p, head_sum) * vs_scr[u, pl.ds(s, 1), :]
            return carry
        lax.fori_loop(0, CHUNK, source_row, 0)
        finish([oi_scr[u] for u in range(len(units))])


def _hgrn_consts():
    t = np.arange(CHUNK)
    tri = np.stack([t[:, None] >= t[None, :], t[:, None] <= t[None, :]]).astype(np.float32)
    tri4 = np.tile(tri, (1, 1, H_HGRN))
    r = np.arange(H_HGRN * CHUNK)
    lane = np.arange(W_HGRN)
    hm = (r[:, None] // CHUNK == lane[None, :] // DK_HGRN).astype(np.float32)
    bd = (lane[:, None] // DV_HGRN == lane[None, :] // DK_HGRN).astype(np.float32)
    return jnp.asarray(tri, BF16), jnp.asarray(tri4), jnp.asarray(hm, BF16), jnp.asarray(bd)


def _hgrn(hg, lb2, s0f, s0b):
    tri, tri4, hm, bd = _hgrn_consts()

    def col(cb, reverse):
        return pl.BlockSpec((T_BLK, W_HGRN), lambda i: (_hgrn_blk(i, reverse), cb))

    def const(shape):
        return pl.BlockSpec(shape, lambda i: (0,) * len(shape))
    state = pl.BlockSpec((None, H_HGRN, DV_HGRN, DK_HGRN), lambda i: (_hgrn_seq(i), 0, 0, 0))
    return pl.pallas_call(
        _hgrn_kernel,
        grid=(N_HGRN_STEPS,),
        in_specs=[col(0, False), col(1, False), col(3, False), col(0, True), col(2, True), col(3, True),
                  const((2, W_HGRN)), state, state,
                  const(tri.shape), const(tri4.shape), const(hm.shape), const(bd.shape)],
        out_specs=[col(0, False), col(0, True), state, state],
        out_shape=[jax.ShapeDtypeStruct((T_ALL, W_HGRN), BF16)] * 2
        + [jax.ShapeDtypeStruct((N_SEQ, H_HGRN, DV_HGRN, DK_HGRN), F32)] * 2,
        scratch_shapes=[pltpu.VMEM((W_HGRN, W_HGRN), F32)] * 2
        + [pltpu.VMEM((2 * T_BLK // CHUNK, CHUNK, W_HGRN), F32)] * 4,
        compiler_params=_cparams(("arbitrary",)),
        name="hgrn",
    )(hg, hg, hg, hg, hg, hg, lb2, s0f, s0b, tri, tri4, hm, bd)


def _attn_kernel(q_ref, k_ref, v_ref, o_ref, *, n_seq=1):
    lane = lax.broadcasted_iota(jnp.int32, (1, HEAD_PAD), 1)
    lq, lk = q_ref.shape[0] // n_seq, k_ref.shape[0] // n_seq
    n_heads = q_ref.shape[1] // HEAD_PAD
    units = [(slice(b * lq, (b + 1) * lq), slice(b * lk, (b + 1) * lk), slice(h * HEAD_PAD, (h + 1) * HEAD_PAD))
             for b in range(n_seq) for h in range(n_heads)]
    s = [_dot_nt(q_ref[qr, hs], k_ref[kr, hs]) for qr, kr, hs in units]
    p = [jnp.exp2(sh - jnp.max(sh, axis=-1, keepdims=True)).astype(BF16) for sh in s]
    pv = [_dot(ph, v_ref[kr, hs]) for ph, (_, kr, hs) in zip(p, units)]
    o = [jnp.where(lane < D_V, x * (1.0 / x[:, D_V:D_V + 1]), 0.0) for x in pv]
    for b in range(n_seq):
        for hp in range(n_heads // 2):
            u = b * n_heads + 2 * hp
            o_ref[units[u][0], hp * 2 * D_V:(hp + 1) * 2 * D_V] = (
                o[u] + pltpu.roll(o[u + 1], D_V, axis=1)).astype(BF16)


def _attn_lat_kernel(q_ref, kp_ref, vp_ref, kn_ref, vn_ref, o_ref, k_scr, v_scr):
    @pl.when(pl.program_id(2) == 0)
    def _():
        k_scr[0:PAST, :] = kp_ref[...]
        k_scr[PAST:, :] = kn_ref[...]
        v_scr[0:PAST, :] = vp_ref[...]
        v_scr[PAST:, :] = vn_ref[...]
    for sub in range(TQ // TQ_SUB):
        rows = pl.ds(sub * TQ_SUB, TQ_SUB)
        _attn_kernel(q_ref.at[rows, :], k_scr, v_scr, o_ref.at[rows, :])


def _attention_ctx(q, k, v):
    rows = CTX_SEQ_PER_STEP * L_CTX
    blk = pl.BlockSpec((rows, W_HEADS), lambda b: (b, 0))
    return pl.pallas_call(
        functools.partial(_attn_kernel, n_seq=CTX_SEQ_PER_STEP),
        grid=(N_CTX_SEQ // CTX_SEQ_PER_STEP,),
        in_specs=[blk, blk, blk],
        out_specs=pl.BlockSpec((rows, W_MLA), lambda b: (b, 0)),
        out_shape=jax.ShapeDtypeStruct((T_CTX, W_MLA), BF16),
        compiler_params=_cparams(("arbitrary",)),
        name="attn_ctx",
    )(q, k, v)


def _attention_lat(q, k, v, k_past, v_past):
    grp = ATT_HEADS * HEAD_PAD
    nq = L_LAT // TQ
    q0 = T_CTX // TQ
    seq0 = T_CTX // L_LAT
    new = pl.BlockSpec((L_LAT, grp), lambda b, hp, qi: (seq0 + b, hp))
    past = pl.BlockSpec((None, PAST, grp), lambda b, hp, qi: (b, 0, hp))
    return pl.pallas_call(
        _attn_lat_kernel,
        grid=(N_LAT_SEQ, H_MLA // ATT_HEADS, nq),
        in_specs=[pl.BlockSpec((TQ, grp), lambda b, hp, qi: (q0 + b * nq + qi, hp)), past, past, new, new],
        out_specs=pl.BlockSpec((TQ, ATT_HEADS * D_V), lambda b, hp, qi: (b * nq + qi, hp)),
        out_shape=jax.ShapeDtypeStruct((T_LAT, W_MLA), BF16),
        scratch_shapes=[pltpu.VMEM((PAST + L_LAT, grp), BF16)] * 2,
        compiler_params=_cparams(("arbitrary", "arbitrary", "arbitrary")),
        name="attn_lat",
    )(q, k_past, v_past, k, v)


def _dft_tables(n):
    a = 2.0 * np.pi * np.outer(np.arange(n), np.arange(n)) / n
    return np.cos(a), np.sin(a)


def _bf16_operand(table):
    return jnp.asarray(table, F32).astype(BF16)


def _fnet_finish(re, im, cs_ref, w_ref, g_ref):
    spec = _dot(jnp.concatenate([re, im], axis=1).astype(BF16), cs_ref[...])
    return _rms_rows(_dot(spec.astype(BF16), w_ref[...]), g_ref[...])


def _fnet_channel_consts(w_bd, g_row):
    c, s = _dft_tables(C_FNET)
    eye = np.eye(G_FNET)
    return [_bf16_operand(np.concatenate([np.kron(eye, c), np.kron(eye, s)], axis=0)), w_bd, g_row]


def _fnet_channel_specs():
    zero = lambda *a: (0, 0)
    return [pl.BlockSpec((2 * W_FNET, W_FNET), zero), pl.BlockSpec((W_FNET, W_FNET), zero),
            pl.BlockSpec((1, W_FNET), zero)]


def _fnet_ctx_kernel(u_ref, m_ref, cs_ref, w_ref, g_ref, o_ref):
    p = [_dot(m_ref[...], u_ref[b * L_CTX:(b + 1) * L_CTX, :].astype(BF16)) for b in range(CTX_SEQ_PER_STEP)]
    o_ref[...] = _fnet_finish(jnp.concatenate([pb[:L_CTX] for pb in p], axis=0),
                              jnp.concatenate([pb[L_CTX:] for pb in p], axis=0), cs_ref, w_ref, g_ref)


def _fnet_ctx(uf, w_bd, g_row):
    c, s = _dft_tables(L_CTX)
    norm = 1.0 / np.sqrt(L_CTX * C_FNET)
    blk = pl.BlockSpec((CTX_SEQ_PER_STEP * L_CTX, W_FNET), lambda b: (b, 0))
    return pl.pallas_call(
        _fnet_ctx_kernel,
        grid=(N_CTX_SEQ // CTX_SEQ_PER_STEP,),
        in_specs=[blk, pl.BlockSpec((2 * L_CTX, L_CTX), lambda b: (0, 0))] + _fnet_channel_specs(),
        out_specs=blk,
        out_shape=jax.ShapeDtypeStruct((T_CTX, W_FNET), F32),
        compiler_params=_cparams(("arbitrary",)),
        name="fnet_ctx",
    )(uf, _bf16_operand(np.concatenate([c, -s], axis=0) * norm), *_fnet_channel_consts(w_bd, g_row))


FN_SUB = 32


def _fnet_rows_kernel(x_ref, m_ref, re_ref, im_ref):
    for s in range(FN_SUB):
        y = _dot(m_ref[s], x_ref[:, s, :].astype(BF16))
        re_ref[s] = y[:GRID_W]
        im_ref[s] = y[GRID_W:]


def _fnet_cols_kernel(re_ref, im_ref, m_ref, cs_ref, w_ref, g_ref, o_ref):
    z = [_dot(m_ref[...], jnp.concatenate([re_ref[:, s, :], im_ref[:, s, :]], axis=0).astype(BF16))
         for s in range(FN_SUB)]
    out = _fnet_finish(jnp.concatenate([zs[:GRID_W] for zs in z], axis=0),
                       jnp.concatenate([zs[GRID_W:] for zs in z], axis=0), cs_ref, w_ref, g_ref)
    for s in range(FN_SUB):
        o_ref[:, s, :] = out[s * GRID_W:(s + 1) * GRID_W]


def _fnet_lat(uf, w_bd, g_row):
    p1 = np.arange(GRID_W)[None, :, None]
    l = GRID_W * np.arange(GRID_W)[None, None, :] + np.arange(GRID_W)[:, None, None]
    ang = 2.0 * np.pi * p1 * l / L_LAT
    norm = 1.0 / np.sqrt(L_LAT * C_FNET)
    m_rows = _bf16_operand(np.concatenate([np.cos(ang), -np.sin(ang)], axis=1) * norm)
    c64, s64 = _dft_tables(GRID_W)
    m_cols = _bf16_operand(np.block([[c64, s64], [-s64, c64]]))
    x4 = uf.reshape(T_ALL // L_LAT, GRID_W, GRID_W, W_FNET)
    seq0 = T_CTX // L_LAT
    grid = (N_LAT_SEQ, GRID_W // FN_SUB)
    mid = pl.BlockSpec((None, FN_SUB, GRID_W, W_FNET), lambda b, j: (b, j, 0, 0))
    mid_shape = jax.ShapeDtypeStruct((N_LAT_SEQ, GRID_W, GRID_W, W_FNET), F32)
    re, im = pl.pallas_call(
        _fnet_rows_kernel,
        grid=grid,
        in_specs=[pl.BlockSpec((None, GRID_W, FN_SUB, W_FNET), lambda b, j: (seq0 + b, 0, j, 0)),
                  pl.BlockSpec((FN_SUB, 2 * GRID_W, GRID_W), lambda b, j: (j, 0, 0))],
        out_specs=[mid, mid],
        out_shape=[mid_shape, mid_shape],
        compiler_params=_cparams(("arbitrary", "arbitrary")),
        name="fnet_rows_lat",
    )(x4, m_rows)
    strided = pl.BlockSpec((None, GRID_W, FN_SUB, W_FNET), lambda b, j: (b, 0, j, 0))
    out = pl.pallas_call(
        _fnet_cols_kernel,
        grid=grid,
        in_specs=[strided, strided, pl.BlockSpec((2 * GRID_W, 2 * GRID_W), lambda b, j: (0, 0))]
        + _fnet_channel_specs(),
        out_specs=strided,
        out_shape=mid_shape,
        compiler_params=_cparams(("arbitrary", "arbitrary")),
        name="fnet_cols_lat",
    )(re, im, m_cols, *_fnet_channel_consts(w_bd, g_row))
    return out.reshape(T_LAT, W_FNET)


def _pad_heads(w, d_head):
    lead = w.shape[:-1]
    w = w.reshape(lead + (H_MLA, d_head))
    w = jnp.pad(w, [(0, 0)] * len(lead) + [(0, 0), (0, HEAD_PAD - d_head)])
    return w.reshape(lead + (W_HEADS,))


def _swap_rope(a):
    n = D_ROPE // 4
    parts = [jnp.zeros(a.shape[:-1] + (D_NOPE,), a.dtype)]
    for ax in range(2):
        base = D_NOPE + ax * 2 * n
        parts += [a[..., base + n:base + 2 * n], a[..., base:base + n]]
    parts.append(jnp.zeros(a.shape[:-1] + (HEAD_PAD - D_QK,), a.dtype))
    return jnp.concatenate(parts, axis=-1)


def _rope_tables(g_q, g_k):
    n_freq = D_ROPE // 4
    t = np.arange(L_LAT)
    pos = np.stack([t // GRID_W, t % GRID_W], axis=-1).astype(np.float32)
    freq = (np.float32(ROPE_THETA) ** (-np.arange(n_freq, dtype=np.float32) / n_freq)).astype(np.float32)
    ang = (pos[:, :, None] * freq).astype(np.float32)
    cos = np.ones((L_LAT, HEAD_PAD), np.float32)
    sin = np.zeros((L_LAT, HEAD_PAD), np.float32)
    for ax in range(2):
        base = D_NOPE + ax * 2 * n_freq
        c, s = np.cos(ang[:, ax, :]), np.sin(ang[:, ax, :])
        cos[:, base:base + n_freq] = c
        cos[:, base + n_freq:base + 2 * n_freq] = c
        sin[:, base:base + n_freq] = -s
        sin[:, base + n_freq:base + 2 * n_freq] = s
    cos = np.concatenate([np.ones((TM, HEAD_PAD), np.float32), cos]).reshape(1, 1 + LAT_TILES_PER_SEQ, TM, HEAD_PAD)
    sin = np.concatenate([np.zeros((TM, HEAD_PAD), np.float32), sin]).reshape(1, 1 + LAT_TILES_PER_SEQ, TM, HEAD_PAD)

    def pair(g, scale):
        g128 = jnp.pad(g, ((0, 0), (0, HEAD_PAD - D_QK))) * scale
        g_partner = _swap_rope(g128)
        return (cos * g128[:, None, None, :], sin * g_partner[:, None, None, :])
    return pair(g_q, D_QK ** -0.5 * LOG2_E) + pair(g_k, 1.0)


def kernel(x_prompt, x_sample, cache_ckv, cache_krope, state_hgrn, c, c_ctx, ada_w, ada_b, norm_g, ffn_w_gu, ffn_w_down, w_in, hgrn_lb, hgrn_norm_g, mla_q_norm_g, mla_w_q_up, mla_kv_norm_g, mla_w_kv_up, mla_qk_norm_g, mla_out_norm_g, fnet_w, fnet_norm_g, w_out):
    w_gu = ffn_w_gu.astype(BF16)
    w_down = ffn_w_down.astype(BF16)
    o = np.cumsum((0, 5 * W_HGRN, Q_RANK, KV_RANK, D_ROPE, W_FNET))
    assert o[3] == IN_MAIN
    w_in_b = w_in.astype(BF16)
    w_kr = jnp.pad(w_in[:, :, o[3]:o[4]], ((0, 0), (0, 0), (D_NOPE, HEAD_PAD - D_NOPE - D_ROPE)))
    w_tail = jnp.concatenate([w_kr, _swap_rope(w_kr), w_in[:, :, o[4]:]], axis=-1).astype(BF16)
    wq = _pad_heads(mla_w_q_up, D_QK)
    wq_sw = _swap_rope(wq.reshape(DEPTH, Q_RANK, H_MLA, HEAD_PAD)).reshape(DEPTH, Q_RANK, W_HEADS)
    wq2 = jnp.concatenate([wq, wq_sw], axis=-1).astype(BF16)
    w_kv = mla_w_kv_up.reshape(DEPTH, KV_RANK, H_MLA, D_NOPE + D_V)
    wk_arr = _pad_heads(w_kv[..., :D_NOPE].reshape(DEPTH, KV_RANK, H_MLA * D_NOPE), D_NOPE).astype(BF16)
    wv_arr = _pad_heads(w_kv[..., D_NOPE:].reshape(DEPTH, KV_RANK, H_MLA * D_V), D_V).astype(BF16)
    w_out_b = w_out.astype(BF16)
    eye_g = jnp.eye(G_FNET, dtype=F32)
    fnet_bd = jnp.einsum("lgcd,gh->lgchd", fnet_w, eye_g).reshape(DEPTH, W_FNET, W_FNET).astype(BF16)
    tabs = _rope_tables(mla_qk_norm_g[:, 0], mla_qk_norm_g[:, 1])
    lbs = jnp.cumsum(jax.nn.softmax(hgrn_lb.astype(F32), axis=0), axis=0)
    lbs = lbs - lbs[:1]

    cond8 = jnp.zeros((8, D), F32).at[0].set(c_ctx).at[1:1 + N_LAT_SEQ].set(c)
    mods = _mods(cond8, ada_w, ada_b).reshape(DEPTH, 8, N_MOD, D)

    x = (x_prompt.reshape(T_CTX, D), x_sample.reshape(T_LAT, D))
    ckv_out, kr_out, st_out = [], [], []
    zero_states = jnp.zeros((N_CTX_SEQ, H_HGRN, DV_HGRN, DK_HGRN), F32)
    for l in range(DEPTH):
        x = _ffn(x, mods, norm_g, w_gu, w_down, l, 0)
        hg, ckvn, kr128, uf, q, k, v = _inproj(x, mods, norm_g, w_in_b, w_tail, mla_q_norm_g, mla_kv_norm_g, wq2, wk_arr,
                                               wv_arr, tabs, l)

        s0 = [jnp.concatenate([zero_states, jnp.swapaxes(state_hgrn[:, l, d], -1, -2)], axis=0) for d in range(2)]
        h_fwd, h_bwd, s_f, s_b = _hgrn(hg, lbs[l], s0[0], s0[1])
        gn_row = jnp.tile(hgrn_norm_g[l], H_HGRN).reshape(1, W_HGRN)
        st_out.append(jnp.swapaxes(jnp.stack([s_f[:N_CTX_SEQ], s_b[:N_CTX_SEQ]], axis=1), -1, -2))

        cache_kr = jnp.pad(cache_krope[:, l], ((0, 0), (0, 0), (D_NOPE, HEAD_PAD - D_NOPE - D_ROPE)))
        k_past, v_past = _kv_cache(cache_ckv[:, l], cache_kr, _swap_rope(cache_kr), wk_arr, wv_arr, tabs, l)
        att_ctx = _attention_ctx(q, k, v)
        att_lat = _attention_lat(q, k, v, k_past, v_past)

        fn_gain = fnet_norm_g[l].reshape(1, W_FNET)
        fn_ctx = _fnet_ctx(uf, fnet_bd[l], fn_gain)
        fn_lat = _fnet_lat(uf, fnet_bd[l], fn_gain)

        dx = _outproj(mods, h_fwd, h_bwd, hg, gn_row, att_ctx, att_lat, fn_ctx, fn_lat, mla_out_norm_g, w_out_b, l)
        if l == DEPTH - 1:
            y_prompt = _ffn(x, mods, norm_g, w_gu, w_down, l, 1, delta=dx, n_rows=T_CTX)
            y_sample = _ffn(x, mods, norm_g, w_gu, w_down, l, 1, delta=dx, src_row0=T_CTX, n_rows=T_LAT)
        else:
            x = _ffn(x, mods, norm_g, w_gu, w_down, l, 1, delta=dx)

        ckv_out.append(ckvn[:T_CTX].reshape(N_CTX_SEQ, L_CTX, KV_RANK))
        kr_out.append(kr128[:T_CTX, D_NOPE:D_NOPE + D_ROPE].reshape(N_CTX_SEQ, L_CTX, D_ROPE))

    return (y_prompt.reshape(N_CTX_SEQ, L_CTX, D), y_sample.reshape(N_LAT_SEQ, L_LAT, D),
            jnp.stack(ckv_out, axis=1), jnp.stack(kr_out, axis=1), jnp.stack(st_out, axis=1))
```

```python
import functools

import numpy as np
import jax
import jax.numpy as jnp
from jax import lax
from jax.experimental import pallas as pl
from jax.experimental.pallas import tpu as pltpu

F32 = jnp.float32
BF16 = jnp.bfloat16

D = 1024
N_CTX_SEQ, L_CTX = 32, 256
N_LAT_SEQ, L_LAT = 2, 4096
T_CTX = N_CTX_SEQ * L_CTX
T_LAT = N_LAT_SEQ * L_LAT
T_ALL = T_CTX + T_LAT
DEPTH = 2
PAST = 512
GRID_W = 64
N_MOD = 9
EPS = 1e-6

H_HGRN, DK_HGRN, DV_HGRN = 4, 64, 64
W_HGRN = H_HGRN * DV_HGRN
CHUNK = 32
HGRN_SAFE_LOG_DECAY = 60.0
H_MLA, Q_RANK, KV_RANK = 8, 384, 256
D_NOPE, D_ROPE, D_V = 64, 32, 64
D_QK = D_NOPE + D_ROPE
HEAD_PAD = 128
W_HEADS = H_MLA * HEAD_PAD
W_MLA = H_MLA * D_V
G_FNET, C_FNET = 4, 64
W_FNET = G_FNET * C_FNET
D_FF = 2816
FF_CHUNK = 256
ROPE_THETA = 10000.0
LOG2_E = 1.4426950408889634

IN_HG = 5 * W_HGRN
IN_MAIN = IN_HG + Q_RANK + KV_RANK
IN_TAIL = 2 * HEAD_PAD + W_FNET

TM = 512
FFN_TM = 512
OUT_TM = 1024
N_TILES = T_ALL // TM
N_CTX_TILES = T_CTX // TM
LAT_TILES_PER_SEQ = L_LAT // TM
T_BLK = 256
TQ = 2048
TQ_SUB = 512
ATT_HEADS = 2
CTX_SEQ_PER_STEP = 8

VMEM_LIMIT = 56 * 1024 * 1024


def _cparams(sem):
    return pltpu.CompilerParams(dimension_semantics=sem, vmem_limit_bytes=VMEM_LIMIT)


def _tile_group(i, tm=TM):
    return (i >= T_CTX // tm).astype(jnp.int32) + (i >= (T_CTX + L_LAT) // tm).astype(jnp.int32)


def _silu(x):
    return x * (1.0 / (1.0 + jnp.exp(-x)))


def _rms_rows(x, g):
    return x * lax.rsqrt(jnp.mean(x * x, axis=-1, keepdims=True) + EPS) * g


def _dot(a, b):
    return jnp.dot(a, b, preferred_element_type=F32)


def _dot_nt(a, b):
    return lax.dot_general(a, b, (((1,), (1,)), ((), ())), preferred_element_type=F32)


def _dot_tn(a, b):
    return lax.dot_general(a, b, (((0,), (0,)), ((), ())), preferred_element_type=F32)


def _step_tile(i):
    return i


def _row_spec(width, col_block=0, tile=_step_tile, tm=TM):
    return pl.BlockSpec((tm, width), lambda i: (tile(i), col_block))


def _ctx_row_spec(width, tile=_step_tile, tm=TM):
    return pl.BlockSpec((tm, width), lambda i: (jnp.minimum(tile(i), T_CTX // tm - 1), 0))


def _lat_row_spec(width, tile=_step_tile, tm=TM):
    return pl.BlockSpec((tm, width), lambda i: (jnp.maximum(tile(i) - T_CTX // tm, 0), 0))


def _mod_spec(layer, tile=_step_tile, tm=TM):
    return pl.BlockSpec((None, None, N_MOD, D), lambda i: (layer, _tile_group(tile(i), tm), 0, 0))


def _resident(shape, index_map):
    return pl.BlockSpec(shape, index_map, pipeline_mode=pl.Buffered(1))


def _mods_kernel(c_ref, w_ref, b_ref, o_ref):
    a = _silu(c_ref[...]).astype(BF16)
    o_ref[...] = _dot(a, w_ref[...].astype(BF16)) + b_ref[...]


def _mods(cond8, ada_w, ada_b):
    tn = 1024
    return pl.pallas_call(
        _mods_kernel,
        grid=(DEPTH, N_MOD * D // tn),
        in_specs=[
            pl.BlockSpec((8, D), lambda l, j: (0, 0)),
            pl.BlockSpec((None, D, tn), lambda l, j: (l, 0, j)),
            pl.BlockSpec((None, 1, tn), lambda l, j: (l, 0, j)),
        ],
        out_specs=pl.BlockSpec((None, 8, tn), lambda l, j: (l, 0, j)),
        out_shape=jax.ShapeDtypeStruct((DEPTH, 8, N_MOD * D), F32),
        compiler_params=_cparams(("arbitrary", "arbitrary")),
        name="ada_mods",
    )(cond8, ada_w, ada_b.reshape(DEPTH, 1, N_MOD * D))


def _mixer_out(mod_ref, ctx, hf_ref, hb_ref, hgate_ref, gn_ref, ones_ref, ac_ref, al_ref, fc_ref, fl_ref, ga_ref,
               w_ref):
    gate = mod_ref[5:6, :]
    oh = hf_ref[...].astype(F32) + hb_ref[...].astype(F32)
    ms = _dot((oh * oh).astype(BF16), ones_ref[...])
    oh = oh * lax.rsqrt(ms + EPS) * gn_ref[...] * _silu(hgate_ref[...])
    oa = _rms_rows(jnp.where(ctx, ac_ref[...], al_ref[...]).astype(F32), ga_ref[...])
    of = jnp.where(ctx, fc_ref[...], fl_ref[...])
    o = (_dot(oh.astype(BF16), w_ref[0:W_HGRN, :])
         + _dot(oa.astype(BF16), w_ref[W_HGRN:W_HGRN + W_MLA, :])
         + _dot(of.astype(BF16), w_ref[W_HGRN + W_MLA:, :]))
    return gate * o


def _outproj_kernel(mod_ref, *rest):
    o_ref = rest[-1]
    o_ref[...] = _mixer_out(mod_ref, pl.program_id(0) < T_CTX // OUT_TM, *rest[:-1]).astype(BF16)


def _outproj(mods, h_fwd, h_bwd, hg, gn_row, att_ctx, att_lat, fn_ctx, fn_lat, g_att, w_out, layer):
    lane = np.arange(W_HGRN)
    ones_bd = jnp.asarray((lane[:, None] // DV_HGRN == lane[None, :] // DV_HGRN).astype(np.float32) / DV_HGRN, BF16)
    tm = OUT_TM
    return pl.pallas_call(
        _outproj_kernel,
        grid=(T_ALL // tm,),
        in_specs=[_mod_spec(layer, tm=tm), _row_spec(W_HGRN, tm=tm), _row_spec(W_HGRN, tm=tm),
                  _row_spec(W_HGRN, 4, tm=tm),
                  pl.BlockSpec((1, W_HGRN), lambda i: (0, 0)), pl.BlockSpec((W_HGRN, W_HGRN), lambda i: (0, 0)),
                  _ctx_row_spec(W_MLA, tm=tm), _lat_row_spec(W_MLA, tm=tm), _ctx_row_spec(W_FNET, tm=tm),
                  _lat_row_spec(W_FNET, tm=tm),
                  pl.BlockSpec((None, 1, W_MLA), lambda i: (layer, 0, 0)),
                  _resident((None, D, D), lambda i: (layer, 0, 0))],
        out_specs=_row_spec(D, tm=tm),
        out_shape=jax.ShapeDtypeStruct((T_ALL, D), BF16),
        compiler_params=_cparams(("parallel",)),
        name=f"outproj_l{layer}",
    )(mods, h_fwd, h_bwd, hg, gn_row, ones_bd, att_ctx, att_lat, fn_ctx, fn_lat,
      g_att.reshape(DEPTH, 1, W_MLA), w_out)


def _ffn_kernel(*refs, mi, two_sources, with_delta):
    if two_sources:
        xc_ref, xl_ref, mod_ref, g_ref, wg_ref, wu_ref, wd_ref, o_ref = refs
        x = jnp.where(pl.program_id(0) < T_CTX // FFN_TM, xc_ref[...], xl_ref[...])
    elif with_delta:
        x_ref, dx_ref, mod_ref, g_ref, wg_ref, wu_ref, wd_ref, o_ref = refs
        x = x_ref[...] + dx_ref[...].astype(F32)
    else:
        x_ref, mod_ref, g_ref, wg_ref, wu_ref, wd_ref, o_ref = refs
        x = x_ref[...]
    shift = mod_ref[mi:mi + 1, :]
    scale = mod_ref[mi + 1:mi + 2, :]
    gate = mod_ref[mi + 2:mi + 3, :]
    hb = (_rms_rows(x, g_ref[...]) * (1.0 + scale) + shift).astype(BF16)
    acc = jnp.zeros(x.shape, F32)
    for j in range(D_FF // FF_CHUNK):
        cs = slice(j * FF_CHUNK, (j + 1) * FF_CHUNK)
        a = _silu(_dot(hb, wg_ref[:, cs])) * _dot(hb, wu_ref[:, cs])
        acc = acc + _dot(a.astype(BF16), wd_ref[cs, :])
    o_ref[...] = x + 0.5 * gate * acc


def _ffn(xs, mods, norm_g, w_gu, w_down, layer, which, *, delta=None, src_row0=0, n_rows=T_ALL):
    mi = 0 if which == 0 else 6
    gi = 0 if which == 0 else 2
    tm = FFN_TM

    def tile(i):
        return i + src_row0 // tm
    two = isinstance(xs, tuple)
    if two:
        in_specs = [_ctx_row_spec(D, tm=tm), _lat_row_spec(D, tm=tm)]
        args = list(xs)
    else:
        in_specs = [_row_spec(D, tile=tile, tm=tm)]
        args = [xs]
        if delta is not None:
            in_specs.append(_row_spec(D, tile=tile, tm=tm))
            args.append(delta)
    in_specs += [
        _mod_spec(layer, tile, tm),
        pl.BlockSpec((None, None, 1, D), lambda i: (layer, gi, 0, 0)),
        _resident((None, None, D, D_FF), lambda i: (layer, which, 0, 0)),
        _resident((None, None, D, D_FF), lambda i: (layer, which, 0, 1)),
        _resident((None, None, D_FF, D), lambda i: (layer, which, 0, 0)),
    ]
    args += [mods, norm_g.reshape(DEPTH, 3, 1, D), w_gu, w_gu, w_down]
    return pl.pallas_call(
        functools.partial(_ffn_kernel, mi=mi, two_sources=two, with_delta=delta is not None),
        grid=(n_rows // tm,),
        in_specs=in_specs,
        out_specs=pl.BlockSpec((tm, D), lambda i: (i, 0)),
        out_shape=jax.ShapeDtypeStruct((n_rows, D), F32),
        compiler_params=_cparams(("parallel",)),
        name=f"ffn_l{layer}_{which}",
    )(*args)


def _mla_heads(qa, ka, va, kr, kr_sw, tabs, q_ref, k_ref, v_ref):
    cq, sq, ck, sk = tabs
    lane = lax.broadcasted_iota(jnp.int32, (1, W_HEADS), 1) % HEAD_PAD
    v_ref[...] = jnp.where(lane == D_V, 1.0, va).astype(BF16)
    k_rot = kr_sw * sk
    heads = [slice(h * HEAD_PAD, (h + 1) * HEAD_PAD) for h in range(H_MLA)]

    def inv_rms(xs):
        return [lax.rsqrt(jnp.sum(x * x, axis=-1, keepdims=True) * (1.0 / D_QK) + EPS) for x in xs]
    if qa is not None:
        q = [qa[:, hs] for hs in heads]
        q_sw = [qa[:, W_HEADS + hs.start:W_HEADS + hs.stop] for hs in heads]
        for hs, x, x_sw, rs in zip(heads, q, q_sw, inv_rms(q)):
            q_ref[:, hs] = (rs * (x * cq + x_sw * sq)).astype(BF16)
    k = [ka[:, hs] + kr for hs in heads]
    for hs, x, rs in zip(heads, k, inv_rms(k)):
        k_ref[:, hs] = (rs * (x * ck + k_rot)).astype(BF16)


def _inproj_kernel(x_ref, mod_ref, g_ref, w_ref, wt_ref, gq_ref, gkv_ref, wq_ref, wk_ref, wv_ref, cq_ref, sq_ref,
                   ck_ref, sk_ref, hg_ref, ckv_ref, kr_ref, uf_ref, q_ref, k_ref, v_ref):
    x = x_ref[...]
    shift = mod_ref[3:4, :]
    scale = mod_ref[4:5, :]
    hb = (_rms_rows(x, g_ref[...]) * (1.0 + scale) + shift).astype(BF16)
    ul = _dot(hb, w_ref[:, IN_HG:])
    cqn = _rms_rows(ul[:, :Q_RANK], gq_ref[...])
    ckvn = _rms_rows(ul[:, Q_RANK:], gkv_ref[...])
    ckv_ref[...] = ckvn
    ut = _dot(hb, wt_ref[...])
    kr = ut[:, 0:HEAD_PAD]
    kr_ref[...] = kr
    kr_sw = ut[:, HEAD_PAD:2 * HEAD_PAD]
    uf_ref[...] = ut[:, 2 * HEAD_PAD:]
    hg_ref[...] = _dot(hb, w_ref[:, :IN_HG])
    cb = ckvn.astype(BF16)
    _mla_heads(_dot(cqn.astype(BF16), wq_ref[...]), _dot(cb, wk_ref[...]), _dot(cb, wv_ref[...]), kr, kr_sw,
               (cq_ref[...], sq_ref[...], ck_ref[...], sk_ref[...]), q_ref, k_ref, v_ref)


def _rope_block(i):
    return jnp.where(i < N_CTX_TILES, 0, 1 + (i - N_CTX_TILES) % LAT_TILES_PER_SEQ)


def _inproj(x, mods, norm_g, w_in_b, w_tail, gq, gkv, wq2, wk_arr, wv_arr, tabs, layer):
    tab = pl.BlockSpec((None, None, TM, HEAD_PAD), lambda i: (layer, _rope_block(i), 0, 0))
    f32_widths = (IN_HG, KV_RANK, HEAD_PAD, W_FNET)
    return pl.pallas_call(
        _inproj_kernel,
        grid=(N_TILES,),
        in_specs=[
            _row_spec(D),
            _mod_spec(layer),
            pl.BlockSpec((None, None, 1, D), lambda i: (layer, 1, 0, 0)),
            _resident((None, D, IN_MAIN), lambda i: (layer, 0, 0)),
            _resident((None, D, IN_TAIL), lambda i: (layer, 0, 0)),
            pl.BlockSpec((None, 1, Q_RANK), lambda i: (layer, 0, 0)),
            pl.BlockSpec((None, 1, KV_RANK), lambda i: (layer, 0, 0)),
            _resident((None, Q_RANK, 2 * W_HEADS), lambda i: (layer, 0, 0)),
            _resident((None, KV_RANK, W_HEADS), lambda i: (layer, 0, 0)),
            _resident((None, KV_RANK, W_HEADS), lambda i: (layer, 0, 0)),
            tab, tab, tab, tab,
        ],
        out_specs=[_row_spec(w) for w in f32_widths] + [_row_spec(W_HEADS)] * 3,
        out_shape=[jax.ShapeDtypeStruct((T_ALL, w), F32) for w in f32_widths]
        + [jax.ShapeDtypeStruct((T_ALL, W_HEADS), BF16)] * 3,
        compiler_params=_cparams(("parallel",)),
        name=f"inproj_l{layer}",
    )(x, mods, norm_g.reshape(DEPTH, 3, 1, D), w_in_b, w_tail, gq.reshape(DEPTH, 1, Q_RANK),
      gkv.reshape(DEPTH, 1, KV_RANK), wq2, wk_arr, wv_arr, *tabs)


def _kv_cache_kernel(ckv_ref, kr_ref, krsw_ref, wk_ref, wv_ref, ck_ref, sk_ref, k_ref, v_ref):
    cb = ckv_ref[...].astype(BF16)
    _mla_heads(None, _dot(cb, wk_ref[...]), _dot(cb, wv_ref[...]), kr_ref[...], krsw_ref[...],
               (None, None, ck_ref[...], sk_ref[...]), None, k_ref, v_ref)


def _kv_cache(cache_ckv_l, cache_kr, cache_kr_sw, wk_arr, wv_arr, tabs, layer):
    assert PAST == TM
    blk = lambda w: pl.BlockSpec((None, PAST, w), lambda b: (b, 0, 0))
    tab = pl.BlockSpec((None, None, TM, HEAD_PAD), lambda b: (layer, 0, 0, 0))
    return pl.pallas_call(
        _kv_cache_kernel,
        grid=(N_LAT_SEQ,),
        in_specs=[blk(KV_RANK), blk(HEAD_PAD), blk(HEAD_PAD),
                  _resident((None, KV_RANK, W_HEADS), lambda b: (layer, 0, 0)),
                  _resident((None, KV_RANK, W_HEADS), lambda b: (layer, 0, 0)), tab, tab],
        out_specs=[blk(W_HEADS)] * 2,
        out_shape=[jax.ShapeDtypeStruct((N_LAT_SEQ, PAST, W_HEADS), BF16)] * 2,
        compiler_params=_cparams(("arbitrary",)),
        name=f"mla_kv_cache_l{layer}",
    )(cache_ckv_l, cache_kr, cache_kr_sw, wk_arr, wv_arr, tabs[2], tabs[3])


N_SEQ = N_CTX_SEQ + N_LAT_SEQ
BLK_PER_LAT = L_LAT // T_BLK
N_HGRN_STEPS = N_CTX_SEQ + N_LAT_SEQ * BLK_PER_LAT


def _hgrn_seq(i):
    return jnp.where(i < N_CTX_SEQ, i, N_CTX_SEQ + (i - N_CTX_SEQ) // BLK_PER_LAT)


def _hgrn_blk(i, reverse):
    j = (i - N_CTX_SEQ) % BLK_PER_LAT
    if reverse:
        j = BLK_PER_LAT - 1 - j
    lat = N_CTX_SEQ + ((i - N_CTX_SEQ) // BLK_PER_LAT) * BLK_PER_LAT + j
    return jnp.where(i < N_CTX_SEQ, i, lat)


def _hgrn_kernel(qf_ref, ff_ref, vf_ref, qb_ref, fb_ref, vb_ref, lb_ref, s0f_ref, s0b_ref,
                 tri_ref, tri4_ref, hm_ref, bd_ref, of_ref, ob_ref, sf_ref, sb_ref, stf_scr, stb_scr,
                 ks_scr, bs_scr, vs_scr, oi_scr):
    i = pl.program_id(0)
    first = jnp.logical_or(i < N_CTX_SEQ, (i - N_CTX_SEQ) % BLK_PER_LAT == 0)
    heads = [slice(h * DK_HGRN, (h + 1) * DK_HGRN) for h in range(H_HGRN)]

    @pl.when(first)
    def _():
        for s0_ref, scr in ((s0f_ref, stf_scr), (s0b_ref, stb_scr)):
            scr[...] = jnp.zeros(scr.shape, F32)
            for h, hs in enumerate(heads):
                scr[hs, hs] = s0_ref[h]

    lb = lb_ref[...]
    loglb = jnp.log(lb)
    log1mlb = jnp.log(1.0 - lb)
    n_chunks = T_BLK // CHUNK
    dirs = ((qf_ref, ff_ref, vf_ref, of_ref, stf_scr), (qb_ref, fb_ref, vb_ref, ob_ref, stb_scr))
    units = [(d, c if d == 0 else n_chunks - 1 - c) for c in range(n_chunks) for d in (0, 1)]
    rows = [slice(cc * CHUNK, (cc + 1) * CHUNK) for _, cc in units]
    end_row = (CHUNK - 1, 0)
    mid_row = (CHUNK // 2 - 1, CHUNK // 2)
    hm, bd = hm_ref[...], bd_ref[...]

    q = [_silu(dirs[d][0][r, :]) for (d, _), r in zip(units, rows)]
    v = [dirs[d][2][r, :] for (d, _), r in zip(units, rows)]
    g = []
    for (d, _), r in zip(units, rows):
        x = dirs[d][1][r, :]
        y = log1mlb[d:d + 1] + (jnp.minimum(x, 0.0) - jnp.log(1.0 + jnp.exp(-jnp.abs(x))))
        g.append(jnp.maximum(loglb[d:d + 1], y) + jnp.log(1.0 + jnp.exp(-jnp.abs(loglb[d:d + 1] - y))))
    kk = [1.0 - jnp.exp(gu) for gu in g]
    g_hi = [gu.astype(BF16) for gu in g]
    g_lo = [(gu - gh.astype(F32)).astype(BF16) for gu, gh in zip(g, g_hi)]
    b = [_dot(tri_ref[d], gh) + _dot(tri_ref[d], gl) for (d, _), gh, gl in zip(units, g_hi, g_lo)]
    b_end = [bu[end_row[d]:end_row[d] + 1, :] for (d, _), bu in zip(units, b)]

    def finish(o_intra):
        q_in = [(qu * jnp.exp(bu)).astype(BF16) for qu, bu in zip(q, b)]
        k_e = [(ku * jnp.exp(be - bu)).astype(BF16) for ku, bu, be in zip(kk, b, b_end)]
        decay = [jnp.exp(be) for be in b_end]
        kv = [_dot_tn(vu.astype(BF16), ku) * bd for vu, ku in zip(v, k_e)]
        st = [stf_scr[...], stb_scr[...]]
        for u, (d, _) in enumerate(units):
            dirs[d][3][rows[u], :] = (o_intra[u] + _dot_nt(q_in[u], st[d].astype(BF16))).astype(BF16)
            st[d] = st[d] * decay[u] + kv[u]
        stf_scr[...] = st[0]
        stb_scr[...] = st[1]
        for h, hs in enumerate(heads):
            sf_ref[h] = st[0][hs, hs]
            sb_ref[h] = st[1][hs, hs]

    safe = jnp.max(jnp.concatenate([jnp.abs(be) for be in b_end], axis=0)) <= HGRN_SAFE_LOG_DECAY

    @pl.when(safe)
    def _():
        b_mid = [bu[mid_row[d]:mid_row[d] + 1, :] for (d, _), bu in zip(units, b)]
        q_t = [(qu * jnp.exp(bu - bm)).astype(BF16) for qu, bu, bm in zip(q, b, b_mid)]
        k_t = [ku * jnp.exp(bm - bu) for ku, bu, bm in zip(kk, b, b_mid)]
        k_bd = [jnp.concatenate([ku.astype(BF16)] * H_HGRN, axis=0) * hm for ku in k_t]
        v_bd = [jnp.concatenate([vu.astype(BF16)] * H_HGRN, axis=0) * hm for vu in v]
        sc = [(_dot_nt(qu, ku) * tri4_ref[d]).astype(BF16) for (d, _), qu, ku in zip(units, q_t, k_bd)]
        finish([_dot(su, vu) for su, vu in zip(sc, v_bd)])

    @pl.when(jnp.logical_not(safe))
    def _():
        t = lax.broadcasted_iota(jnp.int32, (CHUNK, 1), 0)
        head_sum = bd.astype(BF16)
        for u in range(len(units)):
            ks_scr[u] = kk[u]
            bs_scr[u] = b[u]
            vs_scr[u] = v[u]
            oi_scr[u] = jnp.zeros((CHUNK, W_HGRN), F32)

        def source_row(s, carry):
            for u, (d, _) in enumerate(units):
                seen = (t >= s) if d == 0 else (t <= s)
                e = jnp.exp(jnp.where(seen, b[u] - bs_scr[u, pl.ds(s, 1), :], -jnp.inf))
                p = (q[u] * ks_scr[u, pl.ds(s, 1), :] * e).astype(BF16)
                oi_scr[u] += _dot(p, head_sum) * vs_scr[u, pl.ds(s, 1), :]
            return carry
        lax.fori_loop(0, CHUNK, source_row, 0)
        finish([oi_scr[u] for u in range(len(units))])


def _hgrn_consts():
    t = np.arange(CHUNK)
    tri = np.stack([t[:, None] >= t[None, :], t[:, None] <= t[None, :]]).astype(np.float32)
    tri4 = np.tile(tri, (1, 1, H_HGRN))
    r = np.arange(H_HGRN * CHUNK)
    lane = np.arange(W_HGRN)
    hm = (r[:, None] // CHUNK == lane[None, :] // DK_HGRN).astype(np.float32)
    bd = (lane[:, None] // DV_HGRN == lane[None, :] // DK_HGRN).astype(np.float32)
    return jnp.asarray(tri, BF16), jnp.asarray(tri4), jnp.asarray(hm, BF16), jnp.asarray(bd)


def _hgrn(hg, lb2, s0f, s0b):
    tri, tri4, hm, bd = _hgrn_consts()

    def col(cb, reverse):
        return pl.BlockSpec((T_BLK, W_HGRN), lambda i: (_hgrn_blk(i, reverse), cb))

    def const(shape):
        return pl.BlockSpec(shape, lambda i: (0,) * len(shape))
    state = pl.BlockSpec((None, H_HGRN, DV_HGRN, DK_HGRN), lambda i: (_hgrn_seq(i), 0, 0, 0))
    return pl.pallas_call(
        _hgrn_kernel,
        grid=(N_HGRN_STEPS,),
        in_specs=[col(0, False), col(1, False), col(3, False), col(0, True), col(2, True), col(3, True),
                  const((2, W_HGRN)), state, state,
                  const(tri.shape), const(tri4.shape), const(hm.shape), const(bd.shape)],
        out_specs=[col(0, False), col(0, True), state, state],
        out_shape=[jax.ShapeDtypeStruct((T_ALL, W_HGRN), BF16)] * 2
        + [jax.ShapeDtypeStruct((N_SEQ, H_HGRN, DV_HGRN, DK_HGRN), F32)] * 2,
        scratch_shapes=[pltpu.VMEM((W_HGRN, W_HGRN), F32)] * 2
        + [pltpu.VMEM((2 * T_BLK // CHUNK, CHUNK, W_HGRN), F32)] * 4,
        compiler_params=_cparams(("arbitrary",)),
        name="hgrn",
    )(hg, hg, hg, hg, hg, hg, lb2, s0f, s0b, tri, tri4, hm, bd)


def _attn_kernel(q_ref, k_ref, v_ref, o_ref, *, n_seq=1):
    lane = lax.broadcasted_iota(jnp.int32, (1, HEAD_PAD), 1)
    lq, lk = q_ref.shape[0] // n_seq, k_ref.shape[0] // n_seq
    n_heads = q_ref.shape[1] // HEAD_PAD
    units = [(slice(b * lq, (b + 1) * lq), slice(b * lk, (b + 1) * lk), slice(h * HEAD_PAD, (h + 1) * HEAD_PAD))
             for b in range(n_seq) for h in range(n_heads)]
    s = [_dot_nt(q_ref[qr, hs], k_ref[kr, hs]) for qr, kr, hs in units]
    p = [jnp.exp2(sh - jnp.max(sh, axis=-1, keepdims=True)).astype(BF16) for sh in s]
    pv = [_dot(ph, v_ref[kr, hs]) for ph, (_, kr, hs) in zip(p, units)]
    o = [jnp.where(lane < D_V, x * (1.0 / x[:, D_V:D_V + 1]), 0.0) for x in pv]
    for b in range(n_seq):
        for hp in range(n_heads // 2):
            u = b * n_heads + 2 * hp
            o_ref[units[u][0], hp * 2 * D_V:(hp + 1) * 2 * D_V] = (
                o[u] + pltpu.roll(o[u + 1], D_V, axis=1)).astype(BF16)


def _attn_lat_kernel(q_ref, kp_ref, vp_ref, kn_ref, vn_ref, o_ref, k_scr, v_scr):
    @pl.when(pl.program_id(2) == 0)
    def _():
        k_scr[0:PAST, :] = kp_ref[...]
        k_scr[PAST:, :] = kn_ref[...]
        v_scr[0:PAST, :] = vp_ref[...]
        v_scr[PAST:, :] = vn_ref[...]
    for sub in range(TQ // TQ_SUB):
        rows = pl.ds(sub * TQ_SUB, TQ_SUB)
        _attn_kernel(q_ref.at[rows, :], k_scr, v_scr, o_ref.at[rows, :])


def _attention_ctx(q, k, v):
    rows = CTX_SEQ_PER_STEP * L_CTX
    blk = pl.BlockSpec((rows, W_HEADS), lambda b: (b, 0))
    return pl.pallas_call(
        functools.partial(_attn_kernel, n_seq=CTX_SEQ_PER_STEP),
        grid=(N_CTX_SEQ // CTX_SEQ_PER_STEP,),
        in_specs=[blk, blk, blk],
        out_specs=pl.BlockSpec((rows, W_MLA), lambda b: (b, 0)),
        out_shape=jax.ShapeDtypeStruct((T_CTX, W_MLA), BF16),
        compiler_params=_cparams(("arbitrary",)),
        name="attn_ctx",
    )(q, k, v)


def _attention_lat(q, k, v, k_past, v_past):
    grp = ATT_HEADS * HEAD_PAD
    nq = L_LAT // TQ
    q0 = T_CTX // TQ
    seq0 = T_CTX // L_LAT
    new = pl.BlockSpec((L_LAT, grp), lambda b, hp, qi: (seq0 + b, hp))
    past = pl.BlockSpec((None, PAST, grp), lambda b, hp, qi: (b, 0, hp))
    return pl.pallas_call(
        _attn_lat_kernel,
        grid=(N_LAT_SEQ, H_MLA // ATT_HEADS, nq),
        in_specs=[pl.BlockSpec((TQ, grp), lambda b, hp, qi: (q0 + b * nq + qi, hp)), past, past, new, new],
        out_specs=pl.BlockSpec((TQ, ATT_HEADS * D_V), lambda b, hp, qi: (b * nq + qi, hp)),
        out_shape=jax.ShapeDtypeStruct((T_LAT, W_MLA), BF16),
        scratch_shapes=[pltpu.VMEM((PAST + L_LAT, grp), BF16)] * 2,
        compiler_params=_cparams(("arbitrary", "arbitrary", "arbitrary")),
        name="attn_lat",
    )(q, k_past, v_past, k, v)


def _dft_tables(n):
    a = 2.0 * np.pi * np.outer(np.arange(n), np.arange(n)) / n
    return np.cos(a), np.sin(a)


def _bf16_operand(table):
    return jnp.asarray(table, F32).astype(BF16)


def _fnet_finish(re, im, cs_ref, w_ref, g_ref):
    spec = _dot(jnp.concatenate([re, im], axis=1).astype(BF16), cs_ref[...])
    return _rms_rows(_dot(spec.astype(BF16), w_ref[...]), g_ref[...])


def _fnet_channel_consts(w_bd, g_row):
    c, s = _dft_tables(C_FNET)
    eye = np.eye(G_FNET)
    return [_bf16_operand(np.concatenate([np.kron(eye, c), np.kron(eye, s)], axis=0)), w_bd, g_row]


def _fnet_channel_specs():
    zero = lambda *a: (0, 0)
    return [pl.BlockSpec((2 * W_FNET, W_FNET), zero), pl.BlockSpec((W_FNET, W_FNET), zero),
            pl.BlockSpec((1, W_FNET), zero)]


def _fnet_ctx_kernel(u_ref, m_ref, cs_ref, w_ref, g_ref, o_ref):
    p = [_dot(m_ref[...], u_ref[b * L_CTX:(b + 1) * L_CTX, :].astype(BF16)) for b in range(CTX_SEQ_PER_STEP)]
    o_ref[...] = _fnet_finish(jnp.concatenate([pb[:L_CTX] for pb in p], axis=0),
                              jnp.concatenate([pb[L_CTX:] for pb in p], axis=0), cs_ref, w_ref, g_ref)


def _fnet_ctx(uf, w_bd, g_row):
    c, s = _dft_tables(L_CTX)
    norm = 1.0 / np.sqrt(L_CTX * C_FNET)
    blk = pl.BlockSpec((CTX_SEQ_PER_STEP * L_CTX, W_FNET), lambda b: (b, 0))
    return pl.pallas_call(
        _fnet_ctx_kernel,
        grid=(N_CTX_SEQ // CTX_SEQ_PER_STEP,),
        in_specs=[blk, pl.BlockSpec((2 * L_CTX, L_CTX), lambda b: (0, 0))] + _fnet_channel_specs(),
        out_specs=blk,
        out_shape=jax.ShapeDtypeStruct((T_CTX, W_FNET), F32),
        compiler_params=_cparams(("arbitrary",)),
        name="fnet_ctx",
    )(uf, _bf16_operand(np.concatenate([c, -s], axis=0) * norm), *_fnet_channel_consts(w_bd, g_row))


FN_SUB = 32


def _fnet_rows_kernel(x_ref, m_ref, re_ref, im_ref):
    for s in range(FN_SUB):
        y = _dot(m_ref[s], x_ref[:, s, :].astype(BF16))
        re_ref[s] = y[:GRID_W]
        im_ref[s] = y[GRID_W:]


def _fnet_cols_kernel(re_ref, im_ref, m_ref, cs_ref, w_ref, g_ref, o_ref):
    z = [_dot(m_ref[...], jnp.concatenate([re_ref[:, s, :], im_ref[:, s, :]], axis=0).astype(BF16))
         for s in range(FN_SUB)]
    out = _fnet_finish(jnp.concatenate([zs[:GRID_W] for zs in z], axis=0),
                       jnp.concatenate([zs[GRID_W:] for zs in z], axis=0), cs_ref, w_ref, g_ref)
    for s in range(FN_SUB):
        o_ref[:, s, :] = out[s * GRID_W:(s + 1) * GRID_W]


def _fnet_lat(uf, w_bd, g_row):
    p1 = np.arange(GRID_W)[None, :, None]
    l = GRID_W * np.arange(GRID_W)[None, None, :] + np.arange(GRID_W)[:, None, None]
    ang = 2.0 * np.pi * p1 * l / L_LAT
    norm = 1.0 / np.sqrt(L_LAT * C_FNET)
    m_rows = _bf16_operand(np.concatenate([np.cos(ang), -np.sin(ang)], axis=1) * norm)
    c64, s64 = _dft_tables(GRID_W)
    m_cols = _bf16_operand(np.block([[c64, s64], [-s64, c64]]))
    x4 = uf.reshape(T_ALL // L_LAT, GRID_W, GRID_W, W_FNET)
    seq0 = T_CTX // L_LAT
    grid = (N_LAT_SEQ, GRID_W // FN_SUB)
    mid = pl.BlockSpec((None, FN_SUB, GRID_W, W_FNET), lambda b, j: (b, j, 0, 0))
    mid_shape = jax.ShapeDtypeStruct((N_LAT_SEQ, GRID_W, GRID_W, W_FNET), F32)
    re, im = pl.pallas_call(
        _fnet_rows_kernel,
        grid=grid,
        in_specs=[pl.BlockSpec((None, GRID_W, FN_SUB, W_FNET), lambda b, j: (seq0 + b, 0, j, 0)),
                  pl.BlockSpec((FN_SUB, 2 * GRID_W, GRID_W), lambda b, j: (j, 0, 0))],
        out_specs=[mid, mid],
        out_shape=[mid_shape, mid_shape],
        compiler_params=_cparams(("arbitrary", "arbitrary")),
        name="fnet_rows_lat",
    )(x4, m_rows)
    strided = pl.BlockSpec((None, GRID_W, FN_SUB, W_FNET), lambda b, j: (b, 0, j, 0))
    out = pl.pallas_call(
        _fnet_cols_kernel,
        grid=grid,
        in_specs=[strided, strided, pl.BlockSpec((2 * GRID_W, 2 * GRID_W), lambda b, j: (0, 0))]
        + _fnet_channel_specs(),
        out_specs=strided,
        out_shape=mid_shape,
        compiler_params=_cparams(("arbitrary", "arbitrary")),
        name="fnet_cols_lat",
    )(re, im, m_cols, *_fnet_channel_consts(w_bd, g_row))
    return out.reshape(T_LAT, W_FNET)


def _pad_heads(w, d_head):
    lead = w.shape[:-1]
    w = w.reshape(lead + (H_MLA, d_head))
    w = jnp.pad(w, [(0, 0)] * len(lead) + [(0, 0), (0, HEAD_PAD - d_head)])
    return w.reshape(lead + (W_HEADS,))


def _swap_rope(a):
    n = D_ROPE // 4
    parts = [jnp.zeros(a.shape[:-1] + (D_NOPE,), a.dtype)]
    for ax in range(2):
        base = D_NOPE + ax * 2 * n
        parts += [a[..., base + n:base + 2 * n], a[..., base:base + n]]
    parts.append(jnp.zeros(a.shape[:-1] + (HEAD_PAD - D_QK,), a.dtype))
    return jnp.concatenate(parts, axis=-1)


def _rope_tables(g_q, g_k):
    n_freq = D_ROPE // 4
    t = np.arange(L_LAT)
    pos = np.stack([t // GRID_W, t % GRID_W], axis=-1).astype(np.float32)
    freq = (np.float32(ROPE_THETA) ** (-np.arange(n_freq, dtype=np.float32) / n_freq)).astype(np.float32)
    ang = (pos[:, :, None] * freq).astype(np.float32)
    cos = np.ones((L_LAT, HEAD_PAD), np.float32)
    sin = np.zeros((L_LAT, HEAD_PAD), np.float32)
    for ax in range(2):
        base = D_NOPE + ax * 2 * n_freq
        c, s = np.cos(ang[:, ax, :]), np.sin(ang[:, ax, :])
        cos[:, base:base + n_freq] = c
        cos[:, base + n_freq:base + 2 * n_freq] = c
        sin[:, base:base + n_freq] = -s
        sin[:, base + n_freq:base + 2 * n_freq] = s
    cos = np.concatenate([np.ones((TM, HEAD_PAD), np.float32), cos]).reshape(1, 1 + LAT_TILES_PER_SEQ, TM, HEAD_PAD)
    sin = np.concatenate([np.zeros((TM, HEAD_PAD), np.float32), sin]).reshape(1, 1 + LAT_TILES_PER_SEQ, TM, HEAD_PAD)

    def pair(g, scale):
        g128 = jnp.pad(g, ((0, 0), (0, HEAD_PAD - D_QK))) * scale
        g_partner = _swap_rope(g128)
        return (cos * g128[:, None, None, :], sin * g_partner[:, None, None, :])
    return pair(g_q, D_QK ** -0.5 * LOG2_E) + pair(g_k, 1.0)


def kernel(x_prompt, x_sample, cache_ckv, cache_krope, state_hgrn, c, c_ctx, ada_w, ada_b, norm_g, ffn_w_gu, ffn_w_down, w_in, hgrn_lb, hgrn_norm_g, mla_q_norm_g, mla_w_q_up, mla_kv_norm_g, mla_w_kv_up, mla_qk_norm_g, mla_out_norm_g, fnet_w, fnet_norm_g, w_out):
    w_gu = ffn_w_gu.astype(BF16)
    w_down = ffn_w_down.astype(BF16)
    o = np.cumsum((0, 5 * W_HGRN, Q_RANK, KV_RANK, D_ROPE, W_FNET))
    assert o[3] == IN_MAIN
    w_in_b = w_in.astype(BF16)
    w_kr = jnp.pad(w_in[:, :, o[3]:o[4]], ((0, 0), (0, 0), (D_NOPE, HEAD_PAD - D_NOPE - D_ROPE)))
    w_tail = jnp.concatenate([w_kr, _swap_rope(w_kr), w_in[:, :, o[4]:]], axis=-1).astype(BF16)
    wq = _pad_heads(mla_w_q_up, D_QK)
    wq_sw = _swap_rope(wq.reshape(DEPTH, Q_RANK, H_MLA, HEAD_PAD)).reshape(DEPTH, Q_RANK, W_HEADS)
    wq2 = jnp.concatenate([wq, wq_sw], axis=-1).astype(BF16)
    w_kv = mla_w_kv_up.reshape(DEPTH, KV_RANK, H_MLA, D_NOPE + D_V)
    wk_arr = _pad_heads(w_kv[..., :D_NOPE].reshape(DEPTH, KV_RANK, H_MLA * D_NOPE), D_NOPE).astype(BF16)
    wv_arr = _pad_heads(w_kv[..., D_NOPE:].reshape(DEPTH, KV_RANK, H_MLA * D_V), D_V).astype(BF16)
    w_out_b = w_out.astype(BF16)
    eye_g = jnp.eye(G_FNET, dtype=F32)
    fnet_bd = jnp.einsum("lgcd,gh->lgchd", fnet_w, eye_g).reshape(DEPTH, W_FNET, W_FNET).astype(BF16)
    tabs = _rope_tables(mla_qk_norm_g[:, 0], mla_qk_norm_g[:, 1])
    lbs = jnp.cumsum(jax.nn.softmax(hgrn_lb.astype(F32), axis=0), axis=0)
    lbs = lbs - lbs[:1]

    cond8 = jnp.zeros((8, D), F32).at[0].set(c_ctx).at[1:1 + N_LAT_SEQ].set(c)
    mods = _mods(cond8, ada_w, ada_b).reshape(DEPTH, 8, N_MOD, D)

    x = (x_prompt.reshape(T_CTX, D), x_sample.reshape(T_LAT, D))
    ckv_out, kr_out, st_out = [], [], []
    zero_states = jnp.zeros((N_CTX_SEQ, H_HGRN, DV_HGRN, DK_HGRN), F32)
    for l in range(DEPTH):
        x = _ffn(x, mods, norm_g, w_gu, w_down, l, 0)
        hg, ckvn, kr128, uf, q, k, v = _inproj(x, mods, norm_g, w_in_b, w_tail, mla_q_norm_g, mla_kv_norm_g, wq2, wk_arr,
                                               wv_arr, tabs, l)

        s0 = [jnp.concatenate([zero_states, jnp.swapaxes(state_hgrn[:, l, d], -1, -2)], axis=0) for d in range(2)]
        h_fwd, h_bwd, s_f, s_b = _hgrn(hg, lbs[l], s0[0], s0[1])
        gn_row = jnp.tile(hgrn_norm_g[l], H_HGRN).reshape(1, W_HGRN)
        st_out.append(jnp.swapaxes(jnp.stack([s_f[:N_CTX_SEQ], s_b[:N_CTX_SEQ]], axis=1), -1, -2))

        cache_kr = jnp.pad(cache_krope[:, l], ((0, 0), (0, 0), (D_NOPE, HEAD_PAD - D_NOPE - D_ROPE)))
        k_past, v_past = _kv_cache(cache_ckv[:, l], cache_kr, _swap_rope(cache_kr), wk_arr, wv_arr, tabs, l)
        att_ctx = _attention_ctx(q, k, v)
        att_lat = _attention_lat(q, k, v, k_past, v_past)

        fn_gain = fnet_norm_g[l].reshape(1, W_FNET)
        fn_ctx = _fnet_ctx(uf, fnet_bd[l], fn_gain)
        fn_lat = _fnet_lat(uf, fnet_bd[l], fn_gain)

        dx = _outproj(mods, h_fwd, h_bwd, hg, gn_row, att_ctx, att_lat, fn_ctx, fn_lat, mla_out_norm_g, w_out_b, l)
        if l == DEPTH - 1:
            y_prompt = _ffn(x, mods, norm_g, w_gu, w_down, l, 1, delta=dx, n_rows=T_CTX)
            y_sample = _ffn(x, mods, norm_g, w_gu, w_down, l, 1, delta=dx, src_row0=T_CTX, n_rows=T_LAT)
        else:
            x = _ffn(x, mods, norm_g, w_gu, w_down, l, 1, delta=dx)

        ckv_out.append(ckvn[:T_CTX].reshape(N_CTX_SEQ, L_CTX, KV_RANK))
        kr_out.append(kr128[:T_CTX, D_NOPE:D_NOPE + D_ROPE].reshape(N_CTX_SEQ, L_CTX, D_ROPE))

    return (y_prompt.reshape(N_CTX_SEQ, L_CTX, D), y_sample.reshape(N_LAT_SEQ, L_LAT, D),
            jnp.stack(ckv_out, axis=1), jnp.stack(kr_out, axis=1), jnp.stack(st_out, axis=1))
```
